```python
import functools
import jax, jax.numpy as jnp
from jax import lax
import numpy as np

D_MODEL = 2048
BATCH = 4
SEQ = 2048
DEPTH = 1
DEC_BATCH = 32
DEC_SEQ = 1
PAST_LEN = 8192
PAGE_SIZE = 128

H_R = 4
DK_R = 256
DV_R = 256
W_R = H_R * DV_R
CHUNK = 128
ROPE_BASE = 10000.0
GN_EPS = 1e-6
H_F = 8
D_F = 128
W_F = H_F * D_F
Q_BLOCK = 128
D_FF = -(-8 * D_MODEL // (3 * 256)) * 256
D_PLE = 256
LN_EPS = 1e-5
ALPHA = (2.0 * DEPTH) ** 0.25
BETA = (8.0 * DEPTH) ** -0.25
IN_SIZES = (H_R * DK_R, H_R * DK_R, H_R * DV_R, H_R * DV_R, W_F, W_F, W_F, H_F, D_MODEL, D_MODEL)
N_IN = sum(IN_SIZES)

kernel_name = 'hybrid_retention_fox_decoder_step'


def layer_norm(x, g, b):
    xf = x.astype(jnp.float32)
    mu = xf.mean(-1, keepdims=True)
    var = jnp.square(xf - mu).mean(-1, keepdims=True)
    return ((xf - mu) * lax.rsqrt(var + LN_EPS) * g + b).astype(x.dtype)


def rope(x, pos):
    half = x.shape[-1] // 2
    inv = ROPE_BASE ** (-jnp.arange(half, dtype=jnp.float32) / half)
    ang = pos.astype(jnp.float32)[:, None] * inv[None, :]
    cos = jnp.cos(ang)[None, :, None, :]
    sin = jnp.sin(ang)[None, :, None, :]
    x = x.astype(jnp.float32)
    x1, x2 = x[..., :half], x[..., half:]
    return jnp.concatenate([x1 * cos - x2 * sin, x2 * cos + x1 * sin], axis=-1)


def branch_inputs(h, w_in, b_f, pos):
    b, l = h.shape[0], h.shape[1]
    z = h @ w_in
    cuts = [int(c) for c in np.cumsum(IN_SIZES)[:-1]]
    rq, rk, rv, rg, fq, fk, fv, ff, gr, gf = jnp.split(z, cuts, axis=-1)
    rq = rope(rq.reshape(b, l, H_R, DK_R), pos)
    rk = rope(rk.reshape(b, l, H_R, DK_R), pos) * (DK_R ** -0.5)
    rv = rv.reshape(b, l, H_R, DV_R)
    fq = fq.reshape(b, l, H_F, D_F)
    fk = fk.reshape(b, l, H_F, D_F)
    fv = fv.reshape(b, l, H_F, D_F)
    logf = jax.nn.log_sigmoid(ff.astype(jnp.float32) + b_f.astype(jnp.float32))
    return rq, rk, rv, rg, fq, fk, fv, logf, gr, gf


def retention_chunk(q, k, v, state, log_g):
    q = q.astype(jnp.float32)
    k = k.astype(jnp.float32)
    v = v.astype(jnp.float32)
    state = state.astype(jnp.float32)
    l = q.shape[1]
    i = jnp.arange(l, dtype=jnp.float32)
    diff = i[:, None] - i[None, :]
    causal = diff >= 0
    dec = jnp.where(causal[None], jnp.exp(log_g[:, None, None] * jnp.where(causal, diff, 0.0)[None]), 0.0)
    scores = jnp.einsum('bihd,bjhd->bhij', q, k) * dec[None]
    o = jnp.einsum('bhij,bjhe->bihe', scores, v)
    cross = jnp.exp(log_g[None, :] * (i[:, None] + 1.0))
    o = o + jnp.einsum('bihd,bhde->bihe', q, state) * cross[None, :, :, None]
    kdec = jnp.exp(log_g[None, :] * (l - 1.0 - i)[:, None])
    new_state = jnp.exp(log_g * l)[None, :, None, None] * state + jnp.einsum('bjhd,bjhe->bhde', k * kdec[None, :, :, None], v)
    return o, new_state


def retention_prompt(q, k, v, log_g):
    b, s = q.shape[0], q.shape[1]
    nc = s // CHUNK

    def chunks(t):
        return jnp.swapaxes(t.reshape((b, nc, CHUNK) + t.shape[2:]), 0, 1)

    def step(state, qkv):
        o, state = retention_chunk(qkv[0], qkv[1], qkv[2], state, log_g)
        return state, o

    s0 = jnp.zeros((b, H_R, DK_R, DV_R), jnp.float32)
    s_fin, o = lax.scan(step, s0, (chunks(q), chunks(k), chunks(v)))
    return jnp.swapaxes(o, 0, 1).reshape(b, s, H_R, DV_R), s_fin


def fox_prompt(q, k, v, logf):
    b, s = q.shape[0], q.shape[1]
    nb = s // Q_BLOCK
    c_t = jnp.swapaxes(jnp.cumsum(logf, axis=1), 1, 2)
    kpos = jnp.arange(s)
    scale = D_F ** -0.5

    def block(bi):
        start = bi * Q_BLOCK
        qb = lax.dynamic_slice_in_dim(q, start, Q_BLOCK, axis=1)
        cb = lax.dynamic_slice_in_dim(c_t, start, Q_BLOCK, axis=2)
        sc = jnp.einsum('bqhd,bkhd->bhqk', qb, k).astype(jnp.float32) * scale + cb[..., None] - c_t[:, :, None, :]
        qpos = start + jnp.arange(Q_BLOCK)
        sc = jnp.where((kpos[None, :] <= qpos[:, None])[None, None], sc, -jnp.inf)
        p = jax.nn.softmax(sc, axis=-1)
        return jnp.einsum('bhqk,bkhd->bqhd', p.astype(v.dtype), v)

    o = lax.map(block, jnp.arange(nb))
    return jnp.swapaxes(o, 0, 1).reshape(b, s, H_F, D_F)


def fox_sample(q, k_new, v_new, logf_new, k_past, v_past, logf_past):
    p_len, t = k_past.shape[1], q.shape[1]
    c = jnp.cumsum(jnp.concatenate([logf_past.astype(jnp.float32), logf_new.astype(jnp.float32)], axis=1), axis=1)
    c_t = jnp.swapaxes(c, 1, 2)
    c_q = c_t[:, :, p_len:]
    s_past = jnp.einsum('bqhd,bkhd->bhqk', q, k_past).astype(jnp.float32)
    s_new = jnp.einsum('bqhd,bkhd->bhqk', q, k_new).astype(jnp.float32)
    sc = jnp.concatenate([s_past, s_new], axis=-1) * (D_F ** -0.5) + c_q[..., None] - c_t[:, :, None, :]
    mask = jnp.concatenate([jnp.ones((t, p_len), bool), jnp.tril(jnp.ones((t, t), bool))], axis=1)
    sc = jnp.where(mask[None, None], sc, -jnp.inf)
    p = jax.nn.softmax(sc, axis=-1).astype(v_new.dtype)
    return jnp.einsum('bhqk,bkhd->bqhd', p[..., :p_len], v_past) + jnp.einsum('bhqk,bkhd->bqhd', p[..., p_len:], v_new)


def attend_prompt(rq, rk, rv, fq, fk, fv, logf, log_g):
    o_r, s_new = retention_prompt(rq, rk, rv, log_g)
    return o_r, s_new, fox_prompt(fq, fk, fv, logf)


def attend_sample(rq, rk, rv, fq, fk, fv, logf, log_g, state, k_past, v_past, logf_past):
    o_r, s_new = retention_chunk(rq, rk, rv, state, log_g)
    return o_r, s_new, fox_sample(fq, fk, fv, logf, k_past, v_past, logf_past)


def trunk_layer(h, p, pos, attend, w_in, b_f, gn_g, w_ret_proj, w_fox_proj, w_out,
                ln1_g, ln1_b, w1, w3, w2, w_pg, w_pe, ln2_g, ln2_b):
    b, l = h.shape[0], h.shape[1]
    rq, rk, rv, rg, fq, fk, fv, logf, gr, gf = branch_inputs(h, w_in, b_f, pos)
    o_r, s_new, o_f = attend(rq, rk, rv, fq, fk, fv, logf)
    mu = o_r.mean(-1, keepdims=True)
    var = jnp.square(o_r - mu).mean(-1, keepdims=True)
    o_r = (o_r - mu) * lax.rsqrt(var + GN_EPS) * gn_g.astype(jnp.float32) * jax.nn.silu(rg.astype(jnp.float32).reshape(b, l, H_R, DV_R))
    y_r = o_r.reshape(b, l, W_R).astype(h.dtype) @ w_ret_proj
    y_f = o_f.reshape(b, l, W_F).astype(h.dtype) @ w_fox_proj
    mix = (jax.nn.sigmoid(gr) * y_r + jax.nn.sigmoid(gf) * y_f) @ w_out
    x1 = layer_norm(ALPHA * h + mix, ln1_g, ln1_b)
    ffn = (jax.nn.silu(x1 @ w1) * (x1 @ w3)) @ w2
    ple = jax.nn.sigmoid(x1 @ w_pg) * (p.astype(h.dtype) @ w_pe)
    y = layer_norm(ALPHA * x1 + ffn + ple, ln2_g, ln2_b)
    return y, (s_new, fk, fv, logf)


def setup_inputs(seed: int = 0) -> dict:
    key = jax.random.key(seed)
    ks = jax.random.split(key, 32)
    f32 = jnp.float32
    n_pages = PAST_LEN // PAGE_SIZE
    n_pool = (5 * DEC_BATCH * n_pages + 3) // 4

    def nrm(k, shape, scale=1.0):
        return jax.random.normal(k, shape, f32) * scale

    col_scale = np.concatenate([np.full((s,), BETA if i in (2, 6) else 1.0, np.float32) for i, s in enumerate(IN_SIZES)])
    page_table = jax.random.permutation(ks[7], n_pool)[: DEC_BATCH * n_pages].reshape(DEC_BATCH, n_pages).astype(jnp.int32)
    return {
        'x_prompt': nrm(ks[0], (BATCH, SEQ, D_MODEL)),
        'x_sample': nrm(ks[1], (DEC_BATCH, DEC_SEQ, D_MODEL)),
        'state_ret': nrm(ks[2], (DEPTH, DEC_BATCH, H_R, DK_R, DV_R), 0.3),
        'cache_k': nrm(ks[3], (DEPTH, n_pool, PAGE_SIZE, H_F, D_F)),
        'cache_v': nrm(ks[4], (DEPTH, n_pool, PAGE_SIZE, H_F, D_F), BETA),
        'cache_logf': jax.nn.log_sigmoid(2.0 + nrm(ks[5], (DEPTH, n_pool, PAGE_SIZE, H_F))),
        'page_table': page_table,
        'p_prompt': nrm(ks[8], (DEPTH, BATCH, SEQ, D_PLE)),
        'p_sample': nrm(ks[9], (DEPTH, DEC_BATCH, DEC_SEQ, D_PLE)),
        'ln_emb_g': 1.0 + nrm(ks[10], (D_MODEL,), 0.02),
        'ln_emb_b': nrm(ks[11], (D_MODEL,), 0.02),
        'w_in': nrm(ks[12], (DEPTH, D_MODEL, N_IN), D_MODEL ** -0.5) * jnp.asarray(col_scale),
        'b_f': 2.0 + nrm(ks[13], (DEPTH, H_F), 0.1),
        'gn_g': 1.0 + nrm(ks[14], (DEPTH, H_R, DV_R), 0.02),
        'w_ret_proj': nrm(ks[15], (DEPTH, W_R, D_MODEL), BETA * W_R ** -0.5),
        'w_fox_proj': nrm(ks[16], (DEPTH, W_F, D_MODEL), BETA * W_F ** -0.5),
        'w_out': nrm(ks[17], (DEPTH, D_MODEL, D_MODEL), BETA * D_MODEL ** -0.5),
        'ln1_g': 1.0 + nrm(ks[18], (DEPTH, D_MODEL), 0.02),
        'ln1_b': nrm(ks[19], (DEPTH, D_MODEL), 0.02),
        'w1': nrm(ks[20], (DEPTH, D_MODEL, D_FF), BETA * D_MODEL ** -0.5),
        'w3': nrm(ks[21], (DEPTH, D_MODEL, D_FF), BETA * D_MODEL ** -0.5),
        'w2': nrm(ks[22], (DEPTH, D_FF, D_MODEL), BETA * D_FF ** -0.5),
        'w_pg': nrm(ks[23], (DEPTH, D_MODEL, D_MODEL), D_MODEL ** -0.5),
        'w_pe': nrm(ks[24], (DEPTH, D_PLE, D_MODEL), BETA * D_PLE ** -0.5),
        'ln2_g': 1.0 + nrm(ks[25], (DEPTH, D_MODEL), 0.02),
        'ln2_b': nrm(ks[26], (DEPTH, D_MODEL), 0.02),
    }


def reference(x_prompt, x_sample, state_ret, cache_k, cache_v, cache_logf, page_table, p_prompt, p_sample,
              ln_emb_g, ln_emb_b, w_in, b_f, gn_g, w_ret_proj, w_fox_proj, w_out, ln1_g, ln1_b,
              w1, w3, w2, w_pg, w_pe, ln2_g, ln2_b):
    dec_b, n_pages = page_table.shape
    past_len = n_pages * PAGE_SIZE
    pos_p = jnp.arange(x_prompt.shape[1])
    pos_s = past_len + jnp.arange(x_sample.shape[1])
    log_g = jnp.log1p(-jnp.exp2(-5.0 - jnp.arange(H_R, dtype=jnp.float32)))
    h_p = layer_norm(x_prompt, ln_emb_g, ln_emb_b)
    h_s = layer_norm(x_sample, ln_emb_g, ln_emb_b)
    outs_p = []
    outs_s = []
    for l in range(DEPTH):
        weights = (w_in[l], b_f[l], gn_g[l], w_ret_proj[l], w_fox_proj[l], w_out[l], ln1_g[l], ln1_b[l],
                   w1[l], w3[l], w2[l], w_pg[l], w_pe[l], ln2_g[l], ln2_b[l])
        h_p, st_p = trunk_layer(h_p, p_prompt[l], pos_p, functools.partial(attend_prompt, log_g=log_g), *weights)
        k_past = cache_k[l][page_table].reshape(dec_b, past_len, H_F, D_F)
        v_past = cache_v[l][page_table].reshape(dec_b, past_len, H_F, D_F)
        logf_past = cache_logf[l][page_table].reshape(dec_b, past_len, H_F)
        attend_s = functools.partial(attend_sample, log_g=log_g, state=state_ret[l],
                                     k_past=k_past, v_past=v_past, logf_past=logf_past)
        h_s, st_s = trunk_layer(h_s, p_sample[l], pos_s, attend_s, *weights)
        outs_p.append(st_p)
        outs_s.append(st_s)
    ret_state_prompt = jnp.stack([o[0] for o in outs_p])
    fox_k_prompt = jnp.stack([o[1] for o in outs_p])
    fox_v_prompt = jnp.stack([o[2] for o in outs_p])
    fox_logf_prompt = jnp.stack([o[3] for o in outs_p])
    ret_state_sample = jnp.stack([o[0] for o in outs_s])
    fox_k_sample = jnp.stack([o[1] for o in outs_s])
    fox_v_sample = jnp.stack([o[2] for o in outs_s])
    fox_logf_sample = jnp.stack([o[3] for o in outs_s])
    return (h_p, h_s, ret_state_prompt, fox_k_prompt, fox_v_prompt, fox_logf_prompt,
            ret_state_sample, fox_k_sample, fox_v_sample, fox_logf_sample)
```

```python
import functools

import jax
import jax.numpy as jnp
import numpy as np
from jax import lax
from jax.experimental import pallas as pl
from jax.experimental.pallas import tpu as pltpu

F32 = jnp.float32
BF16 = jnp.bfloat16

D_MODEL = 2048
DEPTH = 1
PAGE_SIZE = 128
H_R = 4
DK_R = 256
DV_R = 256
W_R = H_R * DV_R
CHUNK = 128
ROPE_BASE = 10000.0
GN_EPS = 1e-6
H_F = 8
D_F = 128
W_F = H_F * D_F
D_FF = 5632
D_PLE = 256
LN_EPS = 1e-5
ALPHA = (2.0 * DEPTH) ** 0.25
FOX_SCALE = D_F ** -0.5

COL_BLOCK = 1024
N_MAIN = 11 * COL_BLOCK
CB_RQ, CB_RK, CB_RV, CB_RG, CB_FQ, CB_FK, CB_FV, CB_GR, CB_GF = 0, 1, 2, 3, 4, 5, 6, 7, 9
FF_LO, FF_HI = 7 * COL_BLOCK, 7 * COL_BLOCK + H_F

VMEM_LIMIT = 56 * 1024 * 1024
DECODE_PAGES_PER_STEP = 8

NT_DIMS = (((1,), (1,)), ((), ()))
TN_DIMS = (((0,), (0,)), ((), ()))


def _params(semantics):
    return pltpu.CompilerParams(dimension_semantics=semantics, vmem_limit_bytes=VMEM_LIMIT)


def _layer_norm(x, g, b):
    mu = jnp.mean(x, axis=-1, keepdims=True)
    xc = x - mu
    var = jnp.mean(xc * xc, axis=-1, keepdims=True)
    return xc * lax.rsqrt(var + LN_EPS) * g + b


def _sigmoid(x):
    return 1.0 / (1.0 + jnp.exp(-x))


def _log_sigmoid(x):
    return jnp.minimum(x, 0.0) - jnp.log(1.0 + jnp.exp(-jnp.abs(x)))


def _resident(shape):
    return pl.BlockSpec(shape, lambda *_: (0,) * len(shape), pipeline_mode=pl.Buffered(1))


def _ln_kernel(x_ref, g_ref, b_ref, o_ref):
    o_ref[...] = _layer_norm(x_ref[...], g_ref[...], b_ref[...]).astype(o_ref.dtype)


def _entry_ln(x, g, b, tm):
    m = x.shape[0]
    return pl.pallas_call(
        _ln_kernel,
        grid=(m // tm,),
        in_specs=[pl.BlockSpec((tm, D_MODEL), lambda i: (i, 0)),
                  pl.BlockSpec((1, D_MODEL), lambda i: (0, 0)),
                  pl.BlockSpec((1, D_MODEL), lambda i: (0, 0))],
        out_specs=pl.BlockSpec((tm, D_MODEL), lambda i: (i, 0)),
        out_shape=jax.ShapeDtypeStruct((m, D_MODEL), BF16),
        compiler_params=_params(("parallel",)),
        name="entry_ln",
    )(x, g, b)


def _inproj_kernel(h_ref, w_ref, cos_ref, sin_ref, z_ref, fk_ref, fv_ref):
    j = pl.program_id(1)
    acc = jnp.dot(h_ref[...], w_ref[...], preferred_element_type=F32)

    @pl.when(j <= CB_RK)
    def _():
        cos = cos_ref[...]
        sin = sin_ref[...]
        scale = jnp.where(j == CB_RK, DK_R ** -0.5, 1.0).astype(F32)
        half = DK_R // 2
        for hd in range(H_R):
            lo = hd * DK_R
            x1 = acc[:, lo:lo + half]
            x2 = acc[:, lo + half:lo + DK_R]
            z_ref[:, lo:lo + half] = ((x1 * cos - x2 * sin) * scale).astype(BF16)
            z_ref[:, lo + half:lo + DK_R] = ((x2 * cos + x1 * sin) * scale).astype(BF16)

    @pl.when(j > CB_RK)
    def _():
        z_ref[...] = acc.astype(BF16)

    @pl.when(j == CB_FK)
    def _():
        fk_ref[...] = acc

    @pl.when(j == CB_FV)
    def _():
        fv_ref[...] = acc


def _inproj(h, w_main, cos, sin, tm):
    m = h.shape[0]
    n_pos_blocks = cos.shape[0] // tm
    return pl.pallas_call(
        _inproj_kernel,
        grid=(m // tm, N_MAIN // COL_BLOCK),
        in_specs=[pl.BlockSpec((tm, D_MODEL), lambda i, j: (i, 0)),
                  pl.BlockSpec((D_MODEL, COL_BLOCK), lambda i, j: (0, j)),
                  pl.BlockSpec((tm, DK_R // 2), lambda i, j: (i % n_pos_blocks, 0)),
                  pl.BlockSpec((tm, DK_R // 2), lambda i, j: (i % n_pos_blocks, 0))],
        out_specs=[pl.BlockSpec((tm, COL_BLOCK), lambda i, j: (i, j)),
                   pl.BlockSpec((tm, COL_BLOCK), lambda i, j: (i, 0)),
                   pl.BlockSpec((tm, COL_BLOCK), lambda i, j: (i, 0))],
        out_shape=[jax.ShapeDtypeStruct((m, N_MAIN), BF16),
                   jax.ShapeDtypeStruct((m, W_F), F32),
                   jax.ShapeDtypeStruct((m, W_F), F32)],
        compiler_params=_params(("parallel", "arbitrary")),
        name="in_proj",
    )(h, w_main, cos, sin)


def _cumsum_lanes(x):
    n = x.shape[-1]
    lane = lax.broadcasted_iota(jnp.int32, x.shape, x.ndim - 1)
    s = 1
    while s < n:
        x = x + jnp.where(lane >= s, pltpu.roll(x, s, axis=x.ndim - 1), 0.0)
        s *= 2
    return x


def _logf_kernel(h_ref, w_ref, bf_ref, lf_ref, c_ref=None):
    ff = lax.dot_general(w_ref[...], h_ref[...], NT_DIMS, preferred_element_type=F32)
    lf = _log_sigmoid(ff + bf_ref[...])
    lf_ref[0] = lf
    if c_ref is not None:
        c_ref[0] = _cumsum_lanes(lf)


def _logf(h, w_ff_t, bf_col, seq, with_cumsum):
    m = h.shape[0]
    nb = m // seq
    shape = jax.ShapeDtypeStruct((nb, H_F, seq), F32)
    spec = pl.BlockSpec((1, H_F, seq), lambda b: (b, 0, 0))
    return pl.pallas_call(
        _logf_kernel,
        grid=(nb,),
        in_specs=[pl.BlockSpec((seq, D_MODEL), lambda b: (b, 0)),
                  pl.BlockSpec((H_F, D_MODEL), lambda b: (0, 0)),
                  pl.BlockSpec((H_F, 1), lambda b: (0, 0))],
        out_specs=[spec, spec] if with_cumsum else spec,
        out_shape=[shape, shape] if with_cumsum else shape,
        compiler_params=_params(("parallel",)),
        name="log_forget_cumsum" if with_cumsum else "log_forget",
    )(h, w_ff_t, bf_col)


def _group_norm_gate(o, gate_bf16, gn_row):
    mu = jnp.mean(o, axis=-1, keepdims=True)
    oc = o - mu
    var = jnp.mean(oc * oc, axis=-1, keepdims=True)
    gate = gate_bf16.astype(F32)
    return oc * lax.rsqrt(var + GN_EPS) * gn_row * (gate * _sigmoid(gate))


def _ret_prompt_kernel(lg_ref, q_ref, k_ref, v_ref, g_ref, gn_ref, o_ref, st_ref):
    c = pl.program_id(1)

    @pl.when(c == 0)
    def _():
        st_ref[...] = jnp.zeros_like(st_ref)

    n = q_ref.shape[0]
    ri = lax.broadcasted_iota(jnp.int32, (n, n), 0)
    ci = lax.broadcasted_iota(jnp.int32, (n, n), 1)
    causal = ri >= ci
    diff = jnp.where(causal, (ri - ci).astype(F32), 0.0)
    pos = lax.broadcasted_iota(jnp.int32, (n, DK_R), 0).astype(F32)
    for hd in range(H_R):
        lg = lg_ref[hd]
        sl = slice(hd * DK_R, (hd + 1) * DK_R)
        q = q_ref[:, sl]
        k = k_ref[:, sl]
        v = v_ref[:, sl]
        dec = jnp.where(causal, jnp.exp(lg * diff), 0.0)
        sc = lax.dot_general(q, k, NT_DIMS, preferred_element_type=F32) * dec
        o = jnp.dot(sc.astype(BF16), v, preferred_element_type=F32)
        state = st_ref[0, hd]
        cross = jnp.exp(lg * (pos + 1.0))
        o = o + jnp.dot(q, state.astype(BF16), preferred_element_type=F32) * cross
        kdec = jnp.exp(lg * (n - 1.0 - pos))
        kd = (k.astype(F32) * kdec).astype(BF16)
        carry = jnp.exp(jnp.full((1, DV_R), lg * n, F32))
        st_ref[0, hd] = carry * state + lax.dot_general(kd, v, TN_DIMS, preferred_element_type=F32)
        o_ref[:, sl] = _group_norm_gate(o, g_ref[:, sl], gn_ref[hd:hd + 1, :]).astype(BF16)


def _ret_prompt(z, log_g, gn_g, batch, seq):
    nc = seq // CHUNK
    row = lambda b, c: b * nc + c
    zspec = lambda cb: pl.BlockSpec((CHUNK, COL_BLOCK), lambda b, c: (row(b, c), cb))
    return pl.pallas_call(
        _ret_prompt_kernel,
        grid=(batch, nc),
        in_specs=[pl.BlockSpec(memory_space=pltpu.SMEM),
                  zspec(CB_RQ), zspec(CB_RK), zspec(CB_RV), zspec(CB_RG),
                  pl.BlockSpec((H_R, DV_R), lambda b, c: (0, 0))],
        out_specs=[pl.BlockSpec((CHUNK, W_R), lambda b, c: (row(b, c), 0)),
                   pl.BlockSpec((1, H_R, DK_R, DV_R), lambda b, c: (b, 0, 0, 0))],
        out_shape=[jax.ShapeDtypeStruct((batch * seq, W_R), BF16),
                   jax.ShapeDtypeStruct((batch, H_R, DK_R, DV_R), F32)],
        compiler_params=_params(("parallel", "arbitrary")),
        name="retention_prompt",
    )(log_g, z, z, z, z, gn_g)


def _ret_sample_kernel(lg_ref, q_ref, k_ref, v_ref, g_ref, gn_ref, st_ref, o_ref, ns_ref):
    row0 = lax.broadcasted_iota(jnp.int32, (8, DK_R), 0) == 0
    for hd in range(H_R):
        lg = lg_ref[hd]
        sl = slice(hd * DK_R, (hd + 1) * DK_R)
        q = q_ref[0, :, sl]
        k = k_ref[0, :, sl]
        v = v_ref[0, :, sl]
        qf = q.astype(F32)
        kf = k.astype(F32)
        vf = v.astype(F32)
        state = st_ref[0, hd]
        decay = jnp.exp(jnp.full((1, DV_R), lg, F32))
        score = jnp.sum(qf * kf, axis=-1, keepdims=True)
        q8 = jnp.broadcast_to(q, (8, DK_R))
        qs = jnp.dot(q8, state.astype(BF16), preferred_element_type=F32)[0:1, :]
        o = score * vf + qs * decay
        k8 = jnp.where(row0, jnp.broadcast_to(kf, (8, DK_R)), 0.0).astype(BF16)
        v8 = jnp.broadcast_to(v, (8, DV_R))
        ns_ref[0, hd] = decay * state + lax.dot_general(k8, v8, TN_DIMS, preferred_element_type=F32)
        o_ref[0, :, sl] = _group_norm_gate(o, g_ref[0, :, sl], gn_ref[hd:hd + 1, :]).astype(BF16)


def _ret_sample(z3, log_g, gn_g, state):
    nb = z3.shape[0]
    zspec = lambda cb: pl.BlockSpec((1, 1, COL_BLOCK), lambda b: (b, 0, cb))
    st_spec = pl.BlockSpec((1, H_R, DK_R, DV_R), lambda b: (b, 0, 0, 0))
    return pl.pallas_call(
        _ret_sample_kernel,
        grid=(nb,),
        in_specs=[pl.BlockSpec(memory_space=pltpu.SMEM),
                  zspec(CB_RQ), zspec(CB_RK), zspec(CB_RV), zspec(CB_RG),
                  pl.BlockSpec((H_R, DV_R), lambda b: (0, 0)),
                  st_spec],
        out_specs=[pl.BlockSpec((1, 1, W_R), lambda b: (b, 0, 0)), st_spec],
        out_shape=[jax.ShapeDtypeStruct((nb, 1, W_R), BF16),
                   jax.ShapeDtypeStruct(state.shape, F32)],
        compiler_params=_params(("parallel",)),
        name="retention_sample",
    )(log_g, z3, z3, z3, z3, gn_g, state)


def _fox_prompt_kernel(q_ref, k_ref, v_ref, crow_ref, ccol_ref, o_ref, *, tq):
    qi = pl.program_id(1)
    ri = lax.broadcasted_iota(jnp.int32, (tq, tq), 0)
    ci = lax.broadcasted_iota(jnp.int32, (tq, tq), 1)
    causal = ri >= ci
    for hd in range(H_F):
        sl = slice(hd * D_F, (hd + 1) * D_F)
        q = q_ref[:, sl]
        c_q = ccol_ref[0, :, hd:hd + 1]

        def scores(kj):
            ks = pl.multiple_of(kj * tq, tq)
            kt = k_ref[pl.ds(ks, tq), sl]
            c_k = crow_ref[0, hd, pl.ds(kj, 1), :]
            s = lax.dot_general(q, kt, NT_DIMS, preferred_element_type=F32)
            return s * FOX_SCALE + (c_q - c_k), ks

        def update(carry, s, ks):
            m, l, acc = carry
            m_new = jnp.maximum(m, jnp.max(s, axis=-1, keepdims=True))
            alpha = jnp.exp(m - m_new)
            p = jnp.exp(s - m_new)
            l = alpha * l + jnp.sum(p, axis=-1, keepdims=True)
            vt = v_ref[pl.ds(ks, tq), sl]
            acc = alpha * acc + jnp.dot(p.astype(BF16), vt, preferred_element_type=F32)
            return m_new, l, acc

        def body(kj, carry):
            s, ks = scores(kj)
            return update(carry, s, ks)

        init = (jnp.full((tq, 1), -jnp.inf, F32), jnp.zeros((tq, 1), F32), jnp.zeros((tq, D_F), F32))
        carry = lax.fori_loop(0, qi, body, init)
        s, ks = scores(qi)
        m, l, acc = update(carry, jnp.where(causal, s, -jnp.inf), ks)
        o_ref[:, sl] = (acc / l).astype(BF16)


def _fox_prompt(z, c_row, c_col, batch, seq, tq=256):
    nq = seq // tq
    return pl.pallas_call(
        functools.partial(_fox_prompt_kernel, tq=tq),
        grid=(batch, nq),
        in_specs=[pl.BlockSpec((tq, COL_BLOCK), lambda b, i: (b * nq + i, CB_FQ)),
                  pl.BlockSpec((seq, COL_BLOCK), lambda b, i: (b, CB_FK)),
                  pl.BlockSpec((seq, COL_BLOCK), lambda b, i: (b, CB_FV)),
                  pl.BlockSpec((1, H_F, nq, tq), lambda b, i: (b, 0, 0, 0)),
                  pl.BlockSpec((1, tq, H_F), lambda b, i: (b, i, 0))],
        out_specs=pl.BlockSpec((tq, W_F), lambda b, i: (b * nq + i, 0)),
        out_shape=jax.ShapeDtypeStruct((batch * seq, W_F), BF16),
        compiler_params=_params(("parallel", "arbitrary")),
        name="fox_prompt",
    )(z, z, z, c_row.reshape(batch, H_F, nq, tq), c_col)


def _decode_bias_kernel(pt_ref, pool_ref, lnew_ref, bias_ref, rows_ref, *, n_pages):
    b = pl.program_id(0)
    width = PAGE_SIZE * H_F
    for p in range(n_pages):
        rows_ref[p:p + 1, :] = pool_ref[pl.ds(pt_ref[b, p], 1), :]
    x = rows_ref[...]
    lane = lax.broadcasted_iota(jnp.int32, x.shape, 1)
    within = x
    total = x
    s = H_F
    while s < width:
        within = within + jnp.where(lane >= s, pltpu.roll(within, s, axis=1), 0.0)
        total = total + pltpu.roll(total, s, axis=1)
        s *= 2
    rest = total - within
    run = lnew_ref[0]
    for p in reversed(range(n_pages)):
        bias_ref[0, p:p + 1, :] = run + rest[p:p + 1, :]
        run = run + total[p:p + 1, :]


def _decode_bias(page_table, logf_pool, lnew):
    nb, n_pages = page_table.shape
    width = PAGE_SIZE * H_F
    return pl.pallas_call(
        functools.partial(_decode_bias_kernel, n_pages=n_pages),
        grid_spec=pltpu.PrefetchScalarGridSpec(
            num_scalar_prefetch=1,
            grid=(nb,),
            in_specs=[pl.BlockSpec(logf_pool.shape, lambda b, pt: (0, 0), pipeline_mode=pl.Buffered(1)),
                      pl.BlockSpec((1, 1, width), lambda b, pt: (b, 0, 0))],
            out_specs=pl.BlockSpec((1, n_pages, width), lambda b, pt: (b, 0, 0)),
            scratch_shapes=[pltpu.VMEM((n_pages, width), F32)]),
        out_shape=jax.ShapeDtypeStruct((nb, n_pages, width), F32),
        compiler_params=_params(("arbitrary",)),
        name="decode_forget_bias",
    )(page_table, logf_pool, lnew)


def _fox_decode_kernel(pt_ref, q_ref, kn_ref, vn_ref, bias_ref, *refs, pages):
    k_refs = refs[:pages]
    v_refs = refs[pages:2 * pages]
    o_ref, m_ref, l_ref, acc_ref = refs[2 * pages:]
    g = pl.program_id(1)
    width = PAGE_SIZE * H_F
    row = lax.broadcasted_iota(jnp.int32, (H_F, width), 0)
    lane = lax.broadcasted_iota(jnp.int32, (H_F, width), 1)
    own = (lane % H_F) == row
    q = q_ref[0]

    @pl.when(g == 0)
    def _():
        s_new = jnp.sum(q.astype(F32) * kn_ref[0].astype(F32), axis=-1, keepdims=True) * FOX_SCALE
        m_ref[...] = jnp.broadcast_to(s_new, m_ref.shape)
        l_ref[...] = jnp.ones_like(l_ref)
        acc_ref[...] = vn_ref[0].astype(F32)

    s_parts = []
    for i in range(pages):
        kt = k_refs[i][...].astype(BF16)
        s = lax.dot_general(q, kt, NT_DIMS, preferred_element_type=F32)
        s_parts.append(jnp.where(own, s * FOX_SCALE + bias_ref[0, i:i + 1, :], -jnp.inf))
    m_old = m_ref[:, 0:1]
    m_new = m_old
    for s in s_parts:
        m_new = jnp.maximum(m_new, jnp.max(s, axis=-1, keepdims=True))
    alpha = jnp.exp(m_old - m_new)
    l_new = alpha * l_ref[:, 0:1]
    pv = jnp.zeros((H_F, D_F), F32)
    for i in range(pages):
        p = jnp.exp(s_parts[i] - m_new)
        l_new = l_new + jnp.sum(p, axis=-1, keepdims=True)
        pv = pv + jnp.dot(p.astype(BF16), v_refs[i][...].astype(BF16), preferred_element_type=F32)
    m_ref[...] = jnp.broadcast_to(m_new, m_ref.shape)
    l_ref[...] = jnp.broadcast_to(l_new, l_ref.shape)
    acc_ref[...] = alpha * acc_ref[...] + pv

    @pl.when(g == pl.num_programs(1) - 1)
    def _():
        o_ref[0] = (acc_ref[...] / l_ref[:, 0:1]).astype(BF16)


def _fox_decode(page_table, q, k_new, v_new, bias, k_pool, v_pool, pages=DECODE_PAGES_PER_STEP):
    nb, n_pages = page_table.shape
    width = PAGE_SIZE * H_F
    head_spec = pl.BlockSpec((1, H_F, D_F), lambda b, g, pt: (b, 0, 0))
    page_spec = lambda i: pl.BlockSpec((None, width, D_F), lambda b, g, pt: (pt[b, g * pages + i], 0, 0))
    return pl.pallas_call(
        functools.partial(_fox_decode_kernel, pages=pages),
        grid_spec=pltpu.PrefetchScalarGridSpec(
            num_scalar_prefetch=1,
            grid=(nb, n_pages // pages),
            in_specs=[head_spec, head_spec, head_spec,
                      pl.BlockSpec((1, pages, width), lambda b, g, pt: (b, g, 0))]
                     + [page_spec(i) for i in range(pages)] * 2,
            out_specs=head_spec,
            scratch_shapes=[pltpu.VMEM((H_F, D_F), F32),
                            pltpu.VMEM((H_F, D_F), F32),
                            pltpu.VMEM((H_F, D_F), F32)]),
        out_shape=jax.ShapeDtypeStruct((nb, H_F, D_F), BF16),
        compiler_params=_params(("parallel", "arbitrary")),
        name="fox_decode",
    )(page_table, q, k_new, v_new, bias, *([k_pool] * pages), *([v_pool] * pages))


def _merge_kernel(or_ref, of_ref, gr0_ref, gr1_ref, gf0_ref, gf1_ref, x_ref, eg_ref, eb_ref,
                  wr_ref, wf_ref, wo_ref, g1_ref, b1_ref, x1_ref, x1b_ref):
    y_r = jnp.dot(or_ref[...], wr_ref[...], preferred_element_type=F32)
    y_f = jnp.dot(of_ref[...], wf_ref[...], preferred_element_type=F32)
    halves = []
    for idx, (gr_ref, gf_ref) in enumerate(((gr0_ref, gf0_ref), (gr1_ref, gf1_ref))):
        sl = slice(idx * COL_BLOCK, (idx + 1) * COL_BLOCK)
        mixed = _sigmoid(gr_ref[...].astype(F32)) * y_r[:, sl] + _sigmoid(gf_ref[...].astype(F32)) * y_f[:, sl]
        halves.append(mixed.astype(BF16))
    mix = jnp.dot(jnp.concatenate(halves, axis=1), wo_ref[...], preferred_element_type=F32)
    h = _layer_norm(x_ref[...], eg_ref[...], eb_ref[...])
    x1 = _layer_norm(ALPHA * h + mix, g1_ref[...], b1_ref[...])
    x1_ref[...] = x1
    x1b_ref[...] = x1.astype(BF16)


def _merge(o_r, o_f, z, x, eg, eb, w_r, w_f, w_o, g1, b1, tm):
    m = x.shape[0]
    row = lambda i: (i, 0)
    vec = pl.BlockSpec((1, D_MODEL), lambda i: (0, 0))
    zspec = lambda cb: pl.BlockSpec((tm, COL_BLOCK), lambda i: (i, cb))
    return pl.pallas_call(
        _merge_kernel,
        grid=(m // tm,),
        in_specs=[pl.BlockSpec((tm, W_R), row), pl.BlockSpec((tm, W_F), row),
                  zspec(CB_GR), zspec(CB_GR + 1), zspec(CB_GF), zspec(CB_GF + 1),
                  pl.BlockSpec((tm, D_MODEL), row), vec, vec,
                  _resident((W_R, D_MODEL)), _resident((W_F, D_MODEL)), _resident((D_MODEL, D_MODEL)),
                  vec, vec],
        out_specs=[pl.BlockSpec((tm, D_MODEL), row), pl.BlockSpec((tm, D_MODEL), row)],
        out_shape=[jax.ShapeDtypeStruct((m, D_MODEL), F32), jax.ShapeDtypeStruct((m, D_MODEL), BF16)],
        compiler_params=_params(("parallel",)),
        name="merge_out_ln1",
    )(o_r, o_f, z, z, z, z, x, eg, eb, w_r, w_f, w_o, g1, b1)


def _ffn_kernel(x_ref, w1_ref, w3_ref, w2_ref, o_ref, acc_ref):
    j = pl.program_id(1)
    xb = x_ref[...]
    a = jnp.dot(xb, w1_ref[...], preferred_element_type=F32)
    b = jnp.dot(xb, w3_ref[...], preferred_element_type=F32)
    mid = (a * _sigmoid(a) * b).astype(BF16)
    part = jnp.dot(mid, w2_ref[...], preferred_element_type=F32)

    @pl.when(j == 0)
    def _():
        acc_ref[...] = part

    @pl.when(j > 0)
    def _():
        acc_ref[...] += part

    @pl.when(j == pl.num_programs(1) - 1)
    def _():
        o_ref[...] = acc_ref[...]


def _ffn(x1b, w1, w3, w2, tm, tf=512):
    m = x1b.shape[0]
    return pl.pallas_call(
        _ffn_kernel,
        grid=(m // tm, D_FF // tf),
        in_specs=[pl.BlockSpec((tm, D_MODEL), lambda i, j: (i, 0)),
                  pl.BlockSpec((D_MODEL, tf), lambda i, j: (0, j)),
                  pl.BlockSpec((D_MODEL, tf), lambda i, j: (0, j)),
                  pl.BlockSpec((tf, D_MODEL), lambda i, j: (j, 0))],
        out_specs=pl.BlockSpec((tm, D_MODEL), lambda i, j: (i, 0)),
        out_shape=jax.ShapeDtypeStruct((m, D_MODEL), F32),
        scratch_shapes=[pltpu.VMEM((tm, D_MODEL), F32)],
        compiler_params=_params(("parallel", "arbitrary")),
        name="swiglu_ffn",
    )(x1b, w1, w3, w2)


def _ple_kernel(x1_ref, x1b_ref, ffn_ref, p_ref, wpg_ref, wpe_ref, g2_ref, b2_ref, y_ref):
    pg = jnp.dot(x1b_ref[...], wpg_ref[...], preferred_element_type=F32)
    pe = jnp.dot(p_ref[...].astype(BF16), wpe_ref[...], preferred_element_type=F32)
    y_ref[...] = _layer_norm(ALPHA * x1_ref[...] + ffn_ref[...] + _sigmoid(pg) * pe, g2_ref[...], b2_ref[...])


def _ple(x1, x1b, ffn, p, w_pg, w_pe, g2, b2, tm):
    m = x1.shape[0]
    row = lambda i: (i, 0)
    vec = pl.BlockSpec((1, D_MODEL), lambda i: (0, 0))
    return pl.pallas_call(
        _ple_kernel,
        grid=(m // tm,),
        in_specs=[pl.BlockSpec((tm, D_MODEL), row), pl.BlockSpec((tm, D_MODEL), row),
                  pl.BlockSpec((tm, D_MODEL), row), pl.BlockSpec((tm, D_PLE), row),
                  _resident((D_MODEL, D_MODEL)), _resident((D_PLE, D_MODEL)), vec, vec],
        out_specs=pl.BlockSpec((tm, D_MODEL), row),
        out_shape=jax.ShapeDtypeStruct((m, D_MODEL), F32),
        compiler_params=_params(("parallel",)),
        name="ple_ln2",
    )(x1, x1b, ffn, p, w_pg, w_pe, g2, b2)


def _rope_tables(pos):
    half = DK_R // 2
    inv = ROPE_BASE ** (-jnp.arange(half, dtype=F32) / half)
    ang = pos.astype(F32)[:, None] * inv[None, :]
    return jnp.cos(ang), jnp.sin(ang)


def kernel(x_prompt, x_sample, state_ret, cache_k, cache_v, cache_logf, page_table, p_prompt, p_sample, ln_emb_g, ln_emb_b, w_in, b_f, gn_g, w_ret_proj, w_fox_proj, w_out, ln1_g, ln1_b, w1, w3, w2, w_pg, w_pe, ln2_g, ln2_b):
    assert w_in.shape[0] == DEPTH == 1
    batch, seq, _ = x_prompt.shape
    dec_b, dec_t, _ = x_sample.shape
    assert dec_t == 1
    n_pages = page_table.shape[1]
    past_len = n_pages * PAGE_SIZE
    n_pool = cache_k.shape[1]

    wi = w_in[0]
    w_main = jnp.concatenate([wi[:, :FF_LO], wi[:, FF_HI:]], axis=1).astype(BF16)
    w_ff_t = wi[:, FF_LO:FF_HI].T.astype(BF16)
    bf_col = b_f[0].reshape(H_F, 1)
    vec = lambda v: v.reshape(1, D_MODEL)
    eg, eb = vec(ln_emb_g), vec(ln_emb_b)
    g1, b1, g2, b2 = vec(ln1_g[0]), vec(ln1_b[0]), vec(ln2_g[0]), vec(ln2_b[0])
    w_r, w_f, w_o = w_ret_proj[0].astype(BF16), w_fox_proj[0].astype(BF16), w_out[0].astype(BF16)
    w1b, w3b, w2b = w1[0].astype(BF16), w3[0].astype(BF16), w2[0].astype(BF16)
    w_pgb, w_peb = w_pg[0].astype(BF16), w_pe[0].astype(BF16)
    log_g = jnp.log1p(-jnp.exp2(-5.0 - jnp.arange(H_R, dtype=F32)))
    cos_p, sin_p = _rope_tables(jnp.arange(seq))
    cos_s, sin_s = _rope_tables(jnp.full((dec_b,), past_len))

    xp = x_prompt.reshape(batch * seq, D_MODEL)
    xs = x_sample.reshape(dec_b, D_MODEL)

    h_p = _entry_ln(xp, eg, eb, tm=512)
    z_p, fk_p, fv_p = _inproj(h_p, w_main, cos_p, sin_p, tm=512)
    lf_p, c_row = _logf(h_p, w_ff_t, bf_col, seq, with_cumsum=True)
    o_r_p, st_p = _ret_prompt(z_p, log_g, gn_g[0], batch, seq)
    o_f_p = _fox_prompt(z_p, c_row, jnp.swapaxes(c_row, 1, 2), batch, seq)
    x1_p, x1b_p = _merge(o_r_p, o_f_p, z_p, xp, eg, eb, w_r, w_f, w_o, g1, b1, tm=256)
    ffn_p = _ffn(x1b_p, w1b, w3b, w2b, tm=512)
    y_p = _ple(x1_p, x1b_p, ffn_p, p_prompt[0].reshape(batch * seq, D_PLE), w_pgb, w_peb, g2, b2, tm=256)

    h_s = _entry_ln(xs, eg, eb, tm=dec_b)
    z_s, fk_s, fv_s = _inproj(h_s, w_main, cos_s, sin_s, tm=dec_b)
    lf_s = _logf(h_s, w_ff_t, bf_col, dec_b, with_cumsum=False)
    z3 = z_s.reshape(dec_b, 1, N_MAIN)
    o_r_s, st_s = _ret_sample(z3, log_g, gn_g[0], state_ret[0])
    width = PAGE_SIZE * H_F
    lf_new = jnp.swapaxes(lf_s, 1, 2).reshape(dec_b, H_F)
    lnew = jnp.tile(lf_new, (1, PAGE_SIZE)).reshape(dec_b, 1, width)
    bias = _decode_bias(page_table, cache_logf[0].reshape(n_pool, width), lnew)
    heads = lambda cb: z_s[:, cb * COL_BLOCK:(cb + 1) * COL_BLOCK].reshape(dec_b, H_F, D_F)
    o_f_s = _fox_decode(page_table, heads(CB_FQ), heads(CB_FK), heads(CB_FV), bias,
                        cache_k[0].reshape(n_pool, width, D_F), cache_v[0].reshape(n_pool, width, D_F))
    x1_s, x1b_s = _merge(o_r_s.reshape(dec_b, W_R), o_f_s.reshape(dec_b, W_F), z_s, xs, eg, eb,
                         w_r, w_f, w_o, g1, b1, tm=dec_b)
    ffn_s = _ffn(x1b_s, w1b, w3b, w2b, tm=dec_b)
    y_s = _ple(x1_s, x1b_s, ffn_s, p_sample[0].reshape(dec_b, D_PLE), w_pgb, w_peb, g2, b2, tm=dec_b)

    return (y_p.reshape(batch, seq, D_MODEL),
            y_s.reshape(dec_b, 1, D_MODEL),
            st_p[None],
            fk_p.reshape(1, batch, seq, H_F, D_F),
            fv_p.reshape(1, batch, seq, H_F, D_F),
            jnp.swapaxes(lf_p, 1, 2)[None],
            st_s[None],
            fk_s.reshape(1, dec_b, 1, H_F, D_F),
            fv_s.reshape(1, dec_b, 1, H_F, D_F),
            jnp.swapaxes(lf_s, 1, 2).reshape(1, dec_b, 1, H_F))
```

```python
import functools

import jax
import jax.numpy as jnp
import numpy as np
from jax import lax
from jax.experimental import pallas as pl
from jax.experimental.pallas import tpu as pltpu

F32 = jnp.float32
BF16 = jnp.bfloat16

D_MODEL = 2048
DEPTH = 1
PAGE_SIZE = 128
H_R = 4
DK_R = 256
DV_R = 256
W_R = H_R * DV_R
CHUNK = 128
ROPE_BASE = 10000.0
GN_EPS = 1e-6
H_F = 8
D_F = 128
W_F = H_F * D_F
D_FF = 5632
D_PLE = 256
LN_EPS = 1e-5
ALPHA = (2.0 * DEPTH) ** 0.25
FOX_SCALE = D_F ** -0.5
LOG2E = float(np.log2(np.e))

COL_BLOCK = 1024
N_MAIN = 11 * COL_BLOCK
CB_RQ, CB_RK, CB_RV, CB_RG, CB_FQ, CB_FK, CB_FV, CB_GR, CB_GF = 0, 1, 2, 3, 4, 5, 6, 7, 9
FF_LO, FF_HI = 7 * COL_BLOCK, 7 * COL_BLOCK + H_F

VMEM_LIMIT = 56 * 1024 * 1024
DECODE_PAGES_PER_STEP = 8

NT_DIMS = (((1,), (1,)), ((), ()))
TN_DIMS = (((0,), (0,)), ((), ()))


def _params(semantics):
    return pltpu.CompilerParams(dimension_semantics=semantics, vmem_limit_bytes=VMEM_LIMIT)


def _layer_norm(x, g, b):
    mu = jnp.mean(x, axis=-1, keepdims=True)
    xc = x - mu
    var = jnp.mean(xc * xc, axis=-1, keepdims=True)
    return xc * lax.rsqrt(var + LN_EPS) * g + b


def _sigmoid(x):
    return 1.0 / (1.0 + jnp.exp(-x))


def _log_sigmoid(x):
    return jnp.minimum(x, 0.0) - jnp.log(1.0 + jnp.exp(-jnp.abs(x)))


def _resident(shape):
    return pl.BlockSpec(shape, lambda *_: (0,) * len(shape), pipeline_mode=pl.Buffered(1))


def _ln_kernel(x_ref, g_ref, b_ref, o_ref):
    o_ref[...] = _layer_norm(x_ref[...], g_ref[...], b_ref[...]).astype(o_ref.dtype)


def _entry_ln(x, g, b, tm):
    m = x.shape[0]
    return pl.pallas_call(
        _ln_kernel,
        grid=(m // tm,),
        in_specs=[pl.BlockSpec((tm, D_MODEL), lambda i: (i, 0)),
                  pl.BlockSpec((1, D_MODEL), lambda i: (0, 0)),
                  pl.BlockSpec((1, D_MODEL), lambda i: (0, 0))],
        out_specs=pl.BlockSpec((tm, D_MODEL), lambda i: (i, 0)),
        out_shape=jax.ShapeDtypeStruct((m, D_MODEL), BF16),
        compiler_params=_params(("parallel",)),
        name="entry_ln",
    )(x, g, b)


def _inproj_kernel(h_ref, w_ref, cos_ref, sin_ref, z_ref, fk_ref, fv_ref):
    j = pl.program_id(1)
    acc = jnp.dot(h_ref[...], w_ref[...], preferred_element_type=F32)

    @pl.when(j <= CB_RK)
    def _():
        cos = cos_ref[...]
        sin = sin_ref[...]
        scale = jnp.where(j == CB_RK, DK_R ** -0.5, 1.0).astype(F32)
        half = DK_R // 2
        for hd in range(H_R):
            lo = hd * DK_R
            x1 = acc[:, lo:lo + half]
            x2 = acc[:, lo + half:lo + DK_R]
            z_ref[:, lo:lo + half] = ((x1 * cos - x2 * sin) * scale).astype(BF16)
            z_ref[:, lo + half:lo + DK_R] = ((x2 * cos + x1 * sin) * scale).astype(BF16)

    @pl.when(j > CB_RK)
    def _():
        z_ref[...] = acc.astype(BF16)

    @pl.when(j == CB_FK)
    def _():
        fk_ref[...] = acc

    @pl.when(j == CB_FV)
    def _():
        fv_ref[...] = acc


def _inproj(h, w_main, cos, sin, tm):
    m = h.shape[0]
    n_pos_blocks = cos.shape[0] // tm
    return pl.pallas_call(
        _inproj_kernel,
        grid=(m // tm, N_MAIN // COL_BLOCK),
        in_specs=[pl.BlockSpec((tm, D_MODEL), lambda i, j: (i, 0)),
                  pl.BlockSpec((D_MODEL, COL_BLOCK), lambda i, j: (0, j)),
                  pl.BlockSpec((tm, DK_R // 2), lambda i, j: (i % n_pos_blocks, 0)),
                  pl.BlockSpec((tm, DK_R // 2), lambda i, j: (i % n_pos_blocks, 0))],
        out_specs=[pl.BlockSpec((tm, COL_BLOCK), lambda i, j: (i, j)),
                   pl.BlockSpec((tm, COL_BLOCK), lambda i, j: (i, 0)),
                   pl.BlockSpec((tm, COL_BLOCK), lambda i, j: (i, 0))],
        out_shape=[jax.ShapeDtypeStruct((m, N_MAIN), BF16),
                   jax.ShapeDtypeStruct((m, W_F), F32),
                   jax.ShapeDtypeStruct((m, W_F), F32)],
        compiler_params=_params(("parallel", "arbitrary")),
        name="in_proj",
    )(h, w_main, cos, sin)


def _cumsum_lanes(x):
    n = x.shape[-1]
    lane = lax.broadcasted_iota(jnp.int32, x.shape, x.ndim - 1)
    s = 1
    while s < n:
        x = x + jnp.where(lane >= s, pltpu.roll(x, s, axis=x.ndim - 1), 0.0)
        s *= 2
    return x


def _logf_kernel(h_ref, w_ref, bf_ref, lf_ref, c_ref=None):
    ff = lax.dot_general(w_ref[...], h_ref[...], NT_DIMS, preferred_element_type=F32)
    lf = _log_sigmoid(ff + bf_ref[...])
    lf_ref[0] = lf
    if c_ref is not None:
        c_ref[0] = _cumsum_lanes(lf) * LOG2E


def _logf(h, w_ff_t, bf_col, seq, with_cumsum):
    m = h.shape[0]
    nb = m // seq
    shape = jax.ShapeDtypeStruct((nb, H_F, seq), F32)
    spec = pl.BlockSpec((1, H_F, seq), lambda b: (b, 0, 0))
    return pl.pallas_call(
        _logf_kernel,
        grid=(nb,),
        in_specs=[pl.BlockSpec((seq, D_MODEL), lambda b: (b, 0)),
                  pl.BlockSpec((H_F, D_MODEL), lambda b: (0, 0)),
                  pl.BlockSpec((H_F, 1), lambda b: (0, 0))],
        out_specs=[spec, spec] if with_cumsum else spec,
        out_shape=[shape, shape] if with_cumsum else shape,
        compiler_params=_params(("parallel",)),
        name="log_forget_cumsum" if with_cumsum else "log_forget",
    )(h, w_ff_t, bf_col)


def _group_norm_gate(o, gate_bf16, gn_row):
    mu = jnp.mean(o, axis=-1, keepdims=True)
    oc = o - mu
    var = jnp.mean(oc * oc, axis=-1, keepdims=True)
    gate = gate_bf16.astype(F32)
    return oc * lax.rsqrt(var + GN_EPS) * gn_row * (gate * _sigmoid(gate))


def _ret_prompt_kernel(lg_ref, q_ref, k_ref, v_ref, g_ref, gn_ref, o_ref, st_ref):
    c = pl.program_id(1)

    @pl.when(c == 0)
    def _():
        st_ref[...] = jnp.zeros_like(st_ref)

    n = q_ref.shape[0]
    ri = lax.broadcasted_iota(jnp.int32, (n, n), 0)
    ci = lax.broadcasted_iota(jnp.int32, (n, n), 1)
    causal = ri >= ci
    diff = jnp.where(causal, (ri - ci).astype(F32), 0.0)
    pos = lax.broadcasted_iota(jnp.int32, (n, DK_R), 0).astype(F32)
    for hd in range(H_R):
        lg = lg_ref[hd]
        sl = slice(hd * DK_R, (hd + 1) * DK_R)
        q = q_ref[:, sl]
        k = k_ref[:, sl]
        v = v_ref[:, sl]
        dec = jnp.where(causal, jnp.exp(lg * diff), 0.0)
        sc = lax.dot_general(q, k, NT_DIMS, preferred_element_type=F32) * dec
        o = jnp.dot(sc.astype(BF16), v, preferred_element_type=F32)
        state = st_ref[0, hd]
        cross = jnp.exp(lg * (pos + 1.0))
        o = o + jnp.dot(q, state.astype(BF16), preferred_element_type=F32) * cross
        kdec = jnp.exp(lg * (n - 1.0 - pos))
        kd = (k.astype(F32) * kdec).astype(BF16)
        carry = jnp.exp(jnp.full((1, DV_R), lg * n, F32))
        st_ref[0, hd] = carry * state + lax.dot_general(kd, v, TN_DIMS, preferred_element_type=F32)
        o_ref[:, sl] = _group_norm_gate(o, g_ref[:, sl], gn_ref[hd:hd + 1, :]).astype(BF16)


def _ret_prompt(z, log_g, gn_g, batch, seq):
    nc = seq // CHUNK
    row = lambda b, c: b * nc + c
    zspec = lambda cb: pl.BlockSpec((CHUNK, COL_BLOCK), lambda b, c: (row(b, c), cb))
    return pl.pallas_call(
        _ret_prompt_kernel,
        grid=(batch, nc),
        in_specs=[pl.BlockSpec(memory_space=pltpu.SMEM),
                  zspec(CB_RQ), zspec(CB_RK), zspec(CB_RV), zspec(CB_RG),
                  pl.BlockSpec((H_R, DV_R), lambda b, c: (0, 0))],
        out_specs=[pl.BlockSpec((CHUNK, W_R), lambda b, c: (row(b, c), 0)),
                   pl.BlockSpec((1, H_R, DK_R, DV_R), lambda b, c: (b, 0, 0, 0))],
        out_shape=[jax.ShapeDtypeStruct((batch * seq, W_R), BF16),
                   jax.ShapeDtypeStruct((batch, H_R, DK_R, DV_R), F32)],
        compiler_params=_params(("parallel", "arbitrary")),
        name="retention_prompt",
    )(log_g, z, z, z, z, gn_g)


def _ret_sample_kernel(lg_ref, q_ref, k_ref, v_ref, g_ref, gn_ref, st_ref, o_ref, ns_ref):
    row0 = lax.broadcasted_iota(jnp.int32, (8, DK_R), 0) == 0
    for hd in range(H_R):
        lg = lg_ref[hd]
        sl = slice(hd * DK_R, (hd + 1) * DK_R)
        q = q_ref[0, :, sl]
        k = k_ref[0, :, sl]
        v = v_ref[0, :, sl]
        qf = q.astype(F32)
        kf = k.astype(F32)
        vf = v.astype(F32)
        state = st_ref[0, hd]
        decay = jnp.exp(jnp.full((1, DV_R), lg, F32))
        score = jnp.sum(qf * kf, axis=-1, keepdims=True)
        q8 = jnp.broadcast_to(q, (8, DK_R))
        qs = jnp.dot(q8, state.astype(BF16), preferred_element_type=F32)[0:1, :]
        o = score * vf + qs * decay
        k8 = jnp.where(row0, jnp.broadcast_to(kf, (8, DK_R)), 0.0).astype(BF16)
        v8 = jnp.broadcast_to(v, (8, DV_R))
        ns_ref[0, hd] = decay * state + lax.dot_general(k8, v8, TN_DIMS, preferred_element_type=F32)
        o_ref[0, :, sl] = _group_norm_gate(o, g_ref[0, :, sl], gn_ref[hd:hd + 1, :]).astype(BF16)


def _ret_sample(z3, log_g, gn_g, state):
    nb = z3.shape[0]
    zspec = lambda cb: pl.BlockSpec((1, 1, COL_BLOCK), lambda b: (b, 0, cb))
    st_spec = pl.BlockSpec((1, H_R, DK_R, DV_R), lambda b: (b, 0, 0, 0))
    return pl.pallas_call(
        _ret_sample_kernel,
        grid=(nb,),
        in_specs=[pl.BlockSpec(memory_space=pltpu.SMEM),
                  zspec(CB_RQ), zspec(CB_RK), zspec(CB_RV), zspec(CB_RG),
                  pl.BlockSpec((H_R, DV_R), lambda b: (0, 0)),
                  st_spec],
        out_specs=[pl.BlockSpec((1, 1, W_R), lambda b: (b, 0, 0)), st_spec],
        out_shape=[jax.ShapeDtypeStruct((nb, 1, W_R), BF16),
                   jax.ShapeDtypeStruct(state.shape, F32)],
        compiler_params=_params(("parallel",)),
        name="retention_sample",
    )(log_g, z3, z3, z3, z3, gn_g, state)


def _fox_prompt_kernel(q_ref, k_ref, v_ref, crow_ref, ccol_ref, o_ref, vt_ref, m_ref, l_ref, acc_ref, *, tq):
    qi = pl.program_id(1)
    nk = vt_ref.shape[1]

    @pl.when(qi == 0)
    def _():
        for hd in range(H_F):
            for kj in range(nk):
                v_tile = v_ref[kj * tq:(kj + 1) * tq, hd * D_F:(hd + 1) * D_F]
                vt_ref[hd, kj] = v_tile.astype(F32).T.astype(BF16)

    m_ref[...] = jnp.full(m_ref.shape, -jnp.inf, F32)
    l_ref[...] = jnp.zeros_like(l_ref)
    acc_ref[...] = jnp.zeros_like(acc_ref)
    key_i = lax.broadcasted_iota(jnp.int32, (tq, tq), 0)
    qry_i = lax.broadcasted_iota(jnp.int32, (tq, tq), 1)
    visible = key_i <= qry_i

    heads = lambda ref, rows: jnp.stack([ref[rows, hd * D_F:(hd + 1) * D_F] for hd in range(H_F)], axis=0)
    q3 = heads(q_ref, slice(None))
    c_q = crow_ref[0, :, pl.ds(qi, 1), :]

    def tile(kj, on_diagonal):
        ks = pl.multiple_of(kj * tq, tq)
        k3 = heads(k_ref, pl.ds(ks, tq))
        s = lax.dot_general(k3, q3, (((2,), (2,)), ((0,), (0,))), preferred_element_type=F32)
        c_k = jnp.stack([ccol_ref[0, pl.ds(ks, tq), hd:hd + 1] for hd in range(H_F)], axis=0)
        t = s * (FOX_SCALE * LOG2E) + (c_q - c_k)
        if on_diagonal:
            t = jnp.where(visible[None], t, -jnp.inf)
        m_old = m_ref[...]
        m_new = jnp.maximum(m_old, jnp.max(t, axis=1, keepdims=True))
        alpha = jnp.exp2(m_old - m_new)
        p = jnp.exp2(t - m_new)
        l_ref[...] = alpha * l_ref[...] + jnp.sum(p, axis=1, keepdims=True)
        m_ref[...] = m_new
        pv = lax.dot_general(vt_ref[:, kj], p.astype(BF16), (((2,), (1,)), ((0,), (0,))),
                             preferred_element_type=F32)
        acc_ref[...] = alpha * acc_ref[...] + pv

    def body(kj, carry):
        tile(kj, False)
        return carry

    lax.fori_loop(0, qi, body, 0)
    tile(qi, True)
    for hd in range(H_F):
        o_ref[:, hd * D_F:(hd + 1) * D_F] = (acc_ref[hd] / l_ref[hd]).T.astype(BF16)


def _fox_prompt(z, c2_row, c2_col, batch, seq, tq=256):
    nq = seq // tq
    return pl.pallas_call(
        functools.partial(_fox_prompt_kernel, tq=tq),
        grid=(batch, nq),
        in_specs=[pl.BlockSpec((tq, COL_BLOCK), lambda b, i: (b * nq + i, CB_FQ)),
                  pl.BlockSpec((seq, COL_BLOCK), lambda b, i: (b, CB_FK)),
                  pl.BlockSpec((seq, COL_BLOCK), lambda b, i: (b, CB_FV)),
                  pl.BlockSpec((1, H_F, nq, tq), lambda b, i: (b, 0, 0, 0)),
                  pl.BlockSpec((1, seq, H_F), lambda b, i: (b, 0, 0))],
        out_specs=pl.BlockSpec((tq, W_F), lambda b, i: (b * nq + i, 0)),
        out_shape=jax.ShapeDtypeStruct((batch * seq, W_F), BF16),
        scratch_shapes=[pltpu.VMEM((H_F, nq, D_F, tq), BF16),
                        pltpu.VMEM((H_F, 1, tq), F32),
                        pltpu.VMEM((H_F, 1, tq), F32),
                        pltpu.VMEM((H_F, D_F, tq), F32)],
        compiler_params=_params(("parallel", "arbitrary")),
        name="fox_prompt",
    )(z, z, z, c2_row.reshape(batch, H_F, nq, tq), c2_col)


def _decode_bias_kernel(pt_ref, pool_ref, lnew_ref, bias_ref, rows_ref, *, n_pages):
    b = pl.program_id(0)
    width = PAGE_SIZE * H_F
    for p in range(n_pages):
        rows_ref[p:p + 1, :] = pool_ref[pl.ds(pt_ref[b, p], 1), :]
    x = rows_ref[...]
    lane = lax.broadcasted_iota(jnp.int32, x.shape, 1)
    within = x
    total = x
    s = H_F
    while s < width:
        within = within + jnp.where(lane >= s, pltpu.roll(within, s, axis=1), 0.0)
        total = total + pltpu.roll(total, s, axis=1)
        s *= 2
    rest = total - within
    run = lnew_ref[0]
    for p in reversed(range(n_pages)):
        bias_ref[0, p:p + 1, :] = run + rest[p:p + 1, :]
        run = run + total[p:p + 1, :]


def _decode_bias(page_table, logf_pool, lnew):
    nb, n_pages = page_table.shape
    width = PAGE_SIZE * H_F
    return pl.pallas_call(
        functools.partial(_decode_bias_kernel, n_pages=n_pages),
        grid_spec=pltpu.PrefetchScalarGridSpec(
            num_scalar_prefetch=1,
            grid=(nb,),
            in_specs=[pl.BlockSpec(logf_pool.shape, lambda b, pt: (0, 0), pipeline_mode=pl.Buffered(1)),
                      pl.BlockSpec((1, 1, width), lambda b, pt: (b, 0, 0))],
            out_specs=pl.BlockSpec((1, n_pages, width), lambda b, pt: (b, 0, 0)),
            scratch_shapes=[pltpu.VMEM((n_pages, width), F32)]),
        out_shape=jax.ShapeDtypeStruct((nb, n_pages, width), F32),
        compiler_params=_params(("arbitrary",)),
        name="decode_forget_bias",
    )(page_table, logf_pool, lnew)


def _fox_decode_kernel(pt_ref, q_ref, kn_ref, vn_ref, bias_ref, *refs, pages):
    k_refs = refs[:pages]
    v_refs = refs[pages:2 * pages]
    o_ref, m_ref, l_ref, acc_ref = refs[2 * pages:]
    g = pl.program_id(1)
    width = PAGE_SIZE * H_F
    row = lax.broadcasted_iota(jnp.int32, (H_F, width), 0)
    lane = lax.broadcasted_iota(jnp.int32, (H_F, width), 1)
    own = (lane % H_F) == row
    q = q_ref[0]

    @pl.when(g == 0)
    def _():
        s_new = jnp.sum(q.astype(F32) * kn_ref[0].astype(F32), axis=-1, keepdims=True) * FOX_SCALE
        m_ref[...] = jnp.broadcast_to(s_new, m_ref.shape)
        l_ref[...] = jnp.ones_like(l_ref)
        acc_ref[...] = vn_ref[0].astype(F32)

    s_parts = []
    for i in range(pages):
        kt = k_refs[i][...].astype(BF16)
        s = lax.dot_general(q, kt, NT_DIMS, preferred_element_type=F32)
        s_parts.append(jnp.where(own, s * FOX_SCALE + bias_ref[0, i:i + 1, :], -jnp.inf))
    m_old = m_ref[:, 0:1]
    m_new = m_old
    for s in s_parts:
        m_new = jnp.maximum(m_new, jnp.max(s, axis=-1, keepdims=True))
    alpha = jnp.exp(m_old - m_new)
    l_new = alpha * l_ref[:, 0:1]
    pv = jnp.zeros((H_F, D_F), F32)
    for i in range(pages):
        p = jnp.exp(s_parts[i] - m_new)
        l_new = l_new + jnp.sum(p, axis=-1, keepdims=True)
        pv = pv + jnp.dot(p.astype(BF16), v_refs[i][...].astype(BF16), preferred_element_type=F32)
    m_ref[...] = jnp.broadcast_to(m_new, m_ref.shape)
    l_ref[...] = jnp.broadcast_to(l_new, l_ref.shape)
    acc_ref[...] = alpha * acc_ref[...] + pv

    @pl.when(g == pl.num_programs(1) - 1)
    def _():
        o_ref[0] = (acc_ref[...] / l_ref[:, 0:1]).astype(BF16)


def _fox_decode(page_table, q, k_new, v_new, bias, k_pool, v_pool, pages=DECODE_PAGES_PER_STEP):
    nb, n_pages = page_table.shape
    width = PAGE_SIZE * H_F
    head_spec = pl.BlockSpec((1, H_F, D_F), lambda b, g, pt: (b, 0, 0))
    page_spec = lambda i: pl.BlockSpec((None, width, D_F), lambda b, g, pt: (pt[b, g * pages + i], 0, 0))
    return pl.pallas_call(
        functools.partial(_fox_decode_kernel, pages=pages),
        grid_spec=pltpu.PrefetchScalarGridSpec(
            num_scalar_prefetch=1,
            grid=(nb, n_pages // pages),
            in_specs=[head_spec, head_spec, head_spec,
                      pl.BlockSpec((1, pages, width), lambda b, g, pt: (b, g, 0))]
                     + [page_spec(i) for i in range(pages)] * 2,
            out_specs=head_spec,
            scratch_shapes=[pltpu.VMEM((H_F, D_F), F32),
                            pltpu.VMEM((H_F, D_F), F32),
                            pltpu.VMEM((H_F, D_F), F32)]),
        out_shape=jax.ShapeDtypeStruct((nb, H_F, D_F), BF16),
        compiler_params=_params(("parallel", "arbitrary")),
        name="fox_decode",
    )(page_table, q, k_new, v_new, bias, *([k_pool] * pages), *([v_pool] * pages))


def _merge_kernel(or_ref, of_ref, gr0_ref, gr1_ref, gf0_ref, gf1_ref, x_ref, eg_ref, eb_ref,
                  wr_ref, wf_ref, wo_ref, g1_ref, b1_ref, x1_ref, x1b_ref):
    y_r = jnp.dot(or_ref[...], wr_ref[...], preferred_element_type=F32)
    y_f = jnp.dot(of_ref[...], wf_ref[...], preferred_element_type=F32)
    halves = []
    for idx, (gr_ref, gf_ref) in enumerate(((gr0_ref, gf0_ref), (gr1_ref, gf1_ref))):
        sl = slice(idx * COL_BLOCK, (idx + 1) * COL_BLOCK)
        mixed = _sigmoid(gr_ref[...].astype(F32)) * y_r[:, sl] + _sigmoid(gf_ref[...].astype(F32)) * y_f[:, sl]
        halves.append(mixed.astype(BF16))
    mix = jnp.dot(jnp.concatenate(halves, axis=1), wo_ref[...], preferred_element_type=F32)
    h = _layer_norm(x_ref[...], eg_ref[...], eb_ref[...])
    x1 = _layer_norm(ALPHA * h + mix, g1_ref[...], b1_ref[...])
    x1_ref[...] = x1
    x1b_ref[...] = x1.astype(BF16)


def _merge(o_r, o_f, z, x, eg, eb, w_r, w_f, w_o, g1, b1, tm):
    m = x.shape[0]
    row = lambda i: (i, 0)
    vec = pl.BlockSpec((1, D_MODEL), lambda i: (0, 0))
    zspec = lambda cb: pl.BlockSpec((tm, COL_BLOCK), lambda i: (i, cb))
    return pl.pallas_call(
        _merge_kernel,
        grid=(m // tm,),
        in_specs=[pl.BlockSpec((tm, W_R), row), pl.BlockSpec((tm, W_F), row),
                  zspec(CB_GR), zspec(CB_GR + 1), zspec(CB_GF), zspec(CB_GF + 1),
                  pl.BlockSpec((tm, D_MODEL), row), vec, vec,
                  _resident((W_R, D_MODEL)), _resident((W_F, D_MODEL)), _resident((D_MODEL, D_MODEL)),
                  vec, vec],
        out_specs=[pl.BlockSpec((tm, D_MODEL), row), pl.BlockSpec((tm, D_MODEL), row)],
        out_shape=[jax.ShapeDtypeStruct((m, D_MODEL), F32), jax.ShapeDtypeStruct((m, D_MODEL), BF16)],
        compiler_params=_params(("parallel",)),
        name="merge_out_ln1",
    )(o_r, o_f, z, z, z, z, x, eg, eb, w_r, w_f, w_o, g1, b1)


def _ffn_kernel(x_ref, w1_ref, w3_ref, w2_ref, o_ref, acc_ref):
    j = pl.program_id(1)
    xb = x_ref[...]
    a = jnp.dot(xb, w1_ref[...], preferred_element_type=F32)
    b = jnp.dot(xb, w3_ref[...], preferred_element_type=F32)
    mid = (a * _sigmoid(a) * b).astype(BF16)
    part = jnp.dot(mid, w2_ref[...], preferred_element_type=F32)

    @pl.when(j == 0)
    def _():
        acc_ref[...] = part

    @pl.when(j > 0)
    def _():
        acc_ref[...] += part

    @pl.when(j == pl.num_programs(1) - 1)
    def _():
        o_ref[...] = acc_ref[...]


def _ffn(x1b, w1, w3, w2, tm, tf=512):
    m = x1b.shape[0]
    return pl.pallas_call(
        _ffn_kernel,
        grid=(m // tm, D_FF // tf),
        in_specs=[pl.BlockSpec((tm, D_MODEL), lambda i, j: (i, 0)),
                  pl.BlockSpec((D_MODEL, tf), lambda i, j: (0, j)),
                  pl.BlockSpec((D_MODEL, tf), lambda i, j: (0, j)),
                  pl.BlockSpec((tf, D_MODEL), lambda i, j: (j, 0))],
        out_specs=pl.BlockSpec((tm, D_MODEL), lambda i, j: (i, 0)),
        out_shape=jax.ShapeDtypeStruct((m, D_MODEL), F32),
        scratch_shapes=[pltpu.VMEM((tm, D_MODEL), F32)],
        compiler_params=_params(("parallel", "arbitrary")),
        name="swiglu_ffn",
    )(x1b, w1, w3, w2)


def _ple_kernel(x1_ref, x1b_ref, ffn_ref, p_ref, wpg_ref, wpe_ref, g2_ref, b2_ref, y_ref):
    pg = jnp.dot(x1b_ref[...], wpg_ref[...], preferred_element_type=F32)
    pe = jnp.dot(p_ref[...].astype(BF16), wpe_ref[...], preferred_element_type=F32)
    y_ref[...] = _layer_norm(ALPHA * x1_ref[...] + ffn_ref[...] + _sigmoid(pg) * pe, g2_ref[...], b2_ref[...])


def _ple(x1, x1b, ffn, p, w_pg, w_pe, g2, b2, tm):
    m = x1.shape[0]
    row = lambda i: (i, 0)
    vec = pl.BlockSpec((1, D_MODEL), lambda i: (0, 0))
    return pl.pallas_call(
        _ple_kernel,
        grid=(m // tm,),
        in_specs=[pl.BlockSpec((tm, D_MODEL), row), pl.BlockSpec((tm, D_MODEL), row),
                  pl.BlockSpec((tm, D_MODEL), row), pl.BlockSpec((tm, D_PLE), row),
                  _resident((D_MODEL, D_MODEL)), _resident((D_PLE, D_MODEL)), vec, vec],
        out_specs=pl.BlockSpec((tm, D_MODEL), row),
        out_shape=jax.ShapeDtypeStruct((m, D_MODEL), F32),
        compiler_params=_params(("parallel",)),
        name="ple_ln2",
    )(x1, x1b, ffn, p, w_pg, w_pe, g2, b2)


def _rope_tables(pos):
    half = DK_R // 2
    inv = ROPE_BASE ** (-jnp.arange(half, dtype=F32) / half)
    ang = pos.astype(F32)[:, None] * inv[None, :]
    return jnp.cos(ang), jnp.sin(ang)


def kernel(x_prompt, x_sample, state_ret, cache_k, cache_v, cache_logf, page_table, p_prompt, p_sample, ln_emb_g, ln_emb_b, w_in, b_f, gn_g, w_ret_proj, w_fox_proj, w_out, ln1_g, ln1_b, w1, w3, w2, w_pg, w_pe, ln2_g, ln2_b):
    assert w_in.shape[0] == DEPTH == 1
    batch, seq, _ = x_prompt.shape
    dec_b, dec_t, _ = x_sample.shape
    assert dec_t == 1
    n_pages = page_table.shape[1]
    past_len = n_pages * PAGE_SIZE
    n_pool = cache_k.shape[1]

    wi = w_in[0]
    w_main = jnp.concatenate([wi[:, :FF_LO], wi[:, FF_HI:]], axis=1).astype(BF16)
    w_ff_t = wi[:, FF_LO:FF_HI].T.astype(BF16)
    bf_col = b_f[0].reshape(H_F, 1)
    vec = lambda v: v.reshape(1, D_MODEL)
    eg, eb = vec(ln_emb_g), vec(ln_emb_b)
    g1, b1, g2, b2 = vec(ln1_g[0]), vec(ln1_b[0]), vec(ln2_g[0]), vec(ln2_b[0])
    w_r, w_f, w_o = w_ret_proj[0].astype(BF16), w_fox_proj[0].astype(BF16), w_out[0].astype(BF16)
    w1b, w3b, w2b = w1[0].astype(BF16), w3[0].astype(BF16), w2[0].astype(BF16)
    w_pgb, w_peb = w_pg[0].astype(BF16), w_pe[0].astype(BF16)
    log_g = jnp.log1p(-jnp.exp2(-5.0 - jnp.arange(H_R, dtype=F32)))
    cos_p, sin_p = _rope_tables(jnp.arange(seq))
    cos_s, sin_s = _rope_tables(jnp.full((dec_b,), past_len))

    xp = x_prompt.reshape(batch * seq, D_MODEL)
    xs = x_sample.reshape(dec_b, D_MODEL)

    h_p = _entry_ln(xp, eg, eb, tm=512)
    z_p, fk_p, fv_p = _inproj(h_p, w_main, cos_p, sin_p, tm=512)
    lf_p, c2_row = _logf(h_p, w_ff_t, bf_col, seq, with_cumsum=True)
    o_r_p, st_p = _ret_prompt(z_p, log_g, gn_g[0], batch, seq)
    o_f_p = _fox_prompt(z_p, c2_row, jnp.swapaxes(c2_row, 1, 2), batch, seq)
    x1_p, x1b_p = _merge(o_r_p, o_f_p, z_p, xp, eg, eb, w_r, w_f, w_o, g1, b1, tm=256)
    ffn_p = _ffn(x1b_p, w1b, w3b, w2b, tm=512)
    y_p = _ple(x1_p, x1b_p, ffn_p, p_prompt[0].reshape(batch * seq, D_PLE), w_pgb, w_peb, g2, b2, tm=256)

    h_s = _entry_ln(xs, eg, eb, tm=dec_b)
    z_s, fk_s, fv_s = _inproj(h_s, w_main, cos_s, sin_s, tm=dec_b)
    lf_s = _logf(h_s, w_ff_t, bf_col, dec_b, with_cumsum=False)
    z3 = z_s.reshape(dec_b, 1, N_MAIN)
    o_r_s, st_s = _ret_sample(z3, log_g, gn_g[0], state_ret[0])
    width = PAGE_SIZE * H_F
    lf_new = jnp.swapaxes(lf_s, 1, 2).reshape(dec_b, H_F)
    lnew = jnp.tile(lf_new, (1, PAGE_SIZE)).reshape(dec_b, 1, width)
    bias = _decode_bias(page_table, cache_logf[0].reshape(n_pool, width), lnew)
    heads = lambda cb: z_s[:, cb * COL_BLOCK:(cb + 1) * COL_BLOCK].reshape(dec_b, H_F, D_F)
    o_f_s = _fox_decode(page_table, heads(CB_FQ), heads(CB_FK), heads(CB_FV), bias,
                        cache_k[0].reshape(n_pool, width, D_F), cache_v[0].reshape(n_pool, width, D_F))
    x1_s, x1b_s = _merge(o_r_s.reshape(dec_b, W_R), o_f_s.reshape(dec_b, W_F), z_s, xs, eg, eb,
                         w_r, w_f, w_o, g1, b1, tm=dec_b)
    ffn_s = _ffn(x1b_s, w1b, w3b, w2b, tm=dec_b)
    y_s = _ple(x1_s, x1b_s, ffn_s, p_sample[0].reshape(dec_b, D_PLE), w_pgb, w_peb, g2, b2, tm=dec_b)

    return (y_p.reshape(batch, seq, D_MODEL),
            y_s.reshape(dec_b, 1, D_MODEL),
            st_p[None],
            fk_p.reshape(1, batch, seq, H_F, D_F),
            fv_p.reshape(1, batch, seq, H_F, D_F),
            jnp.swapaxes(lf_p, 1, 2)[None],
            st_s[None],
            fk_s.reshape(1, dec_b, 1, H_F, D_F),
            fv_s.reshape(1, dec_b, 1, H_F, D_F),
            jnp.swapaxes(lf_s, 1, 2).reshape(1, dec_b, 1, H_F))
```

```python
import functools

import jax
import jax.numpy as jnp
import numpy as np
from jax import lax
from jax.experimental import pallas as pl
from jax.experimental.pallas import tpu as pltpu

F32 = jnp.float32
BF16 = jnp.bfloat16

D_MODEL = 2048
DEPTH = 1
PAGE_SIZE = 128
H_R = 4
DK_R = 256
DV_R = 256
W_R = H_R * DV_R
CHUNK = 128
ROPE_BASE = 10000.0
GN_EPS = 1e-6
H_F = 8
D_F = 128
W_F = H_F * D_F
D_FF = 5632
D_PLE = 256
LN_EPS = 1e-5
ALPHA = (2.0 * DEPTH) ** 0.25
FOX_SCALE = D_F ** -0.5
LOG2E = float(np.log2(np.e))

COL_BLOCK = 1024
N_MAIN = 11 * COL_BLOCK
CB_RQ, CB_RK, CB_RV, CB_RG, CB_FQ, CB_FK, CB_FV, CB_GR, CB_GF = 0, 1, 2, 3, 4, 5, 6, 7, 9
FF_LO, FF_HI = 7 * COL_BLOCK, 7 * COL_BLOCK + H_F

IN_TM, IN_TN = 1024, 512
N_A_BLOCKS = FF_LO // IN_TN
FFN_TM, FFN_TF = 1024, 256
MERGE_TM, PLE_TM, ROW_SUBTILE = 256, 512, 256

VMEM_LIMIT = 56 * 1024 * 1024
DECODE_PAGES_PER_STEP = 8

NT_DIMS = (((1,), (1,)), ((), ()))
TN_DIMS = (((0,), (0,)), ((), ()))


def _params(semantics):
    return pltpu.CompilerParams(dimension_semantics=semantics, vmem_limit_bytes=VMEM_LIMIT)


def _layer_norm(x, g, b):
    mu = jnp.mean(x, axis=-1, keepdims=True)
    xc = x - mu
    var = jnp.mean(xc * xc, axis=-1, keepdims=True)
    return xc * lax.rsqrt(var + LN_EPS) * g + b


def _sigmoid(x):
    return 1.0 / (1.0 + jnp.exp(-x))


def _log_sigmoid(x):
    return jnp.minimum(x, 0.0) - jnp.log(1.0 + jnp.exp(-jnp.abs(x)))


def _resident(shape):
    return pl.BlockSpec(shape, lambda *_: (0,) * len(shape), pipeline_mode=pl.Buffered(1))


def _ln_kernel(x_ref, g_ref, b_ref, o_ref):
    o_ref[...] = _layer_norm(x_ref[...], g_ref[...], b_ref[...]).astype(o_ref.dtype)


def _entry_ln(x, g, b, tm):
    m = x.shape[0]
    return pl.pallas_call(
        _ln_kernel,
        grid=(m // tm,),
        in_specs=[pl.BlockSpec((tm, D_MODEL), lambda i: (i, 0)),
                  pl.BlockSpec((1, D_MODEL), lambda i: (0, 0)),
                  pl.BlockSpec((1, D_MODEL), lambda i: (0, 0))],
        out_specs=pl.BlockSpec((tm, D_MODEL), lambda i: (i, 0)),
        out_shape=jax.ShapeDtypeStruct((m, D_MODEL), BF16),
        compiler_params=_params(("parallel",)),
        name="entry_ln",
    )(x, g, b)


def _inproj_block(h, w_ref, z_ref, rope=None, f32_ref=None):
    half = DK_R // 2
    for hd in range(IN_TN // DK_R):
        lo = hd * DK_R
        acc = jnp.dot(h, w_ref[:, lo:lo + DK_R].astype(BF16), preferred_element_type=F32)
        if rope is None:
            z_ref[:, lo:lo + DK_R] = acc.astype(BF16)
        else:
            cos, sin, scale = rope
            x1 = acc[:, :half]
            x2 = acc[:, half:]
            z_ref[:, lo:lo + half] = ((x1 * cos - x2 * sin) * scale).astype(BF16)
            z_ref[:, lo + half:lo + DK_R] = ((x2 * cos + x1 * sin) * scale).astype(BF16)
        if f32_ref is not None:
            f32_ref[:, lo:lo + DK_R] = acc


def _inproj_kernel(h_ref, xs_ref, eg_ref, eb_ref, wa_ref, wg_ref, cos_ref, sin_ref, coss_ref, sins_ref,
                   z_ref, fk_ref, fv_ref, zs_ref, fks_ref, fvs_ref):
    i = pl.program_id(0)
    j = pl.program_id(1)
    per = COL_BLOCK // IN_TN

    def section(lo, hi, w_ref, rope_scale=None, f32_refs=(None, None)):
        @pl.when((j >= lo) & (j < hi))
        def _():
            rope = None if rope_scale is None else (cos_ref[...], sin_ref[...], rope_scale)
            _inproj_block(h_ref[...], w_ref, z_ref, rope, f32_refs[0])

            @pl.when(i == 0)
            def _():
                hs = _layer_norm(xs_ref[...], eg_ref[...], eb_ref[...]).astype(BF16)
                rope_s = None if rope_scale is None else (coss_ref[...], sins_ref[...], rope_scale)
                _inproj_block(hs, w_ref, zs_ref, rope_s, f32_refs[1])

    @pl.when(i > 0)
    def _():
        zs_ref[...] = jnp.zeros_like(zs_ref)

    section(CB_RQ * per, CB_RK * per, wa_ref, rope_scale=1.0)
    section(CB_RK * per, CB_RV * per, wa_ref, rope_scale=DK_R ** -0.5)
    section(CB_RV * per, CB_FK * per, wa_ref)
    section(CB_FK * per, CB_FV * per, wa_ref, f32_refs=(fk_ref, fks_ref))
    section(CB_FV * per, N_A_BLOCKS, wa_ref, f32_refs=(fv_ref, fvs_ref))
    section(N_A_BLOCKS, N_MAIN // IN_TN, wg_ref)


def _inproj(h, xs, eg, eb, w_in2d, w_gates, cos, sin, cos_s, sin_s, tm):
    m = h.shape[0]
    ms = xs.shape[0]
    n_pos_blocks = cos.shape[0] // tm
    nj = N_MAIN // IN_TN
    per = COL_BLOCK // IN_TN
    half = DK_R // 2
    clamp = lambda j, cb: jnp.clip(j - cb * per, 0, per - 1)
    row = lambda i, j: (i, 0)
    fixed = lambda i, j: (0, 0)
    return pl.pallas_call(
        _inproj_kernel,
        grid=(m // tm, nj),
        in_specs=[pl.BlockSpec((tm, D_MODEL), row),
                  pl.BlockSpec((ms, D_MODEL), fixed),
                  pl.BlockSpec((1, D_MODEL), fixed),
                  pl.BlockSpec((1, D_MODEL), fixed),
                  pl.BlockSpec((D_MODEL, IN_TN), lambda i, j: (0, jnp.minimum(j, N_A_BLOCKS - 1))),
                  pl.BlockSpec((D_MODEL, IN_TN), lambda i, j: (0, jnp.maximum(j - N_A_BLOCKS, 0))),
                  pl.BlockSpec((tm, half), lambda i, j: (i % n_pos_blocks, 0)),
                  pl.BlockSpec((tm, half), lambda i, j: (i % n_pos_blocks, 0)),
                  pl.BlockSpec((ms, half), fixed),
                  pl.BlockSpec((ms, half), fixed)],
        out_specs=[pl.BlockSpec((tm, IN_TN), lambda i, j: (i, j)),
                   pl.BlockSpec((tm, IN_TN), lambda i, j: (i, clamp(j, CB_FK))),
                   pl.BlockSpec((tm, IN_TN), lambda i, j: (i, clamp(j, CB_FV))),
                   pl.BlockSpec((ms, IN_TN), lambda i, j: (0, jnp.where(i == 0, j, nj))),
                   pl.BlockSpec((ms, IN_TN), lambda i, j: (0, jnp.where(i == 0, clamp(j, CB_FK), per - 1))),
                   pl.BlockSpec((ms, IN_TN), lambda i, j: (0, jnp.where(i == 0, clamp(j, CB_FV), per - 1)))],
        out_shape=[jax.ShapeDtypeStruct((m, N_MAIN), BF16),
                   jax.ShapeDtypeStruct((m, W_F), F32),
                   jax.ShapeDtypeStruct((m, W_F), F32),
                   jax.ShapeDtypeStruct((ms, N_MAIN + IN_TN), BF16),
                   jax.ShapeDtypeStruct((ms, W_F), F32),
                   jax.ShapeDtypeStruct((ms, W_F), F32)],
        compiler_params=_params(("arbitrary", "arbitrary")),
        name="in_proj",
    )(h, xs, eg, eb, w_in2d, w_gates, cos, sin, cos_s, sin_s)


def _cumsum_lanes(x):
    n = x.shape[-1]
    lane = lax.broadcasted_iota(jnp.int32, x.shape, x.ndim - 1)
    s = 1
    while s < n:
        x = x + jnp.where(lane >= s, pltpu.roll(x, s, axis=x.ndim - 1), 0.0)
        s *= 2
    return x


def _logf_prompt_kernel(h_ref, w_ref, bf_ref, lf_ref, c_ref):
    ff = lax.dot_general(w_ref[...], h_ref[...], NT_DIMS, preferred_element_type=F32)
    lf = _log_sigmoid(ff + bf_ref[...])
    lf_ref[0] = lf
    c_ref[0] = _cumsum_lanes(lf) * LOG2E


def _logf_sample_kernel(x_ref, eg_ref, eb_ref, w_ref, bf_ref, lf_ref):
    h = _layer_norm(x_ref[...], eg_ref[...], eb_ref[...]).astype(BF16)
    ff = lax.dot_general(w_ref[...], h, NT_DIMS, preferred_element_type=F32)
    lf_ref[0] = _log_sigmoid(ff + bf_ref[...])


def _logf_prompt(h, w_ff_t, bf_col, seq):
    nb = h.shape[0] // seq
    shape = jax.ShapeDtypeStruct((nb, H_F, seq), F32)
    spec = pl.BlockSpec((1, H_F, seq), lambda b: (b, 0, 0))
    return pl.pallas_call(
        _logf_prompt_kernel,
        grid=(nb,),
        in_specs=[pl.BlockSpec((seq, D_MODEL), lambda b: (b, 0)),
                  pl.BlockSpec((H_F, D_MODEL), lambda b: (0, 0)),
                  pl.BlockSpec((H_F, 1), lambda b: (0, 0))],
        out_specs=[spec, spec],
        out_shape=[shape, shape],
        compiler_params=_params(("parallel",)),
        name="log_forget_cumsum",
    )(h, w_ff_t, bf_col)


def _logf_sample(xs, eg, eb, w_ff_t, bf_col):
    ms = xs.shape[0]
    fixed = lambda b: (0, 0)
    return pl.pallas_call(
        _logf_sample_kernel,
        grid=(1,),
        in_specs=[pl.BlockSpec((ms, D_MODEL), fixed), pl.BlockSpec((1, D_MODEL), fixed),
                  pl.BlockSpec((1, D_MODEL), fixed), pl.BlockSpec((H_F, D_MODEL), fixed),
                  pl.BlockSpec((H_F, 1), fixed)],
        out_specs=pl.BlockSpec((1, H_F, ms), lambda b: (0, 0, 0)),
        out_shape=jax.ShapeDtypeStruct((1, H_F, ms), F32),
        compiler_params=_params(("arbitrary",)),
        name="log_forget",
    )(xs, eg, eb, w_ff_t, bf_col)


def _group_norm_gate(o, gate_bf16, gn_row):
    mu = jnp.mean(o, axis=-1, keepdims=True)
    oc = o - mu
    var = jnp.mean(oc * oc, axis=-1, keepdims=True)
    gate = gate_bf16.astype(F32)
    return oc * lax.rsqrt(var + GN_EPS) * gn_row * (gate * _sigmoid(gate))


def _ret_prompt_kernel(lg_ref, q_ref, k_ref, v_ref, g_ref, gn_ref, o_ref, st_ref):
    c = pl.program_id(1)

    @pl.when(c == 0)
    def _():
        st_ref[...] = jnp.zeros_like(st_ref)

    n = q_ref.shape[0]
    ri = lax.broadcasted_iota(jnp.int32, (n, n), 0)
    ci = lax.broadcasted_iota(jnp.int32, (n, n), 1)
    causal = ri >= ci
    diff = jnp.where(causal, (ri - ci).astype(F32), 0.0)
    pos = lax.broadcasted_iota(jnp.int32, (n, DK_R), 0).astype(F32)
    for hd in range(H_R):
        lg = lg_ref[hd]
        sl = slice(hd * DK_R, (hd + 1) * DK_R)
        q = q_ref[:, sl]
        k = k_ref[:, sl]
        v = v_ref[:, sl]
        dec = jnp.where(causal, jnp.exp(lg * diff), 0.0)
        sc = lax.dot_general(q, k, NT_DIMS, preferred_element_type=F32) * dec
        o = jnp.dot(sc.astype(BF16), v, preferred_element_type=F32)
        state = st_ref[0, hd]
        cross = jnp.exp(lg * (pos + 1.0))
        o = o + jnp.dot(q, state.astype(BF16), preferred_element_type=F32) * cross
        kdec = jnp.exp(lg * (n - 1.0 - pos))
        kd = (k.astype(F32) * kdec).astype(BF16)
        carry = jnp.exp(jnp.full((1, DV_R), lg * n, F32))
        st_ref[0, hd] = carry * state + lax.dot_general(kd, v, TN_DIMS, preferred_element_type=F32)
        o_ref[:, sl] = _group_norm_gate(o, g_ref[:, sl], gn_ref[hd:hd + 1, :]).astype(BF16)


def _ret_prompt(z, log_g, gn_g, batch, seq):
    nc = seq // CHUNK
    row = lambda b, c: b * nc + c
    zspec = lambda cb: pl.BlockSpec((CHUNK, COL_BLOCK), lambda b, c: (row(b, c), cb))
    return pl.pallas_call(
        _ret_prompt_kernel,
        grid=(batch, nc),
        in_specs=[pl.BlockSpec(memory_space=pltpu.SMEM),
                  zspec(CB_RQ), zspec(CB_RK), zspec(CB_RV), zspec(CB_RG),
                  pl.BlockSpec((H_R, DV_R), lambda b, c: (0, 0))],
        out_specs=[pl.BlockSpec((CHUNK, W_R), lambda b, c: (row(b, c), 0)),
                   pl.BlockSpec((1, H_R, DK_R, DV_R), lambda b, c: (b, 0, 0, 0))],
        out_shape=[jax.ShapeDtypeStruct((batch * seq, W_R), BF16),
                   jax.ShapeDtypeStruct((batch, H_R, DK_R, DV_R), F32)],
        compiler_params=_params(("parallel", "arbitrary")),
        name="retention_prompt",
    )(log_g, z, z, z, z, gn_g)


def _ret_sample_kernel(lg_ref, q_ref, k_ref, v_ref, g_ref, gn_ref, st_ref, o_ref, ns_ref):
    row0 = lax.broadcasted_iota(jnp.int32, (8, DK_R), 0) == 0
    for hd in range(H_R):
        lg = lg_ref[hd]
        sl = slice(hd * DK_R, (hd + 1) * DK_R)
        q = q_ref[0, :, sl]
        k = k_ref[0, :, sl]
        v = v_ref[0, :, sl]
        qf = q.astype(F32)
        kf = k.astype(F32)
        vf = v.astype(F32)
        state = st_ref[0, hd]
        decay = jnp.exp(jnp.full((1, DV_R), lg, F32))
        score = jnp.sum(qf * kf, axis=-1, keepdims=True)
        q8 = jnp.broadcast_to(q, (8, DK_R))
        qs = jnp.dot(q8, state.astype(BF16), preferred_element_type=F32)[0:1, :]
        o = score * vf + qs * decay
        k8 = jnp.where(row0, jnp.broadcast_to(kf, (8, DK_R)), 0.0).astype(BF16)
        v8 = jnp.broadcast_to(v, (8, DV_R))
        ns_ref[0, hd] = decay * state + lax.dot_general(k8, v8, TN_DIMS, preferred_element_type=F32)
        o_ref[0, :, sl] = _group_norm_gate(o, g_ref[0, :, sl], gn_ref[hd:hd + 1, :]).astype(BF16)


def _ret_sample(z3, log_g, gn_g, state):
    nb = z3.shape[0]
    zspec = lambda cb: pl.BlockSpec((1, 1, COL_BLOCK), lambda b: (b, 0, cb))
    st_spec = pl.BlockSpec((1, H_R, DK_R, DV_R), lambda b: (b, 0, 0, 0))
    return pl.pallas_call(
        _ret_sample_kernel,
        grid=(nb,),
        in_specs=[pl.BlockSpec(memory_space=pltpu.SMEM),
                  zspec(CB_RQ), zspec(CB_RK), zspec(CB_RV), zspec(CB_RG),
                  pl.BlockSpec((H_R, DV_R), lambda b: (0, 0)),
                  st_spec],
        out_specs=[pl.BlockSpec((1, 1, W_R), lambda b: (b, 0, 0)), st_spec],
        out_shape=[jax.ShapeDtypeStruct((nb, 1, W_R), BF16),
                   jax.ShapeDtypeStruct(state.shape, F32)],
        compiler_params=_params(("parallel",)),
        name="retention_sample",
    )(log_g, z3, z3, z3, z3, gn_g, state)


def _fox_prompt_kernel(q_ref, k_ref, v_ref, crow_ref, ccol_ref, o_ref, vt_ref, m_ref, l_ref, acc_ref, *, tq):
    qi = pl.program_id(1)
    nk = vt_ref.shape[1]

    @pl.when(qi == 0)
    def _():
        for hd in range(H_F):
            for kj in range(nk):
                v_tile = v_ref[kj * tq:(kj + 1) * tq, hd * D_F:(hd + 1) * D_F]
                vt_ref[hd, kj] = v_tile.astype(F32).T.astype(BF16)

    m_ref[...] = jnp.full(m_ref.shape, -jnp.inf, F32)
    l_ref[...] = jnp.zeros_like(l_ref)
    acc_ref[...] = jnp.zeros_like(acc_ref)
    key_i = lax.broadcasted_iota(jnp.int32, (tq, tq), 0)
    qry_i = lax.broadcasted_iota(jnp.int32, (tq, tq), 1)
    visible = key_i <= qry_i

    heads = lambda ref, rows: jnp.stack([ref[rows, hd * D_F:(hd + 1) * D_F] for hd in range(H_F)], axis=0)
    q3 = heads(q_ref, slice(None))
    c_q = crow_ref[0, :, pl.ds(qi, 1), :]

    def tile(kj, on_diagonal):
        ks = pl.multiple_of(kj * tq, tq)
        k3 = heads(k_ref, pl.ds(ks, tq))
        s = lax.dot_general(k3, q3, (((2,), (2,)), ((0,), (0,))), preferred_element_type=F32)
        c_k = jnp.stack([ccol_ref[0, pl.ds(ks, tq), hd:hd + 1] for hd in range(H_F)], axis=0)
        t = s * (FOX_SCALE * LOG2E) + (c_q - c_k)
        if on_diagonal:
            t = jnp.where(visible[None], t, -jnp.inf)
        m_old = m_ref[...]
        m_new = jnp.maximum(m_old, jnp.max(t, axis=1, keepdims=True))
        alpha = jnp.exp2(m_old - m_new)
        p = jnp.exp2(t - m_new)
        l_ref[...] = alpha * l_ref[...] + jnp.sum(p, axis=1, keepdims=True)
        m_ref[...] = m_new
        pv = lax.dot_general(vt_ref[:, kj], p.astype(BF16), (((2,), (1,)), ((0,), (0,))),
                             preferred_element_type=F32)
        acc_ref[...] = alpha * acc_ref[...] + pv

    def body(kj, carry):
        tile(kj, False)
        return carry

    lax.fori_loop(0, qi, body, 0)
    tile(qi, True)
    for hd in range(H_F):
        o_ref[:, hd * D_F:(hd + 1) * D_F] = (acc_ref[hd] / l_ref[hd]).T.astype(BF16)


def _fox_prompt(z, c2_row, c2_col, batch, seq, tq=256):
    nq = seq // tq
    return pl.pallas_call(
        functools.partial(_fox_prompt_kernel, tq=tq),
        grid=(batch, nq),
        in_specs=[pl.BlockSpec((tq, COL_BLOCK), lambda b, i: (b * nq + i, CB_FQ)),
                  pl.BlockSpec((seq, COL_BLOCK), lambda b, i: (b, CB_FK)),
                  pl.BlockSpec((seq, COL_BLOCK), lambda b, i: (b, CB_FV)),
                  pl.BlockSpec((1, H_F, nq, tq), lambda b, i: (b, 0, 0, 0)),
                  pl.BlockSpec((1, seq, H_F), lambda b, i: (b, 0, 0))],
        out_specs=pl.BlockSpec((tq, W_F), lambda b, i: (b * nq + i, 0)),
        out_shape=jax.ShapeDtypeStruct((batch * seq, W_F), BF16),
        scratch_shapes=[pltpu.VMEM((H_F, nq, D_F, tq), BF16),
                        pltpu.VMEM((H_F, 1, tq), F32),
                        pltpu.VMEM((H_F, 1, tq), F32),
                        pltpu.VMEM((H_F, D_F, tq), F32)],
        compiler_params=_params(("parallel", "arbitrary")),
        name="fox_prompt",
    )(z, z, z, c2_row.reshape(batch, H_F, nq, tq), c2_col)


def _decode_bias_kernel(pt_ref, pool_ref, lnew_ref, bias_ref, rows_ref, *, n_pages):
    b = pl.program_id(0)
    width = PAGE_SIZE * H_F
    for p in range(n_pages):
        rows_ref[p:p + 1, :] = pool_ref[pl.ds(pt_ref[b, p], 1), :]
    x = rows_ref[...]
    lane = lax.broadcasted_iota(jnp.int32, x.shape, 1)
    within = x
    total = x
    s = H_F
    while s < width:
        within = within + jnp.where(lane >= s, pltpu.roll(within, s, axis=1), 0.0)
        total = total + pltpu.roll(total, s, axis=1)
        s *= 2
    rest = total - within
    run = lnew_ref[0]
    for p in reversed(range(n_pages)):
        bias_ref[0, p:p + 1, :] = run + rest[p:p + 1, :]
        run = run + total[p:p + 1, :]


def _decode_bias(page_table, logf_pool, lnew):
    nb, n_pages = page_table.shape
    width = PAGE_SIZE * H_F
    return pl.pallas_call(
        functools.partial(_decode_bias_kernel, n_pages=n_pages),
        grid_spec=pltpu.PrefetchScalarGridSpec(
            num_scalar_prefetch=1,
            grid=(nb,),
            in_specs=[pl.BlockSpec(logf_pool.shape, lambda b, pt: (0, 0), pipeline_mode=pl.Buffered(1)),
                      pl.BlockSpec((1, 1, width), lambda b, pt: (b, 0, 0))],
            out_specs=pl.BlockSpec((1, n_pages, width), lambda b, pt: (b, 0, 0)),
            scratch_shapes=[pltpu.VMEM((n_pages, width), F32)]),
        out_shape=jax.ShapeDtypeStruct((nb, n_pages, width), F32),
        compiler_params=_params(("arbitrary",)),
        name="decode_forget_bias",
    )(page_table, logf_pool, lnew)


def _fox_decode_kernel(pt_ref, q_ref, kn_ref, vn_ref, bias_ref, *refs, pages):
    k_refs = refs[:pages]
    v_refs = refs[pages:2 * pages]
    o_ref, m_ref, l_ref, acc_ref = refs[2 * pages:]
    g = pl.program_id(1)
    width = PAGE_SIZE * H_F
    row = lax.broadcasted_iota(jnp.int32, (H_F, width), 0)
    lane = lax.broadcasted_iota(jnp.int32, (H_F, width), 1)
    own = (lane % H_F) == row
    q = q_ref[0]

    @pl.when(g == 0)
    def _():
        s_new = jnp.sum(q.astype(F32) * kn_ref[0].astype(F32), axis=-1, keepdims=True) * FOX_SCALE
        m_ref[...] = jnp.broadcast_to(s_new, m_ref.shape)
        l_ref[...] = jnp.ones_like(l_ref)
        acc_ref[...] = vn_ref[0].astype(F32)

    s_parts = []
    for i in range(pages):
        kt = k_refs[i][...].astype(BF16)
        s = lax.dot_general(q, kt, NT_DIMS, preferred_element_type=F32)
        s_parts.append(jnp.where(own, s * FOX_SCALE + bias_ref[0, i:i + 1, :], -jnp.inf))
    m_old = m_ref[:, 0:1]
    m_new = m_old
    for s in s_parts:
        m_new = jnp.maximum(m_new, jnp.max(s, axis=-1, keepdims=True))
    alpha = jnp.exp(m_old - m_new)
    l_new = alpha * l_ref[:, 0:1]
    pv = jnp.zeros((H_F, D_F), F32)
    for i in range(pages):
        p = jnp.exp(s_parts[i] - m_new)
        l_new = l_new + jnp.sum(p, axis=-1, keepdims=True)
        pv = pv + jnp.dot(p.astype(BF16), v_refs[i][...].astype(BF16), preferred_element_type=F32)
    m_ref[...] = jnp.broadcast_to(m_new, m_ref.shape)
    l_ref[...] = jnp.broadcast_to(l_new, l_ref.shape)
    acc_ref[...] = alpha * acc_ref[...] + pv

    @pl.when(g == pl.num_programs(1) - 1)
    def _():
        o_ref[0] = (acc_ref[...] / l_ref[:, 0:1]).astype(BF16)


def _fox_decode(page_table, q, k_new, v_new, bias, k_pool, v_pool, pages=DECODE_PAGES_PER_STEP):
    nb, n_pages = page_table.shape
    width = PAGE_SIZE * H_F
    head_spec = pl.BlockSpec((1, H_F, D_F), lambda b, g, pt: (b, 0, 0))
    page_spec = lambda i: pl.BlockSpec((None, width, D_F), lambda b, g, pt: (pt[b, g * pages + i], 0, 0))
    return pl.pallas_call(
        functools.partial(_fox_decode_kernel, pages=pages),
        grid_spec=pltpu.PrefetchScalarGridSpec(
            num_scalar_prefetch=1,
            grid=(nb, n_pages // pages),
            in_specs=[head_spec, head_spec, head_spec,
                      pl.BlockSpec((1, pages, width), lambda b, g, pt: (b, g, 0))]
                     + [page_spec(i) for i in range(pages)] * 2,
            out_specs=head_spec,
            scratch_shapes=[pltpu.VMEM((H_F, D_F), F32),
                            pltpu.VMEM((H_F, D_F), F32),
                            pltpu.VMEM((H_F, D_F), F32)]),
        out_shape=jax.ShapeDtypeStruct((nb, H_F, D_F), BF16),
        compiler_params=_params(("parallel", "arbitrary")),
        name="fox_decode",
    )(page_table, q, k_new, v_new, bias, *([k_pool] * pages), *([v_pool] * pages))


def _merge_kernel(or_ref, of_ref, gr0_ref, gr1_ref, gf0_ref, gf1_ref, x_ref, eg_ref, eb_ref,
                  wr_ref, wf_ref, wo_ref, g1_ref, b1_ref, x1_ref, x1b_ref):
    tm = x_ref.shape[0]
    sub = min(tm, ROW_SUBTILE)
    for r in range(tm // sub):
        rows = slice(r * sub, (r + 1) * sub)
        y_r = jnp.dot(or_ref[rows, :], wr_ref[...], preferred_element_type=F32)
        y_f = jnp.dot(of_ref[rows, :], wf_ref[...], preferred_element_type=F32)
        halves = []
        for idx, (gr_ref, gf_ref) in enumerate(((gr0_ref, gf0_ref), (gr1_ref, gf1_ref))):
            sl = slice(idx * COL_BLOCK, (idx + 1) * COL_BLOCK)
            mixed = (_sigmoid(gr_ref[rows, :].astype(F32)) * y_r[:, sl]
                     + _sigmoid(gf_ref[rows, :].astype(F32)) * y_f[:, sl])
            halves.append(mixed.astype(BF16))
        mix = jnp.dot(jnp.concatenate(halves, axis=1), wo_ref[...], preferred_element_type=F32)
        h = _layer_norm(x_ref[rows, :], eg_ref[...], eb_ref[...])
        x1 = _layer_norm(ALPHA * h + mix, g1_ref[...], b1_ref[...])
        x1_ref[rows, :] = x1
        x1b_ref[rows, :] = x1.astype(BF16)


def _merge(o_r, o_f, z, x, eg, eb, w_r, w_f, w_o, g1, b1, tm):
    m = x.shape[0]
    row = lambda i: (i, 0)
    vec = pl.BlockSpec((1, D_MODEL), lambda i: (0, 0))
    zspec = lambda cb: pl.BlockSpec((tm, COL_BLOCK), lambda i: (i, cb))
    return pl.pallas_call(
        _merge_kernel,
        grid=(m // tm,),
        in_specs=[pl.BlockSpec((tm, W_R), row), pl.BlockSpec((tm, W_F), row),
                  zspec(CB_GR), zspec(CB_GR + 1), zspec(CB_GF), zspec(CB_GF + 1),
                  pl.BlockSpec((tm, D_MODEL), row), vec, vec,
                  _resident((W_R, D_MODEL)), _resident((W_F, D_MODEL)), _resident((D_MODEL, D_MODEL)),
                  vec, vec],
        out_specs=[pl.BlockSpec((tm, D_MODEL), row), pl.BlockSpec((tm, D_MODEL), row)],
        out_shape=[jax.ShapeDtypeStruct((m, D_MODEL), F32), jax.ShapeDtypeStruct((m, D_MODEL), BF16)],
        compiler_params=_params(("parallel",)),
        name="merge_out_ln1",
    )(o_r, o_f, z, z, z, z, x, eg, eb, w_r, w_f, w_o, g1, b1)


def _ffn_kernel(x_ref, xs_ref, w1_ref, w3_ref, w2_ref, o_ref, os_ref):
    i = pl.program_id(0)
    j = pl.program_id(1)
    @pl.when(j == 0)
    def _():
        o_ref[...] = jnp.zeros_like(o_ref)

    @pl.when((j == 0) & (i == 0))
    def _():
        os_ref[...] = jnp.zeros_like(os_ref)

    w1 = w1_ref[...].astype(BF16)
    w3 = w3_ref[...].astype(BF16)
    w2 = w2_ref[...].astype(BF16)

    def accumulate(x_in, out):
        xb = x_in[...]
        a = jnp.dot(xb, w1, preferred_element_type=F32)
        b = jnp.dot(xb, w3, preferred_element_type=F32)
        mid = (a * _sigmoid(a) * b).astype(BF16)
        out[...] += jnp.dot(mid, w2, preferred_element_type=F32)

    accumulate(x_ref, o_ref)
    pl.when(i == 0)(lambda: accumulate(xs_ref, os_ref))


def _ffn(x1b, x1b_s, w1, w3, w2, tm, tf):
    m = x1b.shape[0]
    ms = x1b_s.shape[0]
    return pl.pallas_call(
        _ffn_kernel,
        grid=(m // tm, D_FF // tf),
        in_specs=[pl.BlockSpec((tm, D_MODEL), lambda i, j: (i, 0)),
                  pl.BlockSpec((ms, D_MODEL), lambda i, j: (0, 0)),
                  pl.BlockSpec((D_MODEL, tf), lambda i, j: (0, j)),
                  pl.BlockSpec((D_MODEL, tf), lambda i, j: (0, j)),
                  pl.BlockSpec((tf, D_MODEL), lambda i, j: (j, 0))],
        out_specs=[pl.BlockSpec((tm, D_MODEL), lambda i, j: (i, 0)),
                   pl.BlockSpec((ms, D_MODEL), lambda i, j: (0, 0))],
        out_shape=[jax.ShapeDtypeStruct((m, D_MODEL), F32),
                   jax.ShapeDtypeStruct((ms, D_MODEL), F32)],
        compiler_params=_params(("arbitrary", "arbitrary")),
        name="swiglu_ffn",
    )(x1b, x1b_s, w1, w3, w2)


def _ple_kernel(x1_ref, x1b_ref, ffn_ref, p_ref, wpg_ref, wpe_ref, g2_ref, b2_ref, y_ref):
    tm = x1_ref.shape[0]
    sub = min(tm, ROW_SUBTILE)
    for r in range(tm // sub):
        rows = slice(r * sub, (r + 1) * sub)
        pg = jnp.dot(x1b_ref[rows, :], wpg_ref[...], preferred_element_type=F32)
        pe = jnp.dot(p_ref[rows, :].astype(BF16), wpe_ref[...], preferred_element_type=F32)
        y_ref[rows, :] = _layer_norm(ALPHA * x1_ref[rows, :] + ffn_ref[rows, :] + _sigmoid(pg) * pe,
                                     g2_ref[...], b2_ref[...])


def _ple(x1, x1b, ffn, p, w_pg, w_pe, g2, b2, tm):
    m = x1.shape[0]
    row = lambda i: (i, 0)
    vec = pl.BlockSpec((1, D_MODEL), lambda i: (0, 0))
    return pl.pallas_call(
        _ple_kernel,
        grid=(m // tm,),
        in_specs=[pl.BlockSpec((tm, D_MODEL), row), pl.BlockSpec((tm, D_MODEL), row),
                  pl.BlockSpec((tm, D_MODEL), row), pl.BlockSpec((tm, D_PLE), row),
                  _resident((D_MODEL, D_MODEL)), _resident((D_PLE, D_MODEL)), vec, vec],
        out_specs=pl.BlockSpec((tm, D_MODEL), row),
        out_shape=jax.ShapeDtypeStruct((m, D_MODEL), F32),
        compiler_params=_params(("parallel",)),
        name="ple_ln2",
    )(x1, x1b, ffn, p, w_pg, w_pe, g2, b2)


def _rope_tables(pos):
    half = DK_R // 2
    inv = ROPE_BASE ** (-jnp.arange(half, dtype=F32) / half)
    ang = pos.astype(F32)[:, None] * inv[None, :]
    return jnp.cos(ang), jnp.sin(ang)


def kernel(x_prompt, x_sample, state_ret, cache_k, cache_v, cache_logf, page_table, p_prompt, p_sample, ln_emb_g, ln_emb_b, w_in, b_f, gn_g, w_ret_proj, w_fox_proj, w_out, ln1_g, ln1_b, w1, w3, w2, w_pg, w_pe, ln2_g, ln2_b):
    assert w_in.shape[0] == DEPTH == 1
    batch, seq, _ = x_prompt.shape
    dec_b, dec_t, _ = x_sample.shape
    assert dec_t == 1
    n_pages = page_table.shape[1]
    past_len = n_pages * PAGE_SIZE
    n_pool = cache_k.shape[1]

    wi = w_in[0]
    w_gates = wi[:, FF_HI:]
    w_ff_t = wi[:, FF_LO:FF_HI].T.astype(BF16)
    bf_col = b_f[0].reshape(H_F, 1)
    vec = lambda v: v.reshape(1, D_MODEL)
    eg, eb = vec(ln_emb_g), vec(ln_emb_b)
    g1, b1, g2, b2 = vec(ln1_g[0]), vec(ln1_b[0]), vec(ln2_g[0]), vec(ln2_b[0])
    w_r, w_f, w_o = w_ret_proj[0].astype(BF16), w_fox_proj[0].astype(BF16), w_out[0].astype(BF16)
    w_pgb, w_peb = w_pg[0].astype(BF16), w_pe[0].astype(BF16)
    log_g = jnp.log1p(-jnp.exp2(-5.0 - jnp.arange(H_R, dtype=F32)))
    cos_p, sin_p = _rope_tables(jnp.arange(seq))
    cos_s, sin_s = _rope_tables(jnp.full((dec_b,), past_len))

    xp = x_prompt.reshape(batch * seq, D_MODEL)
    xs = x_sample.reshape(dec_b, D_MODEL)

    h_p = _entry_ln(xp, eg, eb, tm=512)
    z_p, fk_p, fv_p, z_s, fk_s, fv_s = _inproj(h_p, xs, eg, eb, wi, w_gates, cos_p, sin_p, cos_s, sin_s, tm=IN_TM)
    lf_p, c2_row = _logf_prompt(h_p, w_ff_t, bf_col, seq)
    lf_s = _logf_sample(xs, eg, eb, w_ff_t, bf_col)

    o_r_p, st_p = _ret_prompt(z_p, log_g, gn_g[0], batch, seq)
    o_f_p = _fox_prompt(z_p, c2_row, jnp.swapaxes(c2_row, 1, 2), batch, seq)

    z3 = z_s.reshape(dec_b, 1, z_s.shape[1])
    o_r_s, st_s = _ret_sample(z3, log_g, gn_g[0], state_ret[0])
    width = PAGE_SIZE * H_F
    lf_new = jnp.swapaxes(lf_s, 1, 2).reshape(dec_b, H_F)
    lnew = jnp.tile(lf_new, (1, PAGE_SIZE)).reshape(dec_b, 1, width)
    bias = _decode_bias(page_table, cache_logf[0].reshape(n_pool, width), lnew)
    heads = lambda cb: z_s[:, cb * COL_BLOCK:(cb + 1) * COL_BLOCK].reshape(dec_b, H_F, D_F)
    o_f_s = _fox_decode(page_table, heads(CB_FQ), heads(CB_FK), heads(CB_FV), bias,
                        cache_k[0].reshape(n_pool, width, D_F), cache_v[0].reshape(n_pool, width, D_F))

    x1_p, x1b_p = _merge(o_r_p, o_f_p, z_p, xp, eg, eb, w_r, w_f, w_o, g1, b1, tm=MERGE_TM)
    x1_s, x1b_s = _merge(o_r_s.reshape(dec_b, W_R), o_f_s.reshape(dec_b, W_F), z_s, xs, eg, eb,
                         w_r, w_f, w_o, g1, b1, tm=dec_b)
    ffn_p, ffn_s = _ffn(x1b_p, x1b_s, w1[0], w3[0], w2[0], tm=FFN_TM, tf=FFN_TF)
    y_p = _ple(x1_p, x1b_p, ffn_p, p_prompt[0].reshape(batch * seq, D_PLE), w_pgb, w_peb, g2, b2, tm=PLE_TM)
    y_s = _ple(x1_s, x1b_s, ffn_s, p_sample[0].reshape(dec_b, D_PLE), w_pgb, w_peb, g2, b2, tm=dec_b)

    return (y_p.reshape(batch, seq, D_MODEL),
            y_s.reshape(dec_b, 1, D_MODEL),
            st_p[None],
            fk_p.reshape(1, batch, seq, H_F, D_F),
            fv_p.reshape(1, batch, seq, H_F, D_F),
            jnp.swapaxes(lf_p, 1, 2)[None],
            st_s[None],
            fk_s.reshape(1, dec_b, 1, H_F, D_F),
            fv_s.reshape(1, dec_b, 1, H_F, D_F),
            jnp.swapaxes(lf_s, 1, 2).reshape(1, dec_b, 1, H_F))
```

```python
import functools

import jax
import jax.numpy as jnp
import numpy as np
from jax import lax
from jax.experimental import pallas as pl
from jax.experimental.pallas import tpu as pltpu

F32 = jnp.float32
BF16 = jnp.bfloat16

D_MODEL = 2048
DEPTH = 1
PAGE_SIZE = 128
H_R = 4
DK_R = 256
DV_R = 256
W_R = H_R * DV_R
CHUNK = 128
ROPE_BASE = 10000.0
GN_EPS = 1e-6
H_F = 8
D_F = 128
W_F = H_F * D_F
D_FF = 5632
D_PLE = 256
LN_EPS = 1e-5
ALPHA = (2.0 * DEPTH) ** 0.25
FOX_SCALE = D_F ** -0.5
LOG2E = float(np.log2(np.e))

COL_BLOCK = 1024
N_MAIN = 11 * COL_BLOCK
CB_RQ, CB_RK, CB_RV, CB_RG, CB_FQ, CB_FK, CB_FV, CB_GR, CB_GF = 0, 1, 2, 3, 4, 5, 6, 7, 9
FF_LO, FF_HI = 7 * COL_BLOCK, 7 * COL_BLOCK + H_F

IN_TM, IN_TN = 1024, 512
N_A_BLOCKS = FF_LO // IN_TN
FFN_TM, FFN_TF = 1024, 256
MERGE_TM, PLE_TM, ROW_SUBTILE = 256, 512, 256

VMEM_LIMIT = 56 * 1024 * 1024
DECODE_PAGES_PER_STEP = 8

NT_DIMS = (((1,), (1,)), ((), ()))
TN_DIMS = (((0,), (0,)), ((), ()))


def _params(semantics):
    return pltpu.CompilerParams(dimension_semantics=semantics, vmem_limit_bytes=VMEM_LIMIT)


def _layer_norm(x, g, b):
    mu = jnp.mean(x, axis=-1, keepdims=True)
    xc = x - mu
    var = jnp.mean(xc * xc, axis=-1, keepdims=True)
    return xc * lax.rsqrt(var + LN_EPS) * g + b


def _sigmoid(x):
    return 1.0 / (1.0 + jnp.exp(-x))


def _log_sigmoid(x):
    return jnp.minimum(x, 0.0) - jnp.log(1.0 + jnp.exp(-jnp.abs(x)))


def _resident(shape):
    return pl.BlockSpec(shape, lambda *_: (0,) * len(shape), pipeline_mode=pl.Buffered(1))


def _ln_kernel(x_ref, g_ref, b_ref, o_ref):
    o_ref[...] = _layer_norm(x_ref[...], g_ref[...], b_ref[...]).astype(o_ref.dtype)


def _entry_ln(x, g, b, tm):
    m = x.shape[0]
    return pl.pallas_call(
        _ln_kernel,
        grid=(m // tm,),
        in_specs=[pl.BlockSpec((tm, D_MODEL), lambda i: (i, 0)),
                  pl.BlockSpec((1, D_MODEL), lambda i: (0, 0)),
                  pl.BlockSpec((1, D_MODEL), lambda i: (0, 0))],
        out_specs=pl.BlockSpec((tm, D_MODEL), lambda i: (i, 0)),
        out_shape=jax.ShapeDtypeStruct((m, D_MODEL), BF16),
        compiler_params=_params(("parallel",)),
        name="entry_ln",
    )(x, g, b)


def _inproj_block(h, w_ref, z_ref, rope=None, f32_ref=None):
    half = DK_R // 2
    for hd in range(IN_TN // DK_R):
        lo = hd * DK_R
        acc = lax.dot_general(h, w_ref[lo:lo + DK_R, :].astype(BF16), NT_DIMS, preferred_element_type=F32)
        if rope is None:
            z_ref[:, lo:lo + DK_R] = acc.astype(BF16)
        else:
            cos, sin, scale = rope
            x1 = acc[:, :half]
            x2 = acc[:, half:]
            z_ref[:, lo:lo + half] = ((x1 * cos - x2 * sin) * scale).astype(BF16)
            z_ref[:, lo + half:lo + DK_R] = ((x2 * cos + x1 * sin) * scale).astype(BF16)
        if f32_ref is not None:
            f32_ref[:, lo:lo + DK_R] = acc


def _inproj_kernel(h_ref, xs_ref, eg_ref, eb_ref, wa_ref, wg_ref, cos_ref, sin_ref, coss_ref, sins_ref,
                   z_ref, fk_ref, fv_ref, zs_ref, fks_ref, fvs_ref):
    i = pl.program_id(0)
    j = pl.program_id(1)
    per = COL_BLOCK // IN_TN

    def section(lo, hi, w_ref, rope_scale=None, f32_refs=(None, None)):
        @pl.when((j >= lo) & (j < hi))
        def _():
            rope = None if rope_scale is None else (cos_ref[...], sin_ref[...], rope_scale)
            _inproj_block(h_ref[...], w_ref, z_ref, rope, f32_refs[0])

            @pl.when(i == 0)
            def _():
                hs = _layer_norm(xs_ref[...], eg_ref[...], eb_ref[...]).astype(BF16)
                rope_s = None if rope_scale is None else (coss_ref[...], sins_ref[...], rope_scale)
                _inproj_block(hs, w_ref, zs_ref, rope_s, f32_refs[1])

    @pl.when(i > 0)
    def _():
        zs_ref[...] = jnp.zeros_like(zs_ref)

    section(CB_RQ * per, CB_RK * per, wa_ref, rope_scale=1.0)
    section(CB_RK * per, CB_RV * per, wa_ref, rope_scale=DK_R ** -0.5)
    section(CB_RV * per, CB_FK * per, wa_ref)
    section(CB_FK * per, CB_FV * per, wa_ref, f32_refs=(fk_ref, fks_ref))
    section(CB_FV * per, N_A_BLOCKS, wa_ref, f32_refs=(fv_ref, fvs_ref))
    section(N_A_BLOCKS, N_MAIN // IN_TN, wg_ref)


def _inproj(h, xs, eg, eb, w_in_t, cos, sin, cos_s, sin_s, tm):
    m = h.shape[0]
    ms = xs.shape[0]
    n_pos_blocks = cos.shape[0] // tm
    nj = N_MAIN // IN_TN
    per = COL_BLOCK // IN_TN
    half = DK_R // 2
    clamp = lambda j, cb: jnp.clip(j - cb * per, 0, per - 1)
    row = lambda i, j: (i, 0)
    fixed = lambda i, j: (0, 0)
    return pl.pallas_call(
        _inproj_kernel,
        grid=(m // tm, nj),
        in_specs=[pl.BlockSpec((tm, D_MODEL), row),
                  pl.BlockSpec((ms, D_MODEL), fixed),
                  pl.BlockSpec((1, D_MODEL), fixed),
                  pl.BlockSpec((1, D_MODEL), fixed),
                  pl.BlockSpec((IN_TN, D_MODEL), lambda i, j: (jnp.minimum(j, N_A_BLOCKS - 1), 0)),
                  pl.BlockSpec((pl.Element(IN_TN), pl.Element(D_MODEL)),
                               lambda i, j: (pl.multiple_of(FF_HI + IN_TN * jnp.maximum(j - N_A_BLOCKS, 0), 8), 0)),
                  pl.BlockSpec((tm, half), lambda i, j: (i % n_pos_blocks, 0)),
                  pl.BlockSpec((tm, half), lambda i, j: (i % n_pos_blocks, 0)),
                  pl.BlockSpec((ms, half), fixed),
                  pl.BlockSpec((ms, half), fixed)],
        out_specs=[pl.BlockSpec((tm, IN_TN), lambda i, j: (i, j)),
                   pl.BlockSpec((tm, IN_TN), lambda i, j: (i, clamp(j, CB_FK))),
                   pl.BlockSpec((tm, IN_TN), lambda i, j: (i, clamp(j, CB_FV))),
                   pl.BlockSpec((ms, IN_TN), lambda i, j: (0, jnp.where(i == 0, j, nj))),
                   pl.BlockSpec((ms, IN_TN), lambda i, j: (0, jnp.where(i == 0, clamp(j, CB_FK), per - 1))),
                   pl.BlockSpec((ms, IN_TN), lambda i, j: (0, jnp.where(i == 0, clamp(j, CB_FV), per - 1)))],
        out_shape=[jax.ShapeDtypeStruct((m, N_MAIN), BF16),
                   jax.ShapeDtypeStruct((m, W_F), F32),
                   jax.ShapeDtypeStruct((m, W_F), F32),
                   jax.ShapeDtypeStruct((ms, N_MAIN + IN_TN), BF16),
                   jax.ShapeDtypeStruct((ms, W_F), F32),
                   jax.ShapeDtypeStruct((ms, W_F), F32)],
        compiler_params=_params(("arbitrary", "arbitrary")),
        name="in_proj",
    )(h, xs, eg, eb, w_in_t, w_in_t, cos, sin, cos_s, sin_s)


def _cumsum_lanes(x):
    n = x.shape[-1]
    lane = lax.broadcasted_iota(jnp.int32, x.shape, x.ndim - 1)
    s = 1
    while s < n:
        x = x + jnp.where(lane >= s, pltpu.roll(x, s, axis=x.ndim - 1), 0.0)
        s *= 2
    return x


def _logf_prompt_kernel(h_ref, w_ref, bf_ref, lf_ref, c_ref):
    ff = lax.dot_general(w_ref[...], h_ref[...], NT_DIMS, preferred_element_type=F32)
    lf = _log_sigmoid(ff + bf_ref[...])
    lf_ref[0] = lf
    c_ref[0] = _cumsum_lanes(lf) * LOG2E


def _logf_sample_kernel(x_ref, eg_ref, eb_ref, w_ref, bf_ref, lf_ref):
    h = _layer_norm(x_ref[...], eg_ref[...], eb_ref[...]).astype(BF16)
    ff = lax.dot_general(w_ref[...], h, NT_DIMS, preferred_element_type=F32)
    lf_ref[0] = _log_sigmoid(ff + bf_ref[...])


def _logf_prompt(h, w_ff_t, bf_col, seq):
    nb = h.shape[0] // seq
    shape = jax.ShapeDtypeStruct((nb, H_F, seq), F32)
    spec = pl.BlockSpec((1, H_F, seq), lambda b: (b, 0, 0))
    return pl.pallas_call(
        _logf_prompt_kernel,
        grid=(nb,),
        in_specs=[pl.BlockSpec((seq, D_MODEL), lambda b: (b, 0)),
                  pl.BlockSpec((H_F, D_MODEL), lambda b: (0, 0)),
                  pl.BlockSpec((H_F, 1), lambda b: (0, 0))],
        out_specs=[spec, spec],
        out_shape=[shape, shape],
        compiler_params=_params(("parallel",)),
        name="log_forget_cumsum",
    )(h, w_ff_t, bf_col)


def _logf_sample(xs, eg, eb, w_ff_t, bf_col):
    ms = xs.shape[0]
    fixed = lambda b: (0, 0)
    return pl.pallas_call(
        _logf_sample_kernel,
        grid=(1,),
        in_specs=[pl.BlockSpec((ms, D_MODEL), fixed), pl.BlockSpec((1, D_MODEL), fixed),
                  pl.BlockSpec((1, D_MODEL), fixed), pl.BlockSpec((H_F, D_MODEL), fixed),
                  pl.BlockSpec((H_F, 1), fixed)],
        out_specs=pl.BlockSpec((1, H_F, ms), lambda b: (0, 0, 0)),
        out_shape=jax.ShapeDtypeStruct((1, H_F, ms), F32),
        compiler_params=_params(("arbitrary",)),
        name="log_forget",
    )(xs, eg, eb, w_ff_t, bf_col)


def _group_norm_gate(o, gate_bf16, gn_row):
    mu = jnp.mean(o, axis=-1, keepdims=True)
    oc = o - mu
    var = jnp.mean(oc * oc, axis=-1, keepdims=True)
    gate = gate_bf16.astype(F32)
    return oc * lax.rsqrt(var + GN_EPS) * gn_row * (gate * _sigmoid(gate))


def _ret_prompt_kernel(lg_ref, q_ref, k_ref, v_ref, g_ref, gn_ref, o_ref, st_ref):
    c = pl.program_id(1)

    @pl.when(c == 0)
    def _():
        st_ref[...] = jnp.zeros_like(st_ref)

    n = q_ref.shape[0]
    ri = lax.broadcasted_iota(jnp.int32, (n, n), 0)
    ci = lax.broadcasted_iota(jnp.int32, (n, n), 1)
    causal = ri >= ci
    diff = jnp.where(causal, (ri - ci).astype(F32), 0.0)
    pos = lax.broadcasted_iota(jnp.int32, (n, DK_R), 0).astype(F32)
    for hd in range(H_R):
        lg = lg_ref[hd]
        sl = slice(hd * DK_R, (hd + 1) * DK_R)
        q = q_ref[:, sl]
        k = k_ref[:, sl]
        v = v_ref[:, sl]
        dec = jnp.where(causal, jnp.exp(lg * diff), 0.0)
        sc = lax.dot_general(q, k, NT_DIMS, preferred_element_type=F32) * dec
        o = jnp.dot(sc.astype(BF16), v, preferred_element_type=F32)
        state = st_ref[0, hd]
        cross = jnp.exp(lg * (pos + 1.0))
        o = o + jnp.dot(q, state.astype(BF16), preferred_element_type=F32) * cross
        kdec = jnp.exp(lg * (n - 1.0 - pos))
        kd = (k.astype(F32) * kdec).astype(BF16)
        carry = jnp.exp(jnp.full((1, DV_R), lg * n, F32))
        st_ref[0, hd] = carry * state + lax.dot_general(kd, v, TN_DIMS, preferred_element_type=F32)
        o_ref[:, sl] = _group_norm_gate(o, g_ref[:, sl], gn_ref[hd:hd + 1, :]).astype(BF16)


def _ret_prompt(z, log_g, gn_g, batch, seq):
    nc = seq // CHUNK
    row = lambda b, c: b * nc + c
    zspec = lambda cb: pl.BlockSpec((CHUNK, COL_BLOCK), lambda b, c: (row(b, c), cb))
    return pl.pallas_call(
        _ret_prompt_kernel,
        grid=(batch, nc),
        in_specs=[pl.BlockSpec(memory_space=pltpu.SMEM),
                  zspec(CB_RQ), zspec(CB_RK), zspec(CB_RV), zspec(CB_RG),
                  pl.BlockSpec((H_R, DV_R), lambda b, c: (0, 0))],
        out_specs=[pl.BlockSpec((CHUNK, W_R), lambda b, c: (row(b, c), 0)),
                   pl.BlockSpec((1, H_R, DK_R, DV_R), lambda b, c: (b, 0, 0, 0))],
        out_shape=[jax.ShapeDtypeStruct((batch * seq, W_R), BF16),
                   jax.ShapeDtypeStruct((batch, H_R, DK_R, DV_R), F32)],
        compiler_params=_params(("parallel", "arbitrary")),
        name="retention_prompt",
    )(log_g, z, z, z, z, gn_g)


def _ret_sample_kernel(lg_ref, q_ref, k_ref, v_ref, g_ref, gn_ref, st_ref, o_ref, ns_ref):
    row0 = lax.broadcasted_iota(jnp.int32, (8, DK_R), 0) == 0
    for hd in range(H_R):
        lg = lg_ref[hd]
        sl = slice(hd * DK_R, (hd + 1) * DK_R)
        q = q_ref[0, :, sl]
        k = k_ref[0, :, sl]
        v = v_ref[0, :, sl]
        qf = q.astype(F32)
        kf = k.astype(F32)
        vf = v.astype(F32)
        state = st_ref[0, hd]
        decay = jnp.exp(jnp.full((1, DV_R), lg, F32))
        score = jnp.sum(qf * kf, axis=-1, keepdims=True)
        q8 = jnp.broadcast_to(q, (8, DK_R))
        qs = jnp.dot(q8, state.astype(BF16), preferred_element_type=F32)[0:1, :]
        o = score * vf + qs * decay
        k8 = jnp.where(row0, jnp.broadcast_to(kf, (8, DK_R)), 0.0).astype(BF16)
        v8 = jnp.broadcast_to(v, (8, DV_R))
        ns_ref[0, hd] = decay * state + lax.dot_general(k8, v8, TN_DIMS, preferred_element_type=F32)
        o_ref[0, :, sl] = _group_norm_gate(o, g_ref[0, :, sl], gn_ref[hd:hd + 1, :]).astype(BF16)


def _ret_sample(z3, log_g, gn_g, state):
    nb = z3.shape[0]
    zspec = lambda cb: pl.BlockSpec((1, 1, COL_BLOCK), lambda b: (b, 0, cb))
    st_spec = pl.BlockSpec((1, H_R, DK_R, DV_R), lambda b: (b, 0, 0, 0))
    return pl.pallas_call(
        _ret_sample_kernel,
        grid=(nb,),
        in_specs=[pl.BlockSpec(memory_space=pltpu.SMEM),
                  zspec(CB_RQ), zspec(CB_RK), zspec(CB_RV), zspec(CB_RG),
                  pl.BlockSpec((H_R, DV_R), lambda b: (0, 0)),
                  st_spec],
        out_specs=[pl.BlockSpec((1, 1, W_R), lambda b: (b, 0, 0)), st_spec],
        out_shape=[jax.ShapeDtypeStruct((nb, 1, W_R), BF16),
                   jax.ShapeDtypeStruct(state.shape, F32)],
        compiler_params=_params(("parallel",)),
        name="retention_sample",
    )(log_g, z3, z3, z3, z3, gn_g, state)


def _fox_prompt_kernel(q_ref, k_ref, v_ref, crow_ref, ccol_ref, o_ref, vt_ref, m_ref, l_ref, acc_ref, *, tq):
    qi = pl.program_id(1)
    nk = vt_ref.shape[1]

    @pl.when(qi == 0)
    def _():
        for hd in range(H_F):
            for kj in range(nk):
                v_tile = v_ref[kj * tq:(kj + 1) * tq, hd * D_F:(hd + 1) * D_F]
                vt_ref[hd, kj] = v_tile.astype(F32).T.astype(BF16)

    m_ref[...] = jnp.full(m_ref.shape, -jnp.inf, F32)
    l_ref[...] = jnp.zeros_like(l_ref)
    acc_ref[...] = jnp.zeros_like(acc_ref)
    key_i = lax.broadcasted_iota(jnp.int32, (tq, tq), 0)
    qry_i = lax.broadcasted_iota(jnp.int32, (tq, tq), 1)
    visible = key_i <= qry_i

    heads = lambda ref, rows: jnp.stack([ref[rows, hd * D_F:(hd + 1) * D_F] for hd in range(H_F)], axis=0)
    q3 = heads(q_ref, slice(None))
    c_q = crow_ref[0, :, pl.ds(qi, 1), :]

    def tile(kj, on_diagonal):
        ks = pl.multiple_of(kj * tq, tq)
        k3 = heads(k_ref, pl.ds(ks, tq))
        s = lax.dot_general(k3, q3, (((2,), (2,)), ((0,), (0,))), preferred_element_type=F32)
        c_k = jnp.stack([ccol_ref[0, pl.ds(ks, tq), hd:hd + 1] for hd in range(H_F)], axis=0)
        t = s * (FOX_SCALE * LOG2E) + (c_q - c_k)
        if on_diagonal:
            t = jnp.where(visible[None], t, -jnp.inf)
        m_old = m_ref[...]
        m_new = jnp.maximum(m_old, jnp.max(t, axis=1, keepdims=True))
        alpha = jnp.exp2(m_old - m_new)
        p = jnp.exp2(t - m_new)
        l_ref[...] = alpha * l_ref[...] + jnp.sum(p, axis=1, keepdims=True)
        m_ref[...] = m_new
        pv = lax.dot_general(vt_ref[:, kj], p.astype(BF16), (((2,), (1,)), ((0,), (0,))),
                             preferred_element_type=F32)
        acc_ref[...] = alpha * acc_ref[...] + pv

    def body(kj, carry):
        tile(kj, False)
        return carry

    lax.fori_loop(0, qi, body, 0)
    tile(qi, True)
    for hd in range(H_F):
        o_ref[:, hd * D_F:(hd + 1) * D_F] = (acc_ref[hd] / l_ref[hd]).T.astype(BF16)


def _fox_prompt(z, c2_row, c2_col, batch, seq, tq=256):
    nq = seq // tq
    return pl.pallas_call(
        functools.partial(_fox_prompt_kernel, tq=tq),
        grid=(batch, nq),
        in_specs=[pl.BlockSpec((tq, COL_BLOCK), lambda b, i: (b * nq + i, CB_FQ)),
                  pl.BlockSpec((seq, COL_BLOCK), lambda b, i: (b, CB_FK)),
                  pl.BlockSpec((seq, COL_BLOCK), lambda b, i: (b, CB_FV)),
                  pl.BlockSpec((1, H_F, nq, tq), lambda b, i: (b, 0, 0, 0)),
                  pl.BlockSpec((1, seq, H_F), lambda b, i: (b, 0, 0))],
        out_specs=pl.BlockSpec((tq, W_F), lambda b, i: (b * nq + i, 0)),
        out_shape=jax.ShapeDtypeStruct((batch * seq, W_F), BF16),
        scratch_shapes=[pltpu.VMEM((H_F, nq, D_F, tq), BF16),
                        pltpu.VMEM((H_F, 1, tq), F32),
                        pltpu.VMEM((H_F, 1, tq), F32),
                        pltpu.VMEM((H_F, D_F, tq), F32)],
        compiler_params=_params(("parallel", "arbitrary")),
        name="fox_prompt",
    )(z, z, z, c2_row.reshape(batch, H_F, nq, tq), c2_col)


def _decode_bias_kernel(pt_ref, pool_ref, lnew_ref, bias_ref, rows_ref, *, n_pages):
    b = pl.program_id(0)
    width = PAGE_SIZE * H_F
    for p in range(n_pages):
        rows_ref[p:p + 1, :] = pool_ref[pl.ds(pt_ref[b, p], 1), :]
    x = rows_ref[...]
    lane = lax.broadcasted_iota(jnp.int32, x.shape, 1)
    within = x
    total = x
    s = H_F
    while s < width:
        within = within + jnp.where(lane >= s, pltpu.roll(within, s, axis=1), 0.0)
        total = total + pltpu.roll(total, s, axis=1)
        s *= 2
    rest = total - within
    run = lnew_ref[0]
    for p in reversed(range(n_pages)):
        bias_ref[0, p:p + 1, :] = run + rest[p:p + 1, :]
        run = run + total[p:p + 1, :]


def _decode_bias(page_table, logf_pool, lnew):
    nb, n_pages = page_table.shape
    width = PAGE_SIZE * H_F
    return pl.pallas_call(
        functools.partial(_decode_bias_kernel, n_pages=n_pages),
        grid_spec=pltpu.PrefetchScalarGridSpec(
            num_scalar_prefetch=1,
            grid=(nb,),
            in_specs=[pl.BlockSpec(logf_pool.shape, lambda b, pt: (0, 0), pipeline_mode=pl.Buffered(1)),
                      pl.BlockSpec((1, 1, width), lambda b, pt: (b, 0, 0))],
            out_specs=pl.BlockSpec((1, n_pages, width), lambda b, pt: (b, 0, 0)),
            scratch_shapes=[pltpu.VMEM((n_pages, width), F32)]),
        out_shape=jax.ShapeDtypeStruct((nb, n_pages, width), F32),
        compiler_params=_params(("arbitrary",)),
        name="decode_forget_bias",
    )(page_table, logf_pool, lnew)


def _fox_decode_kernel(pt_ref, q_ref, kn_ref, vn_ref, bias_ref, *refs, pages):
    k_refs = refs[:pages]
    v_refs = refs[pages:2 * pages]
    o_ref, m_ref, l_ref, acc_ref = refs[2 * pages:]
    g = pl.program_id(1)
    width = PAGE_SIZE * H_F
    row = lax.broadcasted_iota(jnp.int32, (H_F, width), 0)
    lane = lax.broadcasted_iota(jnp.int32, (H_F, width), 1)
    own = (lane % H_F) == row
    q = q_ref[0]

    @pl.when(g == 0)
    def _():
        s_new = jnp.sum(q.astype(F32) * kn_ref[0].astype(F32), axis=-1, keepdims=True) * FOX_SCALE
        m_ref[...] = jnp.broadcast_to(s_new, m_ref.shape)
        l_ref[...] = jnp.ones_like(l_ref)
        acc_ref[...] = vn_ref[0].astype(F32)

    s_parts = []
    for i in range(pages):
        kt = k_refs[i][...].astype(BF16)
        s = lax.dot_general(q, kt, NT_DIMS, preferred_element_type=F32)
        s_parts.append(jnp.where(own, s * FOX_SCALE + bias_ref[0, i:i + 1, :], -jnp.inf))
    m_old = m_ref[:, 0:1]
    m_new = m_old
    for s in s_parts:
        m_new = jnp.maximum(m_new, jnp.max(s, axis=-1, keepdims=True))
    alpha = jnp.exp(m_old - m_new)
    l_new = alpha * l_ref[:, 0:1]
    pv = jnp.zeros((H_F, D_F), F32)
    for i in range(pages):
        p = jnp.exp(s_parts[i] - m_new)
        l_new = l_new + jnp.sum(p, axis=-1, keepdims=True)
        pv = pv + jnp.dot(p.astype(BF16), v_refs[i][...].astype(BF16), preferred_element_type=F32)
    m_ref[...] = jnp.broadcast_to(m_new, m_ref.shape)
    l_ref[...] = jnp.broadcast_to(l_new, l_ref.shape)
    acc_ref[...] = alpha * acc_ref[...] + pv

    @pl.when(g == pl.num_programs(1) - 1)
    def _():
        o_ref[0] = (acc_ref[...] / l_ref[:, 0:1]).astype(BF16)


def _fox_decode(page_table, q, k_new, v_new, bias, k_pool, v_pool, pages=DECODE_PAGES_PER_STEP):
    nb, n_pages = page_table.shape
    width = PAGE_SIZE * H_F
    head_spec = pl.BlockSpec((1, H_F, D_F), lambda b, g, pt: (b, 0, 0))
    page_spec = lambda i: pl.BlockSpec((None, width, D_F), lambda b, g, pt: (pt[b, g * pages + i], 0, 0))
    return pl.pallas_call(
        functools.partial(_fox_decode_kernel, pages=pages),
        grid_spec=pltpu.PrefetchScalarGridSpec(
            num_scalar_prefetch=1,
            grid=(nb, n_pages // pages),
            in_specs=[head_spec, head_spec, head_spec,
                      pl.BlockSpec((1, pages, width), lambda b, g, pt: (b, g, 0))]
                     + [page_spec(i) for i in range(pages)] * 2,
            out_specs=head_spec,
            scratch_shapes=[pltpu.VMEM((H_F, D_F), F32),
                            pltpu.VMEM((H_F, D_F), F32),
                            pltpu.VMEM((H_F, D_F), F32)]),
        out_shape=jax.ShapeDtypeStruct((nb, H_F, D_F), BF16),
        compiler_params=_params(("parallel", "arbitrary")),
        name="fox_decode",
    )(page_table, q, k_new, v_new, bias, *([k_pool] * pages), *([v_pool] * pages))


def _merge_kernel(or_ref, of_ref, gr0_ref, gr1_ref, gf0_ref, gf1_ref, x_ref, eg_ref, eb_ref,
                  wr_ref, wf_ref, wo_ref, g1_ref, b1_ref, x1_ref, x1b_ref):
    tm = x_ref.shape[0]
    sub = min(tm, ROW_SUBTILE)
    for r in range(tm // sub):
        rows = slice(r * sub, (r + 1) * sub)
        y_r = jnp.dot(or_ref[rows, :], wr_ref[...], preferred_element_type=F32)
        y_f = jnp.dot(of_ref[rows, :], wf_ref[...], preferred_element_type=F32)
        halves = []
        for idx, (gr_ref, gf_ref) in enumerate(((gr0_ref, gf0_ref), (gr1_ref, gf1_ref))):
            sl = slice(idx * COL_BLOCK, (idx + 1) * COL_BLOCK)
            mixed = (_sigmoid(gr_ref[rows, :].astype(F32)) * y_r[:, sl]
                     + _sigmoid(gf_ref[rows, :].astype(F32)) * y_f[:, sl])
            halves.append(mixed.astype(BF16))
        mix = jnp.dot(jnp.concatenate(halves, axis=1), wo_ref[...], preferred_element_type=F32)
        h = _layer_norm(x_ref[rows, :], eg_ref[...], eb_ref[...])
        x1 = _layer_norm(ALPHA * h + mix, g1_ref[...], b1_ref[...])
        x1_ref[rows, :] = x1
        x1b_ref[rows, :] = x1.astype(BF16)


def _merge(o_r, o_f, z, x, eg, eb, w_r, w_f, w_o, g1, b1, tm):
    m = x.shape[0]
    row = lambda i: (i, 0)
    vec = pl.BlockSpec((1, D_MODEL), lambda i: (0, 0))
    zspec = lambda cb: pl.BlockSpec((tm, COL_BLOCK), lambda i: (i, cb))
    return pl.pallas_call(
        _merge_kernel,
        grid=(m // tm,),
        in_specs=[pl.BlockSpec((tm, W_R), row), pl.BlockSpec((tm, W_F), row),
                  zspec(CB_GR), zspec(CB_GR + 1), zspec(CB_GF), zspec(CB_GF + 1),
                  pl.BlockSpec((tm, D_MODEL), row), vec, vec,
                  _resident((W_R, D_MODEL)), _resident((W_F, D_MODEL)), _resident((D_MODEL, D_MODEL)),
                  vec, vec],
        out_specs=[pl.BlockSpec((tm, D_MODEL), row), pl.BlockSpec((tm, D_MODEL), row)],
        out_shape=[jax.ShapeDtypeStruct((m, D_MODEL), F32), jax.ShapeDtypeStruct((m, D_MODEL), BF16)],
        compiler_params=_params(("parallel",)),
        name="merge_out_ln1",
    )(o_r, o_f, z, z, z, z, x, eg, eb, w_r, w_f, w_o, g1, b1)


def _ffn_kernel(x_ref, xs_ref, w1_ref, w3_ref, w2_ref, o_ref, os_ref):
    i = pl.program_id(0)
    j = pl.program_id(1)
    @pl.when(j == 0)
    def _():
        o_ref[...] = jnp.zeros_like(o_ref)

    @pl.when((j == 0) & (i == 0))
    def _():
        os_ref[...] = jnp.zeros_like(os_ref)

    w1 = w1_ref[...].astype(BF16)
    w3 = w3_ref[...].astype(BF16)
    w2 = w2_ref[...].astype(BF16)

    def accumulate(x_in, out):
        xb = x_in[...]
        a = jnp.dot(xb, w1, preferred_element_type=F32)
        b = jnp.dot(xb, w3, preferred_element_type=F32)
        mid = (a * _sigmoid(a) * b).astype(BF16)
        out[...] += jnp.dot(mid, w2, preferred_element_type=F32)

    accumulate(x_ref, o_ref)
    pl.when(i == 0)(lambda: accumulate(xs_ref, os_ref))


def _ffn(x1b, x1b_s, w1, w3, w2, tm, tf):
    m = x1b.shape[0]
    ms = x1b_s.shape[0]
    return pl.pallas_call(
        _ffn_kernel,
        grid=(m // tm, D_FF // tf),
        in_specs=[pl.BlockSpec((tm, D_MODEL), lambda i, j: (i, 0)),
                  pl.BlockSpec((ms, D_MODEL), lambda i, j: (0, 0)),
                  pl.BlockSpec((D_MODEL, tf), lambda i, j: (0, j)),
                  pl.BlockSpec((D_MODEL, tf), lambda i, j: (0, j)),
                  pl.BlockSpec((tf, D_MODEL), lambda i, j: (j, 0))],
        out_specs=[pl.BlockSpec((tm, D_MODEL), lambda i, j: (i, 0)),
                   pl.BlockSpec((ms, D_MODEL), lambda i, j: (0, 0))],
        out_shape=[jax.ShapeDtypeStruct((m, D_MODEL), F32),
                   jax.ShapeDtypeStruct((ms, D_MODEL), F32)],
        compiler_params=_params(("arbitrary", "arbitrary")),
        name="swiglu_ffn",
    )(x1b, x1b_s, w1, w3, w2)


def _ple_kernel(x1_ref, x1b_ref, ffn_ref, p_ref, wpg_ref, wpe_ref, g2_ref, b2_ref, y_ref):
    tm = x1_ref.shape[0]
    sub = min(tm, ROW_SUBTILE)
    for r in range(tm // sub):
        rows = slice(r * sub, (r + 1) * sub)
        pg = jnp.dot(x1b_ref[rows, :], wpg_ref[...], preferred_element_type=F32)
        pe = jnp.dot(p_ref[rows, :].astype(BF16), wpe_ref[...], preferred_element_type=F32)
        y_ref[rows, :] = _layer_norm(ALPHA * x1_ref[rows, :] + ffn_ref[rows, :] + _sigmoid(pg) * pe,
                                     g2_ref[...], b2_ref[...])


def _ple(x1, x1b, ffn, p, w_pg, w_pe, g2, b2, tm):
    m = x1.shape[0]
    row = lambda i: (i, 0)
    vec = pl.BlockSpec((1, D_MODEL), lambda i: (0, 0))
    return pl.pallas_call(
        _ple_kernel,
        grid=(m // tm,),
        in_specs=[pl.BlockSpec((tm, D_MODEL), row), pl.BlockSpec((tm, D_MODEL), row),
                  pl.BlockSpec((tm, D_MODEL), row), pl.BlockSpec((tm, D_PLE), row),
                  _resident((D_MODEL, D_MODEL)), _resident((D_PLE, D_MODEL)), vec, vec],
        out_specs=pl.BlockSpec((tm, D_MODEL), row),
        out_shape=jax.ShapeDtypeStruct((m, D_MODEL), F32),
        compiler_params=_params(("parallel",)),
        name="ple_ln2",
    )(x1, x1b, ffn, p, w_pg, w_pe, g2, b2)


def _rope_tables(pos):
    half = DK_R // 2
    inv = ROPE_BASE ** (-jnp.arange(half, dtype=F32) / half)
    ang = pos.astype(F32)[:, None] * inv[None, :]
    return jnp.cos(ang), jnp.sin(ang)


def kernel(x_prompt, x_sample, state_ret, cache_k, cache_v, cache_logf, page_table, p_prompt, p_sample, ln_emb_g, ln_emb_b, w_in, b_f, gn_g, w_ret_proj, w_fox_proj, w_out, ln1_g, ln1_b, w1, w3, w2, w_pg, w_pe, ln2_g, ln2_b):
    assert w_in.shape[0] == DEPTH == 1
    batch, seq, _ = x_prompt.shape
    dec_b, dec_t, _ = x_sample.shape
    assert dec_t == 1
    n_pages = page_table.shape[1]
    past_len = n_pages * PAGE_SIZE
    n_pool = cache_k.shape[1]

    w_in_t = jnp.swapaxes(w_in[0], 0, 1)
    w_ff_t = w_in_t[FF_LO:FF_HI].astype(BF16)
    bf_col = b_f[0].reshape(H_F, 1)
    vec = lambda v: v.reshape(1, D_MODEL)
    eg, eb = vec(ln_emb_g), vec(ln_emb_b)
    g1, b1, g2, b2 = vec(ln1_g[0]), vec(ln1_b[0]), vec(ln2_g[0]), vec(ln2_b[0])
    w_r, w_f, w_o = w_ret_proj[0].astype(BF16), w_fox_proj[0].astype(BF16), w_out[0].astype(BF16)
    w_pgb, w_peb = w_pg[0].astype(BF16), w_pe[0].astype(BF16)
    log_g = jnp.log1p(-jnp.exp2(-5.0 - jnp.arange(H_R, dtype=F32)))
    cos_p, sin_p = _rope_tables(jnp.arange(seq))
    cos_s, sin_s = _rope_tables(jnp.full((dec_b,), past_len))

    xp = x_prompt.reshape(batch * seq, D_MODEL)
    xs = x_sample.reshape(dec_b, D_MODEL)

    h_p = _entry_ln(xp, eg, eb, tm=512)
    z_p, fk_p, fv_p, z_s, fk_s, fv_s = _inproj(h_p, xs, eg, eb, w_in_t, cos_p, sin_p, cos_s, sin_s, tm=IN_TM)
    lf_p, c2_row = _logf_prompt(h_p, w_ff_t, bf_col, seq)
    lf_s = _logf_sample(xs, eg, eb, w_ff_t, bf_col)

    o_r_p, st_p = _ret_prompt(z_p, log_g, gn_g[0], batch, seq)
    o_f_p = _fox_prompt(z_p, c2_row, jnp.swapaxes(c2_row, 1, 2), batch, seq)

    z3 = z_s.reshape(dec_b, 1, z_s.shape[1])
    o_r_s, st_s = _ret_sample(z3, log_g, gn_g[0], state_ret[0])
    width = PAGE_SIZE * H_F
    lf_new = jnp.swapaxes(lf_s, 1, 2).reshape(dec_b, H_F)
    lnew = jnp.tile(lf_new, (1, PAGE_SIZE)).reshape(dec_b, 1, width)
    bias = _decode_bias(page_table, cache_logf[0].reshape(n_pool, width), lnew)
    heads = lambda cb: z_s[:, cb * COL_BLOCK:(cb + 1) * COL_BLOCK].reshape(dec_b, H_F, D_F)
    o_f_s = _fox_decode(page_table, heads(CB_FQ), heads(CB_FK), heads(CB_FV), bias,
                        cache_k[0].reshape(n_pool, width, D_F), cache_v[0].reshape(n_pool, width, D_F))

    x1_p, x1b_p = _merge(o_r_p, o_f_p, z_p, xp, eg, eb, w_r, w_f, w_o, g1, b1, tm=MERGE_TM)
    x1_s, x1b_s = _merge(o_r_s.reshape(dec_b, W_R), o_f_s.reshape(dec_b, W_F), z_s, xs, eg, eb,
                         w_r, w_f, w_o, g1, b1, tm=dec_b)
    ffn_p, ffn_s = _ffn(x1b_p, x1b_s, w1[0], w3[0], w2[0], tm=FFN_TM, tf=FFN_TF)
    y_p = _ple(x1_p, x1b_p, ffn_p, p_prompt[0].reshape(batch * seq, D_PLE), w_pgb, w_peb, g2, b2, tm=PLE_TM)
    y_s = _ple(x1_s, x1b_s, ffn_s, p_sample[0].reshape(dec_b, D_PLE), w_pgb, w_peb, g2, b2, tm=dec_b)

    return (y_p.reshape(batch, seq, D_MODEL),
            y_s.reshape(dec_b, 1, D_MODEL),
            st_p[None],
            fk_p.reshape(1, batch, seq, H_F, D_F),
            fv_p.reshape(1, batch, seq, H_F, D_F),
            jnp.swapaxes(lf_p, 1, 2)[None],
            st_s[None],
            fk_s.reshape(1, dec_b, 1, H_F, D_F),
            fv_s.reshape(1, dec_b, 1, H_F, D_F),
            jnp.swapaxes(lf_s, 1, 2).reshape(1, dec_b, 1, H_F))
```

```python
import functools

import jax
import jax.numpy as jnp
import numpy as np
from jax import lax
from jax.experimental import pallas as pl
from jax.experimental.pallas import tpu as pltpu

F32 = jnp.float32
BF16 = jnp.bfloat16

D_MODEL = 2048
DEPTH = 1
PAGE_SIZE = 128
H_R = 4
DK_R = 256
DV_R = 256
W_R = H_R * DV_R
CHUNK = 128
ROPE_BASE = 10000.0
GN_EPS = 1e-6
H_F = 8
D_F = 128
W_F = H_F * D_F
D_FF = 5632
D_PLE = 256
LN_EPS = 1e-5
ALPHA = (2.0 * DEPTH) ** 0.25
FOX_SCALE = D_F ** -0.5
LOG2E = float(np.log2(np.e))

COL_BLOCK = 1024
N_MAIN = 11 * COL_BLOCK
CB_RQ, CB_RK, CB_RV, CB_RG, CB_FQ, CB_FK, CB_FV, CB_GR, CB_GF = 0, 1, 2, 3, 4, 5, 6, 7, 9
FF_LO, FF_HI = 7 * COL_BLOCK, 7 * COL_BLOCK + H_F

IN_TM, IN_TN = 1024, 512
N_A_BLOCKS = FF_LO // IN_TN
FFN_TM, FFN_TF = 1024, 256
MERGE_TM, PLE_TM, ROW_SUBTILE = 256, 512, 256

VMEM_LIMIT = 56 * 1024 * 1024
DECODE_PAGES_PER_STEP = 8

NT_DIMS = (((1,), (1,)), ((), ()))
TN_DIMS = (((0,), (0,)), ((), ()))


def _params(semantics):
    return pltpu.CompilerParams(dimension_semantics=semantics, vmem_limit_bytes=VMEM_LIMIT)


def _layer_norm(x, g, b):
    mu = jnp.mean(x, axis=-1, keepdims=True)
    xc = x - mu
    var = jnp.mean(xc * xc, axis=-1, keepdims=True)
    return xc * lax.rsqrt(var + LN_EPS) * g + b


def _sigmoid(x):
    return 1.0 / (1.0 + jnp.exp(-x))


def _log_sigmoid(x):
    return jnp.minimum(x, 0.0) - jnp.log(1.0 + jnp.exp(-jnp.abs(x)))


def _resident(shape):
    return pl.BlockSpec(shape, lambda *_: (0,) * len(shape), pipeline_mode=pl.Buffered(1))


def _ln_kernel(x_ref, g_ref, b_ref, o_ref):
    o_ref[...] = _layer_norm(x_ref[...], g_ref[...], b_ref[...]).astype(o_ref.dtype)


def _entry_ln(x, g, b, tm):
    m = x.shape[0]
    return pl.pallas_call(
        _ln_kernel,
        grid=(m // tm,),
        in_specs=[pl.BlockSpec((tm, D_MODEL), lambda i: (i, 0)),
                  pl.BlockSpec((1, D_MODEL), lambda i: (0, 0)),
                  pl.BlockSpec((1, D_MODEL), lambda i: (0, 0))],
        out_specs=pl.BlockSpec((tm, D_MODEL), lambda i: (i, 0)),
        out_shape=jax.ShapeDtypeStruct((m, D_MODEL), BF16),
        compiler_params=_params(("parallel",)),
        name="entry_ln",
    )(x, g, b)


def _inproj_block(h, w_ref, z_ref, rope=None, f32_ref=None):
    half = DK_R // 2
    for hd in range(IN_TN // DK_R):
        lo = hd * DK_R
        acc = lax.dot_general(h, w_ref[lo:lo + DK_R, :].astype(BF16), NT_DIMS, preferred_element_type=F32)
        if rope is None:
            z_ref[:, lo:lo + DK_R] = acc.astype(BF16)
        else:
            cos, sin, scale = rope
            x1 = acc[:, :half]
            x2 = acc[:, half:]
            z_ref[:, lo:lo + half] = ((x1 * cos - x2 * sin) * scale).astype(BF16)
            z_ref[:, lo + half:lo + DK_R] = ((x2 * cos + x1 * sin) * scale).astype(BF16)
        if f32_ref is not None:
            f32_ref[:, lo:lo + DK_R] = acc


def _inproj_kernel(h_ref, xs_ref, eg_ref, eb_ref, wa_ref, wg_ref, cos_ref, sin_ref, coss_ref, sins_ref,
                   z_ref, fk_ref, fv_ref, zs_ref, fks_ref, fvs_ref):
    i = pl.program_id(0)
    j = pl.program_id(1)
    per = COL_BLOCK // IN_TN

    def section(lo, hi, w_ref, rope_scale=None, f32_refs=(None, None)):
        @pl.when((j >= lo) & (j < hi))
        def _():
            rope = None if rope_scale is None else (cos_ref[...], sin_ref[...], rope_scale)
            _inproj_block(h_ref[...], w_ref, z_ref, rope, f32_refs[0])

            @pl.when(i == 0)
            def _():
                hs = _layer_norm(xs_ref[...], eg_ref[...], eb_ref[...]).astype(BF16)
                rope_s = None if rope_scale is None else (coss_ref[...], sins_ref[...], rope_scale)
                _inproj_block(hs, w_ref, zs_ref, rope_s, f32_refs[1])

    @pl.when(i > 0)
    def _():
        zs_ref[...] = jnp.zeros_like(zs_ref)

    section(CB_RQ * per, CB_RK * per, wa_ref, rope_scale=1.0)
    section(CB_RK * per, CB_RV * per, wa_ref, rope_scale=DK_R ** -0.5)
    section(CB_RV * per, CB_FK * per, wa_ref)
    section(CB_FK * per, CB_FV * per, wa_ref, f32_refs=(fk_ref, fks_ref))
    section(CB_FV * per, N_A_BLOCKS, wa_ref, f32_refs=(fv_ref, fvs_ref))
    section(N_A_BLOCKS, N_MAIN // IN_TN, wg_ref)


def _inproj(h, xs, eg, eb, w_in_t, cos, sin, cos_s, sin_s, tm):
    m = h.shape[0]
    ms = xs.shape[0]
    n_pos_blocks = cos.shape[0] // tm
    nj = N_MAIN // IN_TN
    per = COL_BLOCK // IN_TN
    half = DK_R // 2
    clamp = lambda j, cb: jnp.clip(j - cb * per, 0, per - 1)
    row = lambda i, j: (i, 0)
    fixed = lambda i, j: (0, 0)
    return pl.pallas_call(
        _inproj_kernel,
        grid=(m // tm, nj),
        in_specs=[pl.BlockSpec((tm, D_MODEL), row),
                  pl.BlockSpec((ms, D_MODEL), fixed),
                  pl.BlockSpec((1, D_MODEL), fixed),
                  pl.BlockSpec((1, D_MODEL), fixed),
                  pl.BlockSpec((IN_TN, D_MODEL), lambda i, j: (jnp.minimum(j, N_A_BLOCKS - 1), 0)),
                  pl.BlockSpec((pl.Element(IN_TN), pl.Element(D_MODEL)),
                               lambda i, j: (pl.multiple_of(FF_HI + IN_TN * jnp.maximum(j - N_A_BLOCKS, 0), 8), 0)),
                  pl.BlockSpec((tm, half), lambda i, j: (i % n_pos_blocks, 0)),
                  pl.BlockSpec((tm, half), lambda i, j: (i % n_pos_blocks, 0)),
                  pl.BlockSpec((ms, half), fixed),
                  pl.BlockSpec((ms, half), fixed)],
        out_specs=[pl.BlockSpec((tm, IN_TN), lambda i, j: (i, j)),
                   pl.BlockSpec((tm, IN_TN), lambda i, j: (i, clamp(j, CB_FK))),
                   pl.BlockSpec((tm, IN_TN), lambda i, j: (i, clamp(j, CB_FV))),
                   pl.BlockSpec((ms, IN_TN), lambda i, j: (0, jnp.where(i == 0, j, nj))),
                   pl.BlockSpec((ms, IN_TN), lambda i, j: (0, jnp.where(i == 0, clamp(j, CB_FK), per - 1))),
                   pl.BlockSpec((ms, IN_TN), lambda i, j: (0, jnp.where(i == 0, clamp(j, CB_FV), per - 1)))],
        out_shape=[jax.ShapeDtypeStruct((m, N_MAIN), BF16),
                   jax.ShapeDtypeStruct((m, W_F), F32),
                   jax.ShapeDtypeStruct((m, W_F), F32),
                   jax.ShapeDtypeStruct((ms, N_MAIN + IN_TN), BF16),
                   jax.ShapeDtypeStruct((ms, W_F), F32),
                   jax.ShapeDtypeStruct((ms, W_F), F32)],
        compiler_params=_params(("arbitrary", "arbitrary")),
        name="in_proj",
    )(h, xs, eg, eb, w_in_t, w_in_t, cos, sin, cos_s, sin_s)


def _cumsum_lanes(x):
    n = x.shape[-1]
    lane = lax.broadcasted_iota(jnp.int32, x.shape, x.ndim - 1)
    s = 1
    while s < n:
        x = x + jnp.where(lane >= s, pltpu.roll(x, s, axis=x.ndim - 1), 0.0)
        s *= 2
    return x


def _logf_prompt_kernel(h_ref, w_ref, bf_ref, lf_ref, c_ref):
    ff = lax.dot_general(w_ref[...], h_ref[...], NT_DIMS, preferred_element_type=F32)
    lf = _log_sigmoid(ff + bf_ref[...])
    lf_ref[0] = lf
    c_ref[0] = _cumsum_lanes(lf) * LOG2E


def _logf_sample_kernel(x_ref, eg_ref, eb_ref, w_ref, bf_ref, lf_ref):
    h = _layer_norm(x_ref[...], eg_ref[...], eb_ref[...]).astype(BF16)
    ff = lax.dot_general(w_ref[...], h, NT_DIMS, preferred_element_type=F32)
    lf_ref[0] = _log_sigmoid(ff + bf_ref[...])


def _logf_prompt(h, w_ff_t, bf_col, seq):
    nb = h.shape[0] // seq
    shape = jax.ShapeDtypeStruct((nb, H_F, seq), F32)
    spec = pl.BlockSpec((1, H_F, seq), lambda b: (b, 0, 0))
    return pl.pallas_call(
        _logf_prompt_kernel,
        grid=(nb,),
        in_specs=[pl.BlockSpec((seq, D_MODEL), lambda b: (b, 0)),
                  pl.BlockSpec((H_F, D_MODEL), lambda b: (0, 0)),
                  pl.BlockSpec((H_F, 1), lambda b: (0, 0))],
        out_specs=[spec, spec],
        out_shape=[shape, shape],
        compiler_params=_params(("parallel",)),
        name="log_forget_cumsum",
    )(h, w_ff_t, bf_col)


def _logf_sample(xs, eg, eb, w_ff_t, bf_col):
    ms = xs.shape[0]
    fixed = lambda b: (0, 0)
    return pl.pallas_call(
        _logf_sample_kernel,
        grid=(1,),
        in_specs=[pl.BlockSpec((ms, D_MODEL), fixed), pl.BlockSpec((1, D_MODEL), fixed),
                  pl.BlockSpec((1, D_MODEL), fixed), pl.BlockSpec((H_F, D_MODEL), fixed),
                  pl.BlockSpec((H_F, 1), fixed)],
        out_specs=pl.BlockSpec((1, H_F, ms), lambda b: (0, 0, 0)),
        out_shape=jax.ShapeDtypeStruct((1, H_F, ms), F32),
        compiler_params=_params(("arbitrary",)),
        name="log_forget",
    )(xs, eg, eb, w_ff_t, bf_col)


def _group_norm_gate(o, gate_bf16, gn_row):
    mu = jnp.mean(o, axis=-1, keepdims=True)
    oc = o - mu
    var = jnp.mean(oc * oc, axis=-1, keepdims=True)
    gate = gate_bf16.astype(F32)
    return oc * lax.rsqrt(var + GN_EPS) * gn_row * (gate * _sigmoid(gate))


def _ret_prompt_kernel(lg_ref, q_ref, k_ref, v_ref, g_ref, gn_ref, o_ref, st_ref):
    c = pl.program_id(1)

    @pl.when(c == 0)
    def _():
        st_ref[...] = jnp.zeros_like(st_ref)

    n = q_ref.shape[0]
    ri = lax.broadcasted_iota(jnp.int32, (n, n), 0)
    ci = lax.broadcasted_iota(jnp.int32, (n, n), 1)
    causal = ri >= ci
    diff = jnp.where(causal, (ri - ci).astype(F32), 0.0)
    pos = lax.broadcasted_iota(jnp.int32, (n, DK_R), 0).astype(F32)
    for hd in range(H_R):
        lg = lg_ref[hd]
        sl = slice(hd * DK_R, (hd + 1) * DK_R)
        q = q_ref[:, sl]
        k = k_ref[:, sl]
        v = v_ref[:, sl]
        dec = jnp.where(causal, jnp.exp(lg * diff), 0.0)
        sc = lax.dot_general(q, k, NT_DIMS, preferred_element_type=F32) * dec
        o = jnp.dot(sc.astype(BF16), v, preferred_element_type=F32)
        state = st_ref[0, hd]
        cross = jnp.exp(lg * (pos + 1.0))
        o = o + jnp.dot(q, state.astype(BF16), preferred_element_type=F32) * cross
        kdec = jnp.exp(lg * (n - 1.0 - pos))
        kd = (k.astype(F32) * kdec).astype(BF16)
        carry = jnp.exp(jnp.full((1, DV_R), lg * n, F32))
        st_ref[0, hd] = carry * state + lax.dot_general(kd, v, TN_DIMS, preferred_element_type=F32)
        o_ref[:, sl] = _group_norm_gate(o, g_ref[:, sl], gn_ref[hd:hd + 1, :]).astype(BF16)


def _ret_prompt(z, log_g, gn_g, batch, seq):
    nc = seq // CHUNK
    row = lambda b, c: b * nc + c
    zspec = lambda cb: pl.BlockSpec((CHUNK, COL_BLOCK), lambda b, c: (row(b, c), cb))
    return pl.pallas_call(
        _ret_prompt_kernel,
        grid=(batch, nc),
        in_specs=[pl.BlockSpec(memory_space=pltpu.SMEM),
                  zspec(CB_RQ), zspec(CB_RK), zspec(CB_RV), zspec(CB_RG),
                  pl.BlockSpec((H_R, DV_R), lambda b, c: (0, 0))],
        out_specs=[pl.BlockSpec((CHUNK, W_R), lambda b, c: (row(b, c), 0)),
                   pl.BlockSpec((1, H_R, DK_R, DV_R), lambda b, c: (b, 0, 0, 0))],
        out_shape=[jax.ShapeDtypeStruct((batch * seq, W_R), BF16),
                   jax.ShapeDtypeStruct((batch, H_R, DK_R, DV_R), F32)],
        compiler_params=_params(("parallel", "arbitrary")),
        name="retention_prompt",
    )(log_g, z, z, z, z, gn_g)


def _ret_sample_kernel(lg_ref, q_ref, k_ref, v_ref, g_ref, gn_ref, st_ref, o_ref, ns_ref):
    row0 = lax.broadcasted_iota(jnp.int32, (8, DK_R), 0) == 0
    for hd in range(H_R):
        lg = lg_ref[hd]
        sl = slice(hd * DK_R, (hd + 1) * DK_R)
        q = q_ref[0, :, sl]
        k = k_ref[0, :, sl]
        v = v_ref[0, :, sl]
        qf = q.astype(F32)
        kf = k.astype(F32)
        vf = v.astype(F32)
        state = st_ref[0, hd]
        decay = jnp.exp(jnp.full((1, DV_R), lg, F32))
        score = jnp.sum(qf * kf, axis=-1, keepdims=True)
        q8 = jnp.broadcast_to(q, (8, DK_R))
        qs = jnp.dot(q8, state.astype(BF16), preferred_element_type=F32)[0:1, :]
        o = score * vf + qs * decay
        k8 = jnp.where(row0, jnp.broadcast_to(kf, (8, DK_R)), 0.0).astype(BF16)
        v8 = jnp.broadcast_to(v, (8, DV_R))
        ns_ref[0, hd] = decay * state + lax.dot_general(k8, v8, TN_DIMS, preferred_element_type=F32)
        o_ref[0, :, sl] = _group_norm_gate(o, g_ref[0, :, sl], gn_ref[hd:hd + 1, :]).astype(BF16)


def _ret_sample(z3, log_g, gn_g, state):
    nb = z3.shape[0]
    zspec = lambda cb: pl.BlockSpec((1, 1, COL_BLOCK), lambda b: (b, 0, cb))
    st_spec = pl.BlockSpec((1, H_R, DK_R, DV_R), lambda b: (b, 0, 0, 0))
    return pl.pallas_call(
        _ret_sample_kernel,
        grid=(nb,),
        in_specs=[pl.BlockSpec(memory_space=pltpu.SMEM),
                  zspec(CB_RQ), zspec(CB_RK), zspec(CB_RV), zspec(CB_RG),
                  pl.BlockSpec((H_R, DV_R), lambda b: (0, 0)),
                  st_spec],
        out_specs=[pl.BlockSpec((1, 1, W_R), lambda b: (b, 0, 0)), st_spec],
        out_shape=[jax.ShapeDtypeStruct((nb, 1, W_R), BF16),
                   jax.ShapeDtypeStruct(state.shape, F32)],
        compiler_params=_params(("parallel",)),
        name="retention_sample",
    )(log_g, z3, z3, z3, z3, gn_g, state)


def _fox_prompt_kernel(q_ref, k_ref, v_ref, crow_ref, ccol_ref, o_ref, vt_ref, m_ref, l_ref, acc_ref, *, tq):
    qi = pl.program_id(1)
    nk = vt_ref.shape[1]

    @pl.when(qi == 0)
    def _():
        for hd in range(H_F):
            for kj in range(nk):
                v_tile = v_ref[kj * tq:(kj + 1) * tq, hd * D_F:(hd + 1) * D_F]
                vt_ref[hd, kj] = v_tile.astype(F32).T.astype(BF16)

    m_ref[...] = jnp.full(m_ref.shape, -jnp.inf, F32)
    l_ref[...] = jnp.zeros_like(l_ref)
    acc_ref[...] = jnp.zeros_like(acc_ref)
    key_i = lax.broadcasted_iota(jnp.int32, (tq, tq), 0)
    qry_i = lax.broadcasted_iota(jnp.int32, (tq, tq), 1)
    visible = key_i <= qry_i

    heads = lambda ref, rows: jnp.stack([ref[rows, hd * D_F:(hd + 1) * D_F] for hd in range(H_F)], axis=0)
    q3 = heads(q_ref, slice(None))
    c_q = crow_ref[0, :, pl.ds(qi, 1), :]

    def tile(kj, on_diagonal):
        ks = pl.multiple_of(kj * tq, tq)
        k3 = heads(k_ref, pl.ds(ks, tq))
        s = lax.dot_general(k3, q3, (((2,), (2,)), ((0,), (0,))), preferred_element_type=F32)
        c_k = jnp.stack([ccol_ref[0, pl.ds(ks, tq), hd:hd + 1] for hd in range(H_F)], axis=0)
        t = s * (FOX_SCALE * LOG2E) + (c_q - c_k)
        if on_diagonal:
            t = jnp.where(visible[None], t, -jnp.inf)
        m_old = m_ref[...]
        m_new = jnp.maximum(m_old, jnp.max(t, axis=1, keepdims=True))
        alpha = jnp.exp2(m_old - m_new)
        p = jnp.exp2(t - m_new)
        l_ref[...] = alpha * l_ref[...] + jnp.sum(p, axis=1, keepdims=True)
        m_ref[...] = m_new
        pv = lax.dot_general(vt_ref[:, kj], p.astype(BF16), (((2,), (1,)), ((0,), (0,))),
                             preferred_element_type=F32)
        acc_ref[...] = alpha * acc_ref[...] + pv

    def body(kj, carry):
        tile(kj, False)
        return carry

    lax.fori_loop(0, qi, body, 0)
    tile(qi, True)
    for hd in range(H_F):
        o_ref[:, hd * D_F:(hd + 1) * D_F] = (acc_ref[hd] / l_ref[hd]).T.astype(BF16)


def _fox_prompt(z, c2_row, c2_col, batch, seq, tq=256):
    nq = seq // tq
    return pl.pallas_call(
        functools.partial(_fox_prompt_kernel, tq=tq),
        grid=(batch, nq),
        in_specs=[pl.BlockSpec((tq, COL_BLOCK), lambda b, i: (b * nq + i, CB_FQ)),
                  pl.BlockSpec((seq, COL_BLOCK), lambda b, i: (b, CB_FK)),
                  pl.BlockSpec((seq, COL_BLOCK), lambda b, i: (b, CB_FV)),
                  pl.BlockSpec((1, H_F, nq, tq), lambda b, i: (b, 0, 0, 0)),
                  pl.BlockSpec((1, seq, H_F), lambda b, i: (b, 0, 0))],
        out_specs=pl.BlockSpec((tq, W_F), lambda b, i: (b * nq + i, 0)),
        out_shape=jax.ShapeDtypeStruct((batch * seq, W_F), BF16),
        scratch_shapes=[pltpu.VMEM((H_F, nq, D_F, tq), BF16),
                        pltpu.VMEM((H_F, 1, tq), F32),
                        pltpu.VMEM((H_F, 1, tq), F32),
                        pltpu.VMEM((H_F, D_F, tq), F32)],
        compiler_params=_params(("parallel", "arbitrary")),
        name="fox_prompt",
    )(z, z, z, c2_row.reshape(batch, H_F, nq, tq), c2_col)


def _decode_bias_kernel(pt_ref, pool_ref, lnew_ref, bias_ref, rows_ref, *, n_pages):
    b = pl.program_id(0)
    width = PAGE_SIZE * H_F
    for p in range(n_pages):
        rows_ref[p:p + 1, :] = pool_ref[pl.ds(pt_ref[b, p], 1), :]
    x = rows_ref[...]
    lane = lax.broadcasted_iota(jnp.int32, x.shape, 1)
    within = x
    total = x
    s = H_F
    while s < width:
        within = within + jnp.where(lane >= s, pltpu.roll(within, s, axis=1), 0.0)
        total = total + pltpu.roll(total, s, axis=1)
        s *= 2
    rest = total - within
    run = lnew_ref[0]
    for p in reversed(range(n_pages)):
        bias_ref[0, p:p + 1, :] = run + rest[p:p + 1, :]
        run = run + total[p:p + 1, :]


def _decode_bias(page_table, logf_pool, lnew):
    nb, n_pages = page_table.shape
    width = PAGE_SIZE * H_F
    return pl.pallas_call(
        functools.partial(_decode_bias_kernel, n_pages=n_pages),
        grid_spec=pltpu.PrefetchScalarGridSpec(
            num_scalar_prefetch=1,
            grid=(nb,),
            in_specs=[pl.BlockSpec(logf_pool.shape, lambda b, pt: (0, 0), pipeline_mode=pl.Buffered(1)),
                      pl.BlockSpec((1, 1, width), lambda b, pt: (b, 0, 0))],
            out_specs=pl.BlockSpec((1, n_pages, width), lambda b, pt: (b, 0, 0)),
            scratch_shapes=[pltpu.VMEM((n_pages, width), F32)]),
        out_shape=jax.ShapeDtypeStruct((nb, n_pages, width), F32),
        compiler_params=_params(("arbitrary",)),
        name="decode_forget_bias",
    )(page_table, logf_pool, lnew)


class _DecodeRefs:
    def __init__(self, q_ref, kn_ref, vn_ref, bias_ref, k_refs, v_refs, o_ref, m_ref, l_ref, acc_ref):
        self.q, self.kn, self.vn, self.bias = q_ref, kn_ref, vn_ref, bias_ref
        self.k, self.v = k_refs, v_refs
        self.o, self.m, self.l, self.acc = o_ref, m_ref, l_ref, acc_ref


def _decode_begin(d):
    q = d.q[0].astype(F32)
    s_new = jnp.sum(q * d.kn[0].astype(F32), axis=-1, keepdims=True) * FOX_SCALE
    d.m[...] = jnp.broadcast_to(s_new, d.m.shape)
    d.l[...] = jnp.ones_like(d.l)
    d.acc[...] = d.vn[0].astype(F32)


def _decode_scores(d):
    width = PAGE_SIZE * H_F
    row = lax.broadcasted_iota(jnp.int32, (H_F, width), 0)
    lane = lax.broadcasted_iota(jnp.int32, (H_F, width), 1)
    own = (lane % H_F) == row
    q = d.q[0]
    s_parts = []
    for i, k_ref in enumerate(d.k):
        kt = k_ref[...].astype(BF16)
        s = lax.dot_general(q, kt, NT_DIMS, preferred_element_type=F32)
        s_parts.append(jnp.where(own, s * FOX_SCALE + d.bias[0, i:i + 1, :], -jnp.inf))
    return s_parts


def _decode_update(d, s_parts):
    m_old = d.m[:, 0:1]
    m_new = m_old
    for s in s_parts:
        m_new = jnp.maximum(m_new, jnp.max(s, axis=-1, keepdims=True))
    alpha = jnp.exp(m_old - m_new)
    l_new = alpha * d.l[:, 0:1]
    pv = jnp.zeros((H_F, D_F), F32)
    for s, v_ref in zip(s_parts, d.v):
        p = jnp.exp(s - m_new)
        l_new = l_new + jnp.sum(p, axis=-1, keepdims=True)
        pv = pv + jnp.dot(p.astype(BF16), v_ref[...].astype(BF16), preferred_element_type=F32)
    d.m[...] = jnp.broadcast_to(m_new, d.m.shape)
    d.l[...] = jnp.broadcast_to(l_new, d.l.shape)
    d.acc[...] = alpha * d.acc[...] + pv


def _decode_end(d):
    d.o[0] = (d.acc[...] / d.l[:, 0:1]).astype(BF16)


def _fox_decode_kernel(pt_ref, q_ref, kn_ref, vn_ref, bias_ref, *refs, pages):
    d = _DecodeRefs(q_ref, kn_ref, vn_ref, bias_ref, refs[:pages], refs[pages:2 * pages], *refs[2 * pages:])
    g = pl.program_id(1)
    pl.when(g == 0)(lambda: _decode_begin(d))
    _decode_update(d, _decode_scores(d))
    pl.when(g == pl.num_programs(1) - 1)(lambda: _decode_end(d))


def _decode_specs(pages, seq_group):
    width = PAGE_SIZE * H_F

    def head_map(*args):
        return (seq_group(*args[:-1])[0], 0, 0)

    def bias_map(*args):
        b, g = seq_group(*args[:-1])
        return (b, g, 0)

    def page_spec(i):
        def page_map(*args):
            b, g = seq_group(*args[:-1])
            return (args[-1][b, g * pages + i], 0, 0)
        return pl.BlockSpec((None, width, D_F), page_map)

    head_spec = pl.BlockSpec((1, H_F, D_F), head_map)
    in_specs = ([head_spec, head_spec, head_spec, pl.BlockSpec((1, pages, width), bias_map)]
                + [page_spec(i) for i in range(pages)] * 2)
    scratch = [pltpu.VMEM((H_F, D_F), F32), pltpu.VMEM((H_F, D_F), F32), pltpu.VMEM((H_F, D_F), F32)]
    return in_specs, head_spec, scratch


def _fox_decode(page_table, q, k_new, v_new, bias, k_pool, v_pool, pages=DECODE_PAGES_PER_STEP):
    nb, n_pages = page_table.shape
    in_specs, out_spec, scratch = _decode_specs(pages, lambda b, g: (b, g))
    return pl.pallas_call(
        functools.partial(_fox_decode_kernel, pages=pages),
        grid_spec=pltpu.PrefetchScalarGridSpec(
            num_scalar_prefetch=1,
            grid=(nb, n_pages // pages),
            in_specs=in_specs,
            out_specs=out_spec,
            scratch_shapes=scratch),
        out_shape=jax.ShapeDtypeStruct((nb, H_F, D_F), BF16),
        compiler_params=_params(("parallel", "arbitrary")),
        name="fox_decode",
    )(page_table, q, k_new, v_new, bias, *([k_pool] * pages), *([v_pool] * pages))


def _merge_kernel(or_ref, of_ref, gr0_ref, gr1_ref, gf0_ref, gf1_ref, x_ref, eg_ref, eb_ref,
                  wr_ref, wf_ref, wo_ref, g1_ref, b1_ref, x1_ref, x1b_ref):
    tm = x_ref.shape[0]
    sub = min(tm, ROW_SUBTILE)
    for r in range(tm // sub):
        rows = slice(r * sub, (r + 1) * sub)
        y_r = jnp.dot(or_ref[rows, :], wr_ref[...], preferred_element_type=F32)
        y_f = jnp.dot(of_ref[rows, :], wf_ref[...], preferred_element_type=F32)
        halves = []
        for idx, (gr_ref, gf_ref) in enumerate(((gr0_ref, gf0_ref), (gr1_ref, gf1_ref))):
            sl = slice(idx * COL_BLOCK, (idx + 1) * COL_BLOCK)
            mixed = (_sigmoid(gr_ref[rows, :].astype(F32)) * y_r[:, sl]
                     + _sigmoid(gf_ref[rows, :].astype(F32)) * y_f[:, sl])
            halves.append(mixed.astype(BF16))
        mix = jnp.dot(jnp.concatenate(halves, axis=1), wo_ref[...], preferred_element_type=F32)
        h = _layer_norm(x_ref[rows, :], eg_ref[...], eb_ref[...])
        x1 = _layer_norm(ALPHA * h + mix, g1_ref[...], b1_ref[...])
        x1_ref[rows, :] = x1
        x1b_ref[rows, :] = x1.astype(BF16)


def _merge(o_r, o_f, z, x, eg, eb, w_r, w_f, w_o, g1, b1, tm):
    m = x.shape[0]
    row = lambda i: (i, 0)
    vec = pl.BlockSpec((1, D_MODEL), lambda i: (0, 0))
    zspec = lambda cb: pl.BlockSpec((tm, COL_BLOCK), lambda i: (i, cb))
    return pl.pallas_call(
        _merge_kernel,
        grid=(m // tm,),
        in_specs=[pl.BlockSpec((tm, W_R), row), pl.BlockSpec((tm, W_F), row),
                  zspec(CB_GR), zspec(CB_GR + 1), zspec(CB_GF), zspec(CB_GF + 1),
                  pl.BlockSpec((tm, D_MODEL), row), vec, vec,
                  _resident((W_R, D_MODEL)), _resident((W_F, D_MODEL)), _resident((D_MODEL, D_MODEL)),
                  vec, vec],
        out_specs=[pl.BlockSpec((tm, D_MODEL), row), pl.BlockSpec((tm, D_MODEL), row)],
        out_shape=[jax.ShapeDtypeStruct((m, D_MODEL), F32), jax.ShapeDtypeStruct((m, D_MODEL), BF16)],
        compiler_params=_params(("parallel",)),
        name="merge_out_ln1",
    )(o_r, o_f, z, z, z, z, x, eg, eb, w_r, w_f, w_o, g1, b1)


def _ffn_up(x_ref, w1_ref, w3_ref):
    xb = x_ref[...]
    a = jnp.dot(xb, w1_ref[...].astype(BF16), preferred_element_type=F32)
    b = jnp.dot(xb, w3_ref[...].astype(BF16), preferred_element_type=F32)
    return (a * _sigmoid(a) * b).astype(BF16)


def _ffn_down(mid, w2_ref, o_ref):
    o_ref[...] += jnp.dot(mid, w2_ref[...].astype(BF16), preferred_element_type=F32)


def _ffn_tile(x_ref, w1_ref, w3_ref, w2_ref, o_ref):
    _ffn_down(_ffn_up(x_ref, w1_ref, w3_ref), w2_ref, o_ref)


def _ffn_kernel(x_ref, w1_ref, w3_ref, w2_ref, o_ref):
    @pl.when(pl.program_id(1) == 0)
    def _():
        o_ref[...] = jnp.zeros_like(o_ref)

    _ffn_tile(x_ref, w1_ref, w3_ref, w2_ref, o_ref)


def _ffn_decode_kernel(pt_ref, x_ref, w1_ref, w3_ref, w2_ref, q_ref, kn_ref, vn_ref, bias_ref, *refs,
                       pages, n_groups):
    o_ref = refs[2 * pages]
    d = _DecodeRefs(q_ref, kn_ref, vn_ref, bias_ref, refs[:pages], refs[pages:2 * pages], *refs[2 * pages + 1:])
    j = pl.program_id(1)
    g = (pl.program_id(0) * pl.num_programs(1) + j) % n_groups

    @pl.when(j == 0)
    def _():
        o_ref[...] = jnp.zeros_like(o_ref)

    pl.when(g == 0)(lambda: _decode_begin(d))
    mid = _ffn_up(x_ref, w1_ref, w3_ref)
    s_parts = _decode_scores(d)
    _ffn_down(mid, w2_ref, o_ref)
    _decode_update(d, s_parts)
    pl.when(g == n_groups - 1)(lambda: _decode_end(d))


def _ffn(x1b, w1, w3, w2, tm, tf):
    m = x1b.shape[0]
    return pl.pallas_call(
        _ffn_kernel,
        grid=(m // tm, D_FF // tf),
        in_specs=[pl.BlockSpec((tm, D_MODEL), lambda i, j: (i, 0)),
                  pl.BlockSpec((D_MODEL, tf), lambda i, j: (0, j)),
                  pl.BlockSpec((D_MODEL, tf), lambda i, j: (0, j)),
                  pl.BlockSpec((tf, D_MODEL), lambda i, j: (j, 0))],
        out_specs=pl.BlockSpec((tm, D_MODEL), lambda i, j: (i, 0)),
        out_shape=jax.ShapeDtypeStruct((m, D_MODEL), F32),
        compiler_params=_params(("parallel", "arbitrary")),
        name="swiglu_ffn_rows",
    )(x1b, w1, w3, w2)


def _ffn_with_decode(x1b, w1, w3, w2, tm, tf, page_table, q, k_new, v_new, bias, k_pool, v_pool,
                     pages=DECODE_PAGES_PER_STEP):
    m = x1b.shape[0]
    nj = D_FF // tf
    n_groups = page_table.shape[1] // pages
    n_seq = (m // tm) * nj // n_groups
    assert n_seq * n_groups == (m // tm) * nj and n_seq <= page_table.shape[0]

    def seq_group(i, j):
        s = i * nj + j
        return s // n_groups, s % n_groups

    dec_in, dec_out, dec_scratch = _decode_specs(pages, seq_group)
    return pl.pallas_call(
        functools.partial(_ffn_decode_kernel, pages=pages, n_groups=n_groups),
        grid_spec=pltpu.PrefetchScalarGridSpec(
            num_scalar_prefetch=1,
            grid=(m // tm, nj),
            in_specs=[pl.BlockSpec((tm, D_MODEL), lambda i, j, pt: (i, 0), pipeline_mode=pl.Buffered(1)),
                      pl.BlockSpec((D_MODEL, tf), lambda i, j, pt: (0, j)),
                      pl.BlockSpec((D_MODEL, tf), lambda i, j, pt: (0, j)),
                      pl.BlockSpec((tf, D_MODEL), lambda i, j, pt: (j, 0))] + dec_in,
            out_specs=[pl.BlockSpec((tm, D_MODEL), lambda i, j, pt: (i, 0)), dec_out],
            scratch_shapes=dec_scratch),
        out_shape=[jax.ShapeDtypeStruct((m, D_MODEL), F32),
                   jax.ShapeDtypeStruct((n_seq, H_F, D_F), BF16)],
        compiler_params=_params(("arbitrary", "arbitrary")),
        name="swiglu_ffn_decode",
    )(page_table, x1b, w1, w3, w2, q, k_new, v_new, bias, *([k_pool] * pages), *([v_pool] * pages))


def _ple_kernel(x1_ref, x1b_ref, ffn_ref, p_ref, wpg_ref, wpe_ref, g2_ref, b2_ref, y_ref):
    tm = x1_ref.shape[0]
    sub = min(tm, ROW_SUBTILE)
    for r in range(tm // sub):
        rows = slice(r * sub, (r + 1) * sub)
        pg = jnp.dot(x1b_ref[rows, :], wpg_ref[...], preferred_element_type=F32)
        pe = jnp.dot(p_ref[rows, :].astype(BF16), wpe_ref[...], preferred_element_type=F32)
        y_ref[rows, :] = _layer_norm(ALPHA * x1_ref[rows, :] + ffn_ref[rows, :] + _sigmoid(pg) * pe,
                                     g2_ref[...], b2_ref[...])


def _ple(x1, x1b, ffn, p, w_pg, w_pe, g2, b2, tm):
    m = x1.shape[0]
    row = lambda i: (i, 0)
    vec = pl.BlockSpec((1, D_MODEL), lambda i: (0, 0))
    return pl.pallas_call(
        _ple_kernel,
        grid=(m // tm,),
        in_specs=[pl.BlockSpec((tm, D_MODEL), row), pl.BlockSpec((tm, D_MODEL), row),
                  pl.BlockSpec((tm, D_MODEL), row), pl.BlockSpec((tm, D_PLE), row),
                  _resident((D_MODEL, D_MODEL)), _resident((D_PLE, D_MODEL)), vec, vec],
        out_specs=pl.BlockSpec((tm, D_MODEL), row),
        out_shape=jax.ShapeDtypeStruct((m, D_MODEL), F32),
        compiler_params=_params(("parallel",)),
        name="ple_ln2",
    )(x1, x1b, ffn, p, w_pg, w_pe, g2, b2)


def _rope_tables(pos):
    half = DK_R // 2
    inv = ROPE_BASE ** (-jnp.arange(half, dtype=F32) / half)
    ang = pos.astype(F32)[:, None] * inv[None, :]
    return jnp.cos(ang), jnp.sin(ang)


def kernel(x_prompt, x_sample, state_ret, cache_k, cache_v, cache_logf, page_table, p_prompt, p_sample, ln_emb_g, ln_emb_b, w_in, b_f, gn_g, w_ret_proj, w_fox_proj, w_out, ln1_g, ln1_b, w1, w3, w2, w_pg, w_pe, ln2_g, ln2_b):
    assert w_in.shape[0] == DEPTH == 1
    batch, seq, _ = x_prompt.shape
    dec_b, dec_t, _ = x_sample.shape
    assert dec_t == 1
    n_pages = page_table.shape[1]
    past_len = n_pages * PAGE_SIZE
    n_pool = cache_k.shape[1]

    w_in_t = jnp.swapaxes(w_in[0], 0, 1)
    w_ff_t = w_in_t[FF_LO:FF_HI].astype(BF16)
    bf_col = b_f[0].reshape(H_F, 1)
    vec = lambda v: v.reshape(1, D_MODEL)
    eg, eb = vec(ln_emb_g), vec(ln_emb_b)
    g1, b1, g2, b2 = vec(ln1_g[0]), vec(ln1_b[0]), vec(ln2_g[0]), vec(ln2_b[0])
    w_r, w_f, w_o = w_ret_proj[0].astype(BF16), w_fox_proj[0].astype(BF16), w_out[0].astype(BF16)
    w_pgb, w_peb = w_pg[0].astype(BF16), w_pe[0].astype(BF16)
    log_g = jnp.log1p(-jnp.exp2(-5.0 - jnp.arange(H_R, dtype=F32)))
    cos_p, sin_p = _rope_tables(jnp.arange(seq))
    cos_s, sin_s = _rope_tables(jnp.full((dec_b,), past_len))

    xp = x_prompt.reshape(batch * seq, D_MODEL)
    xs = x_sample.reshape(dec_b, D_MODEL)

    h_p = _entry_ln(xp, eg, eb, tm=512)
    z_p, fk_p, fv_p, z_s, fk_s, fv_s = _inproj(h_p, xs, eg, eb, w_in_t, cos_p, sin_p, cos_s, sin_s, tm=IN_TM)
    lf_p, c2_row = _logf_prompt(h_p, w_ff_t, bf_col, seq)
    lf_s = _logf_sample(xs, eg, eb, w_ff_t, bf_col)

    o_r_p, st_p = _ret_prompt(z_p, log_g, gn_g[0], batch, seq)
    o_f_p = _fox_prompt(z_p, c2_row, jnp.swapaxes(c2_row, 1, 2), batch, seq)

    z3 = z_s.reshape(dec_b, 1, z_s.shape[1])
    o_r_s, st_s = _ret_sample(z3, log_g, gn_g[0], state_ret[0])
    width = PAGE_SIZE * H_F
    lf_new = jnp.swapaxes(lf_s, 1, 2).reshape(dec_b, H_F)
    lnew = jnp.tile(lf_new, (1, PAGE_SIZE)).reshape(dec_b, 1, width)
    bias = _decode_bias(page_table, cache_logf[0].reshape(n_pool, width), lnew)
    heads = lambda cb: z_s[:, cb * COL_BLOCK:(cb + 1) * COL_BLOCK].reshape(dec_b, H_F, D_F)
    decode_args = (heads(CB_FQ), heads(CB_FK), heads(CB_FV), bias,
                   cache_k[0].reshape(n_pool, width, D_F), cache_v[0].reshape(n_pool, width, D_F))

    x1_p, x1b_p = _merge(o_r_p, o_f_p, z_p, xp, eg, eb, w_r, w_f, w_o, g1, b1, tm=MERGE_TM)
    ffn_p, o_f_ride = _ffn_with_decode(x1b_p, w1[0], w3[0], w2[0], FFN_TM, FFN_TF, page_table, *decode_args)
    n_ride = o_f_ride.shape[0]
    o_f_rest = _fox_decode(page_table[n_ride:], *(a[n_ride:] for a in decode_args[:4]), *decode_args[4:])
    o_f_s = jnp.concatenate([o_f_ride, o_f_rest], axis=0)
    x1_s, x1b_s = _merge(o_r_s.reshape(dec_b, W_R), o_f_s.reshape(dec_b, W_F), z_s, xs, eg, eb,
                         w_r, w_f, w_o, g1, b1, tm=dec_b)
    ffn_s = _ffn(x1b_s, w1[0], w3[0], w2[0], tm=dec_b, tf=FFN_TF)
    y_p = _ple(x1_p, x1b_p, ffn_p, p_prompt[0].reshape(batch * seq, D_PLE), w_pgb, w_peb, g2, b2, tm=PLE_TM)
    y_s = _ple(x1_s, x1b_s, ffn_s, p_sample[0].reshape(dec_b, D_PLE), w_pgb, w_peb, g2, b2, tm=dec_b)

    return (y_p.reshape(batch, seq, D_MODEL),
            y_s.reshape(dec_b, 1, D_MODEL),
            st_p[None],
            fk_p.reshape(1, batch, seq, H_F, D_F),
            fv_p.reshape(1, batch, seq, H_F, D_F),
            jnp.swapaxes(lf_p, 1, 2)[None],
            st_s[None],
            fk_s.reshape(1, dec_b, 1, H_F, D_F),
            fv_s.reshape(1, dec_b, 1, H_F, D_F),
            jnp.swapaxes(lf_s, 1, 2).reshape(1, dec_b, 1, H_F))
```

```python
import functools

import jax
import jax.numpy as jnp
import numpy as np
from jax import lax
from jax.experimental import pallas as pl
from jax.experimental.pallas import tpu as pltpu

F32 = jnp.float32
BF16 = jnp.bfloat16

D_MODEL = 2048
DEPTH = 1
PAGE_SIZE = 128
H_R = 4
DK_R = 256
DV_R = 256
W_R = H_R * DV_R
CHUNK = 128
ROPE_BASE = 10000.0
GN_EPS = 1e-6
H_F = 8
D_F = 128
W_F = H_F * D_F
D_FF = 5632
D_PLE = 256
LN_EPS = 1e-5
ALPHA = (2.0 * DEPTH) ** 0.25
FOX_SCALE = D_F ** -0.5
LOG2E = float(np.log2(np.e))

COL_BLOCK = 1024
N_MAIN = 11 * COL_BLOCK
CB_RQ, CB_RK, CB_RV, CB_RG, CB_FQ, CB_FK, CB_FV, CB_GR, CB_GF = 0, 1, 2, 3, 4, 5, 6, 7, 9
FF_LO, FF_HI = 7 * COL_BLOCK, 7 * COL_BLOCK + H_F

IN_TM, IN_TN = 1024, 1024
W_FF_PAD = 128
N_A_BLOCKS = FF_LO // IN_TN
FFN_TM, FFN_TF = 1024, 256
MERGE_TM, PLE_TM, ROW_SUBTILE = 256, 512, 256

VMEM_LIMIT = 56 * 1024 * 1024
DECODE_PAGES_PER_STEP = 8
RET_SAMPLE_SEQS = 4

NT_DIMS = (((1,), (1,)), ((), ()))
TN_DIMS = (((0,), (0,)), ((), ()))


def _params(semantics):
    return pltpu.CompilerParams(dimension_semantics=semantics, vmem_limit_bytes=VMEM_LIMIT)


def _layer_norm(x, g, b):
    mu = jnp.mean(x, axis=-1, keepdims=True)
    xc = x - mu
    var = jnp.mean(xc * xc, axis=-1, keepdims=True)
    return xc * lax.rsqrt(var + LN_EPS) * g + b


def _sigmoid(x):
    return 1.0 / (1.0 + jnp.exp(-x))


def _log_sigmoid(x):
    return jnp.minimum(x, 0.0) - jnp.log(1.0 + jnp.exp(-jnp.abs(x)))


def _resident(shape):
    return pl.BlockSpec(shape, lambda *_: (0,) * len(shape), pipeline_mode=pl.Buffered(1))


def _ln_kernel(x_ref, g_ref, b_ref, o_ref):
    o_ref[...] = _layer_norm(x_ref[...], g_ref[...], b_ref[...]).astype(o_ref.dtype)


def _entry_ln(x, g, b, tm):
    m = x.shape[0]
    return pl.pallas_call(
        _ln_kernel,
        grid=(m // tm,),
        in_specs=[pl.BlockSpec((tm, D_MODEL), lambda i: (i, 0)),
                  pl.BlockSpec((1, D_MODEL), lambda i: (0, 0)),
                  pl.BlockSpec((1, D_MODEL), lambda i: (0, 0))],
        out_specs=pl.BlockSpec((tm, D_MODEL), lambda i: (i, 0)),
        out_shape=jax.ShapeDtypeStruct((m, D_MODEL), BF16),
        compiler_params=_params(("parallel",)),
        name="entry_ln",
    )(x, g, b)


def _inproj_block(h, w_ref, z_ref, rope=None, f32_ref=None):
    half = DK_R // 2
    for hd in range(IN_TN // DK_R):
        lo = hd * DK_R
        acc = lax.dot_general(h, w_ref[lo:lo + DK_R, :].astype(BF16), NT_DIMS, preferred_element_type=F32)
        if rope is None:
            z_ref[:, lo:lo + DK_R] = acc.astype(BF16)
        else:
            cos, sin, scale = rope
            x1 = acc[:, :half]
            x2 = acc[:, half:]
            z_ref[:, lo:lo + half] = ((x1 * cos - x2 * sin) * scale).astype(BF16)
            z_ref[:, lo + half:lo + DK_R] = ((x2 * cos + x1 * sin) * scale).astype(BF16)
        if f32_ref is not None:
            f32_ref[:, lo:lo + DK_R] = acc


def _inproj_kernel(h_ref, xs_ref, eg_ref, eb_ref, w_ref, wff_ref, cos_ref, sin_ref, coss_ref, sins_ref,
                   z_ref, fk_ref, fv_ref, ff_ref, zs_ref, fks_ref, fvs_ref):
    i = pl.program_id(0)
    j = pl.program_id(1)
    per = COL_BLOCK // IN_TN

    @pl.when(j == 0)
    def _():
        ff_ref[...] = lax.dot_general(h_ref[...], wff_ref[...], NT_DIMS, preferred_element_type=F32)

    def section(lo, hi, rope_scale=None, f32_refs=(None, None)):
        @pl.when((j >= lo) & (j < hi))
        def _():
            rope = None if rope_scale is None else (cos_ref[...], sin_ref[...], rope_scale)
            _inproj_block(h_ref[...], w_ref, z_ref, rope, f32_refs[0])

            @pl.when(i == 0)
            def _():
                hs = _layer_norm(xs_ref[...], eg_ref[...], eb_ref[...]).astype(BF16)
                rope_s = None if rope_scale is None else (coss_ref[...], sins_ref[...], rope_scale)
                _inproj_block(hs, w_ref, zs_ref, rope_s, f32_refs[1])

    @pl.when(i > 0)
    def _():
        zs_ref[...] = jnp.zeros_like(zs_ref)

    section(CB_RQ * per, CB_RK * per, rope_scale=1.0)
    section(CB_RK * per, CB_RV * per, rope_scale=DK_R ** -0.5)
    section(CB_RV * per, CB_FK * per)
    section(CB_FK * per, CB_FV * per, f32_refs=(fk_ref, fks_ref))
    section(CB_FV * per, N_A_BLOCKS, f32_refs=(fv_ref, fvs_ref))
    section(N_A_BLOCKS, N_MAIN // IN_TN)


def _inproj(h, xs, eg, eb, w_in_t, w_ff_pad, cos, sin, cos_s, sin_s, tm):
    m = h.shape[0]
    ms = xs.shape[0]
    n_pos_blocks = cos.shape[0] // tm
    nj = N_MAIN // IN_TN
    per = COL_BLOCK // IN_TN
    half = DK_R // 2
    clamp = lambda j, cb: jnp.clip(j - cb * per, 0, per - 1)
    row = lambda i, j: (i, 0)
    fixed = lambda i, j: (0, 0)
    return pl.pallas_call(
        _inproj_kernel,
        grid=(m // tm, nj),
        in_specs=[pl.BlockSpec((tm, D_MODEL), row),
                  pl.BlockSpec((ms, D_MODEL), fixed),
                  pl.BlockSpec((1, D_MODEL), fixed),
                  pl.BlockSpec((1, D_MODEL), fixed),
                  pl.BlockSpec((pl.Element(IN_TN), pl.Element(D_MODEL)),
                               lambda i, j: (pl.multiple_of(IN_TN * j + jnp.where(j < N_A_BLOCKS, 0, H_F), 8), 0)),
                  pl.BlockSpec((W_FF_PAD, D_MODEL), fixed),
                  pl.BlockSpec((tm, half), lambda i, j: (i % n_pos_blocks, 0)),
                  pl.BlockSpec((tm, half), lambda i, j: (i % n_pos_blocks, 0)),
                  pl.BlockSpec((ms, half), fixed),
                  pl.BlockSpec((ms, half), fixed)],
        out_specs=[pl.BlockSpec((tm, IN_TN), lambda i, j: (i, j)),
                   pl.BlockSpec((tm, IN_TN), lambda i, j: (i, clamp(j, CB_FK))),
                   pl.BlockSpec((tm, IN_TN), lambda i, j: (i, clamp(j, CB_FV))),
                   pl.BlockSpec((tm, W_FF_PAD), row),
                   pl.BlockSpec((ms, IN_TN), lambda i, j: (0, jnp.where(i == 0, j, nj))),
                   pl.BlockSpec((ms, IN_TN), lambda i, j: (0, jnp.where(i == 0, clamp(j, CB_FK), per - 1))),
                   pl.BlockSpec((ms, IN_TN), lambda i, j: (0, jnp.where(i == 0, clamp(j, CB_FV), per - 1)))],
        out_shape=[jax.ShapeDtypeStruct((m, N_MAIN), BF16),
                   jax.ShapeDtypeStruct((m, W_F), F32),
                   jax.ShapeDtypeStruct((m, W_F), F32),
                   jax.ShapeDtypeStruct((m, W_FF_PAD), F32),
                   jax.ShapeDtypeStruct((ms, N_MAIN + IN_TN), BF16),
                   jax.ShapeDtypeStruct((ms, W_F), F32),
                   jax.ShapeDtypeStruct((ms, W_F), F32)],
        compiler_params=_params(("arbitrary", "arbitrary")),
        name="in_proj",
    )(h, xs, eg, eb, w_in_t, w_ff_pad, cos, sin, cos_s, sin_s)


def _cumsum_lanes(x):
    n = x.shape[-1]
    lane = lax.broadcasted_iota(jnp.int32, x.shape, x.ndim - 1)
    s = 1
    while s < n:
        x = x + jnp.where(lane >= s, pltpu.roll(x, s, axis=x.ndim - 1), 0.0)
        s *= 2
    return x


def _logf_prompt_kernel(ff_ref, bf_ref, lf_ref, c_ref):
    ff = ff_ref[...].T[:H_F, :]
    lf = _log_sigmoid(ff + bf_ref[...])
    lf_ref[0] = lf
    c_ref[0] = _cumsum_lanes(lf) * LOG2E


def _logf_sample_kernel(x_ref, eg_ref, eb_ref, w_ref, bf_ref, lf_ref):
    h = _layer_norm(x_ref[...], eg_ref[...], eb_ref[...]).astype(BF16)
    ff = lax.dot_general(w_ref[...], h, NT_DIMS, preferred_element_type=F32)
    lf_ref[0] = _log_sigmoid(ff + bf_ref[...])


def _logf_prompt(ff, bf_col, seq):
    nb = ff.shape[0] // seq
    shape = jax.ShapeDtypeStruct((nb, H_F, seq), F32)
    spec = pl.BlockSpec((1, H_F, seq), lambda b: (b, 0, 0))
    return pl.pallas_call(
        _logf_prompt_kernel,
        grid=(nb,),
        in_specs=[pl.BlockSpec((seq, W_FF_PAD), lambda b: (b, 0)),
                  pl.BlockSpec((H_F, 1), lambda b: (0, 0))],
        out_specs=[spec, spec],
        out_shape=[shape, shape],
        compiler_params=_params(("parallel",)),
        name="log_forget_cumsum",
    )(ff, bf_col)


def _logf_sample(xs, eg, eb, w_ff_t, bf_col):
    ms = xs.shape[0]
    fixed = lambda b: (0, 0)
    return pl.pallas_call(
        _logf_sample_kernel,
        grid=(1,),
        in_specs=[pl.BlockSpec((ms, D_MODEL), fixed), pl.BlockSpec((1, D_MODEL), fixed),
                  pl.BlockSpec((1, D_MODEL), fixed), pl.BlockSpec((H_F, D_MODEL), fixed),
                  pl.BlockSpec((H_F, 1), fixed)],
        out_specs=pl.BlockSpec((1, H_F, ms), lambda b: (0, 0, 0)),
        out_shape=jax.ShapeDtypeStruct((1, H_F, ms), F32),
        compiler_params=_params(("arbitrary",)),
        name="log_forget",
    )(xs, eg, eb, w_ff_t, bf_col)


def _group_norm_gate(o, gate_bf16, gn_row):
    mu = jnp.mean(o, axis=-1, keepdims=True)
    oc = o - mu
    var = jnp.mean(oc * oc, axis=-1, keepdims=True)
    gate = gate_bf16.astype(F32)
    return oc * lax.rsqrt(var + GN_EPS) * gn_row * (gate * _sigmoid(gate))


def _ret_prompt_kernel(lg_ref, q_ref, k_ref, v_ref, g_ref, gn_ref, o_ref, st_ref,
                       dec_ref, cross_ref, kdec_ref, carry_ref):
    b = pl.program_id(0)
    c = pl.program_id(1)
    n = q_ref.shape[0]

    @pl.when((b == 0) & (c == 0))
    def _():
        ri = lax.broadcasted_iota(jnp.int32, (n, n), 0)
        ci = lax.broadcasted_iota(jnp.int32, (n, n), 1)
        causal = ri >= ci
        diff = jnp.where(causal, (ri - ci).astype(F32), 0.0)
        pos = lax.broadcasted_iota(jnp.int32, (n, DK_R), 0).astype(F32)
        for hd in range(H_R):
            lg = lg_ref[hd]
            dec_ref[hd] = jnp.where(causal, jnp.exp(lg * diff), 0.0)
            cross_ref[hd] = jnp.exp(lg * (pos + 1.0))
            kdec_ref[hd] = jnp.exp(lg * (n - 1.0 - pos))
            carry_ref[hd] = jnp.exp(jnp.full((1, DV_R), lg * n, F32))

    @pl.when(c == 0)
    def _():
        st_ref[...] = jnp.zeros_like(st_ref)

    for hd in range(H_R):
        sl = slice(hd * DK_R, (hd + 1) * DK_R)
        q = q_ref[:, sl]
        k = k_ref[:, sl]
        v = v_ref[:, sl]
        sc = lax.dot_general(q, k, NT_DIMS, preferred_element_type=F32) * dec_ref[hd]
        o = jnp.dot(sc.astype(BF16), v, preferred_element_type=F32)
        state = st_ref[0, hd]
        o = o + jnp.dot(q, state.astype(BF16), preferred_element_type=F32) * cross_ref[hd]
        kd = (k.astype(F32) * kdec_ref[hd]).astype(BF16)
        st_ref[0, hd] = carry_ref[hd] * state + lax.dot_general(kd, v, TN_DIMS, preferred_element_type=F32)
        o_ref[:, sl] = _group_norm_gate(o, g_ref[:, sl], gn_ref[hd:hd + 1, :]).astype(BF16)


def _ret_prompt(z, log_g, gn_g, batch, seq):
    nc = seq // CHUNK
    row = lambda b, c: b * nc + c
    zspec = lambda cb: pl.BlockSpec((CHUNK, COL_BLOCK), lambda b, c: (row(b, c), cb))
    return pl.pallas_call(
        _ret_prompt_kernel,
        grid=(batch, nc),
        in_specs=[pl.BlockSpec(memory_space=pltpu.SMEM),
                  zspec(CB_RQ), zspec(CB_RK), zspec(CB_RV), zspec(CB_RG),
                  pl.BlockSpec((H_R, DV_R), lambda b, c: (0, 0))],
        out_specs=[pl.BlockSpec((CHUNK, W_R), lambda b, c: (row(b, c), 0)),
                   pl.BlockSpec((1, H_R, DK_R, DV_R), lambda b, c: (b, 0, 0, 0))],
        out_shape=[jax.ShapeDtypeStruct((batch * seq, W_R), BF16),
                   jax.ShapeDtypeStruct((batch, H_R, DK_R, DV_R), F32)],
        scratch_shapes=[pltpu.VMEM((H_R, CHUNK, CHUNK), F32),
                        pltpu.VMEM((H_R, CHUNK, DV_R), F32),
                        pltpu.VMEM((H_R, CHUNK, DK_R), F32),
                        pltpu.VMEM((H_R, 1, DV_R), F32)],
        compiler_params=_params(("arbitrary", "arbitrary")),
        name="retention_prompt",
    )(log_g, z, z, z, z, gn_g)


def _ret_sample_kernel(lg_ref, q_ref, k_ref, v_ref, g_ref, gn_ref, st_ref, o_ref, ns_ref):
    row0 = lax.broadcasted_iota(jnp.int32, (8, DK_R), 0) == 0
    for sq in range(q_ref.shape[0]):
        for hd in range(H_R):
            lg = lg_ref[hd]
            sl = slice(hd * DK_R, (hd + 1) * DK_R)
            q = q_ref[sq, :, sl]
            k = k_ref[sq, :, sl]
            v = v_ref[sq, :, sl]
            qf = q.astype(F32)
            kf = k.astype(F32)
            vf = v.astype(F32)
            state = st_ref[sq, hd]
            decay = jnp.exp(jnp.full((1, DV_R), lg, F32))
            score = jnp.sum(qf * kf, axis=-1, keepdims=True)
            q8 = jnp.broadcast_to(q, (8, DK_R))
            qs = jnp.dot(q8, state.astype(BF16), preferred_element_type=F32)[0:1, :]
            o = score * vf + qs * decay
            k8 = jnp.where(row0, jnp.broadcast_to(kf, (8, DK_R)), 0.0).astype(BF16)
            v8 = jnp.broadcast_to(v, (8, DV_R))
            ns_ref[sq, hd] = decay * state + lax.dot_general(k8, v8, TN_DIMS, preferred_element_type=F32)
            o_ref[sq, :, sl] = _group_norm_gate(o, g_ref[sq, :, sl], gn_ref[hd:hd + 1, :]).astype(BF16)


def _ret_sample(z3, log_g, gn_g, state):
    nb = z3.shape[0]
    per = RET_SAMPLE_SEQS
    zspec = lambda cb: pl.BlockSpec((per, 1, COL_BLOCK), lambda b: (b, 0, cb))
    st_spec = pl.BlockSpec((per, H_R, DK_R, DV_R), lambda b: (b, 0, 0, 0))
    return pl.pallas_call(
        _ret_sample_kernel,
        grid=(nb // per,),
        in_specs=[pl.BlockSpec(memory_space=pltpu.SMEM),
                  zspec(CB_RQ), zspec(CB_RK), zspec(CB_RV), zspec(CB_RG),
                  pl.BlockSpec((H_R, DV_R), lambda b: (0, 0)),
                  st_spec],
        out_specs=[pl.BlockSpec((per, 1, W_R), lambda b: (b, 0, 0)), st_spec],
        out_shape=[jax.ShapeDtypeStruct((nb, 1, W_R), BF16),
                   jax.ShapeDtypeStruct(state.shape, F32)],
        compiler_params=_params(("parallel",)),
        name="retention_sample",
    )(log_g, z3, z3, z3, z3, gn_g, state)


def _fox_prompt_kernel(q_ref, k_ref, v_ref, crow_ref, ccol_ref, o_ref, vt_ref, m_ref, l_ref, acc_ref, *, tq):
    qi = pl.program_id(1)
    nk = vt_ref.shape[1]

    @pl.when(qi == 0)
    def _():
        for hd in range(H_F):
            for kj in range(nk):
                v_tile = v_ref[kj * tq:(kj + 1) * tq, hd * D_F:(hd + 1) * D_F]
                vt_ref[hd, kj] = v_tile.astype(F32).T.astype(BF16)

    m_ref[...] = jnp.full(m_ref.shape, -jnp.inf, F32)
    l_ref[...] = jnp.zeros_like(l_ref)
    acc_ref[...] = jnp.zeros_like(acc_ref)
    key_i = lax.broadcasted_iota(jnp.int32, (tq, tq), 0)
    qry_i = lax.broadcasted_iota(jnp.int32, (tq, tq), 1)
    visible = key_i <= qry_i

    heads = lambda ref, rows: jnp.stack([ref[rows, hd * D_F:(hd + 1) * D_F] for hd in range(H_F)], axis=0)
    q3 = heads(q_ref, slice(None))
    c_q = crow_ref[0, :, pl.ds(qi, 1), :]

    def tile(kj, on_diagonal):
        ks = pl.multiple_of(kj * tq, tq)
        k3 = heads(k_ref, pl.ds(ks, tq))
        s = lax.dot_general(k3, q3, (((2,), (2,)), ((0,), (0,))), preferred_element_type=F32)
        c_k = jnp.stack([ccol_ref[0, pl.ds(ks, tq), hd:hd + 1] for hd in range(H_F)], axis=0)
        t = s * (FOX_SCALE * LOG2E) + (c_q - c_k)
        if on_diagonal:
            t = jnp.where(visible[None], t, -jnp.inf)
        m_old = m_ref[...]
        m_new = jnp.maximum(m_old, jnp.max(t, axis=1, keepdims=True))
        alpha = jnp.exp2(m_old - m_new)
        p = jnp.exp2(t - m_new)
        l_ref[...] = alpha * l_ref[...] + jnp.sum(p, axis=1, keepdims=True)
        m_ref[...] = m_new
        pv = lax.dot_general(vt_ref[:, kj], p.astype(BF16), (((2,), (1,)), ((0,), (0,))),
                             preferred_element_type=F32)
        acc_ref[...] = alpha * acc_ref[...] + pv

    def body(kj, carry):
        tile(kj, False)
        return carry

    lax.fori_loop(0, qi, body, 0)
    tile(qi, True)
    for hd in range(H_F):
        o_ref[:, hd * D_F:(hd + 1) * D_F] = (acc_ref[hd] / l_ref[hd]).T.astype(BF16)


def _fox_prompt(z, c2_row, c2_col, batch, seq, tq=256):
    nq = seq // tq
    return pl.pallas_call(
        functools.partial(_fox_prompt_kernel, tq=tq),
        grid=(batch, nq),
        in_specs=[pl.BlockSpec((tq, COL_BLOCK), lambda b, i: (b * nq + i, CB_FQ)),
                  pl.BlockSpec((seq, COL_BLOCK), lambda b, i: (b, CB_FK)),
                  pl.BlockSpec((seq, COL_BLOCK), lambda b, i: (b, CB_FV)),
                  pl.BlockSpec((1, H_F, nq, tq), lambda b, i: (b, 0, 0, 0)),
                  pl.BlockSpec((1, seq, H_F), lambda b, i: (b, 0, 0))],
        out_specs=pl.BlockSpec((tq, W_F), lambda b, i: (b * nq + i, 0)),
        out_shape=jax.ShapeDtypeStruct((batch * seq, W_F), BF16),
        scratch_shapes=[pltpu.VMEM((H_F, nq, D_F, tq), BF16),
                        pltpu.VMEM((H_F, 1, tq), F32),
                        pltpu.VMEM((H_F, 1, tq), F32),
                        pltpu.VMEM((H_F, D_F, tq), F32)],
        compiler_params=_params(("parallel", "arbitrary")),
        name="fox_prompt",
    )(z, z, z, c2_row.reshape(batch, H_F, nq, tq), c2_col)


def _decode_bias_kernel(pt_ref, pool_ref, lnew_ref, bias_ref, rows_ref, *, n_pages):
    b = pl.program_id(0)
    width = PAGE_SIZE * H_F
    for p in range(n_pages):
        rows_ref[p:p + 1, :] = pool_ref[pl.ds(pt_ref[b, p], 1), :]
    x = rows_ref[...]
    lane = lax.broadcasted_iota(jnp.int32, x.shape, 1)
    within = x
    total = x
    s = H_F
    while s < width:
        within = within + jnp.where(lane >= s, pltpu.roll(within, s, axis=1), 0.0)
        total = total + pltpu.roll(total, s, axis=1)
        s *= 2
    rest = total - within
    run = lnew_ref[0]
    for p in reversed(range(n_pages)):
        bias_ref[0, p:p + 1, :] = run + rest[p:p + 1, :]
        run = run + total[p:p + 1, :]


def _decode_bias(page_table, logf_pool, lnew):
    nb, n_pages = page_table.shape
    width = PAGE_SIZE * H_F
    return pl.pallas_call(
        functools.partial(_decode_bias_kernel, n_pages=n_pages),
        grid_spec=pltpu.PrefetchScalarGridSpec(
            num_scalar_prefetch=1,
            grid=(nb,),
            in_specs=[pl.BlockSpec(logf_pool.shape, lambda b, pt: (0, 0), pipeline_mode=pl.Buffered(1)),
                      pl.BlockSpec((1, 1, width), lambda b, pt: (b, 0, 0))],
            out_specs=pl.BlockSpec((1, n_pages, width), lambda b, pt: (b, 0, 0)),
            scratch_shapes=[pltpu.VMEM((n_pages, width), F32)]),
        out_shape=jax.ShapeDtypeStruct((nb, n_pages, width), F32),
        compiler_params=_params(("arbitrary",)),
        name="decode_forget_bias",
    )(page_table, logf_pool, lnew)


class _DecodeRefs:
    def __init__(self, q_ref, kn_ref, vn_ref, bias_ref, k_refs, v_refs, o_ref, m_ref, l_ref, acc_ref):
        self.q, self.kn, self.vn, self.bias = q_ref, kn_ref, vn_ref, bias_ref
        self.k, self.v = k_refs, v_refs
        self.o, self.m, self.l, self.acc = o_ref, m_ref, l_ref, acc_ref


def _decode_begin(d):
    q = d.q[0].astype(F32)
    s_new = jnp.sum(q * d.kn[0].astype(F32), axis=-1, keepdims=True) * FOX_SCALE
    d.m[...] = jnp.broadcast_to(s_new, d.m.shape)
    d.l[...] = jnp.ones_like(d.l)
    d.acc[...] = d.vn[0].astype(F32)


def _decode_scores(d):
    width = PAGE_SIZE * H_F
    row = lax.broadcasted_iota(jnp.int32, (H_F, width), 0)
    lane = lax.broadcasted_iota(jnp.int32, (H_F, width), 1)
    own = (lane % H_F) == row
    q = d.q[0]
    s_parts = []
    for i, k_ref in enumerate(d.k):
        kt = k_ref[...].astype(BF16)
        s = lax.dot_general(q, kt, NT_DIMS, preferred_element_type=F32)
        s_parts.append(jnp.where(own, s * FOX_SCALE + d.bias[0, i:i + 1, :], -jnp.inf))
    return s_parts


def _decode_update(d, s_parts):
    m_old = d.m[:, 0:1]
    m_new = m_old
    for s in s_parts:
        m_new = jnp.maximum(m_new, jnp.max(s, axis=-1, keepdims=True))
    alpha = jnp.exp(m_old - m_new)
    l_new = alpha * d.l[:, 0:1]
    pv = jnp.zeros((H_F, D_F), F32)
    for s, v_ref in zip(s_parts, d.v):
        p = jnp.exp(s - m_new)
        l_new = l_new + jnp.sum(p, axis=-1, keepdims=True)
        pv = pv + jnp.dot(p.astype(BF16), v_ref[...].astype(BF16), preferred_element_type=F32)
    d.m[...] = jnp.broadcast_to(m_new, d.m.shape)
    d.l[...] = jnp.broadcast_to(l_new, d.l.shape)
    d.acc[...] = alpha * d.acc[...] + pv


def _decode_end(d):
    d.o[0] = (d.acc[...] / d.l[:, 0:1]).astype(BF16)


def _fox_decode_kernel(pt_ref, q_ref, kn_ref, vn_ref, bias_ref, *refs, pages):
    d = _DecodeRefs(q_ref, kn_ref, vn_ref, bias_ref, refs[:pages], refs[pages:2 * pages], *refs[2 * pages:])
    g = pl.program_id(1)
    pl.when(g == 0)(lambda: _decode_begin(d))
    _decode_update(d, _decode_scores(d))
    pl.when(g == pl.num_programs(1) - 1)(lambda: _decode_end(d))


def _decode_specs(pages, seq_group):
    width = PAGE_SIZE * H_F

    def head_map(*args):
        return (seq_group(*args[:-1])[0], 0, 0)

    def bias_map(*args):
        b, g = seq_group(*args[:-1])
        return (b, g, 0)

    def page_spec(i):
        def page_map(*args):
            b, g = seq_group(*args[:-1])
            return (args[-1][b, g * pages + i], 0, 0)
        return pl.BlockSpec((None, width, D_F), page_map)

    head_spec = pl.BlockSpec((1, H_F, D_F), head_map)
    in_specs = ([head_spec, head_spec, head_spec, pl.BlockSpec((1, pages, width), bias_map)]
                + [page_spec(i) for i in range(pages)] * 2)
    scratch = [pltpu.VMEM((H_F, D_F), F32), pltpu.VMEM((H_F, D_F), F32), pltpu.VMEM((H_F, D_F), F32)]
    return in_specs, head_spec, scratch


def _fox_decode(page_table, q, k_new, v_new, bias, k_pool, v_pool, pages=DECODE_PAGES_PER_STEP):
    nb, n_pages = page_table.shape
    in_specs, out_spec, scratch = _decode_specs(pages, lambda b, g: (b, g))
    return pl.pallas_call(
        functools.partial(_fox_decode_kernel, pages=pages),
        grid_spec=pltpu.PrefetchScalarGridSpec(
            num_scalar_prefetch=1,
            grid=(nb, n_pages // pages),
            in_specs=in_specs,
            out_specs=out_spec,
            scratch_shapes=scratch),
        out_shape=jax.ShapeDtypeStruct((nb, H_F, D_F), BF16),
        compiler_params=_params(("parallel", "arbitrary")),
        name="fox_decode",
    )(page_table, q, k_new, v_new, bias, *([k_pool] * pages), *([v_pool] * pages))


def _merge_kernel(or_ref, of_ref, gr0_ref, gr1_ref, gf0_ref, gf1_ref, x_ref, eg_ref, eb_ref,
                  wr_ref, wf_ref, wo_ref, g1_ref, b1_ref, x1_ref, x1b_ref):
    tm = x_ref.shape[0]
    sub = min(tm, ROW_SUBTILE)
    for r in range(tm // sub):
        rows = slice(r * sub, (r + 1) * sub)
        y_r = jnp.dot(or_ref[rows, :], wr_ref[...], preferred_element_type=F32)
        y_f = jnp.dot(of_ref[rows, :], wf_ref[...], preferred_element_type=F32)
        halves = []
        for idx, (gr_ref, gf_ref) in enumerate(((gr0_ref, gf0_ref), (gr1_ref, gf1_ref))):
            sl = slice(idx * COL_BLOCK, (idx + 1) * COL_BLOCK)
            mixed = (_sigmoid(gr_ref[rows, :].astype(F32)) * y_r[:, sl]
                     + _sigmoid(gf_ref[rows, :].astype(F32)) * y_f[:, sl])
            halves.append(mixed.astype(BF16))
        mix = jnp.dot(jnp.concatenate(halves, axis=1), wo_ref[...], preferred_element_type=F32)
        h = _layer_norm(x_ref[rows, :], eg_ref[...], eb_ref[...])
        x1 = _layer_norm(ALPHA * h + mix, g1_ref[...], b1_ref[...])
        x1_ref[rows, :] = x1
        x1b_ref[rows, :] = x1.astype(BF16)


def _merge(o_r, o_f, z, x, eg, eb, w_r, w_f, w_o, g1, b1, tm):
    m = x.shape[0]
    row = lambda i: (i, 0)
    vec = pl.BlockSpec((1, D_MODEL), lambda i: (0, 0))
    zspec = lambda cb: pl.BlockSpec((tm, COL_BLOCK), lambda i: (i, cb))
    return pl.pallas_call(
        _merge_kernel,
        grid=(m // tm,),
        in_specs=[pl.BlockSpec((tm, W_R), row), pl.BlockSpec((tm, W_F), row),
                  zspec(CB_GR), zspec(CB_GR + 1), zspec(CB_GF), zspec(CB_GF + 1),
                  pl.BlockSpec((tm, D_MODEL), row), vec, vec,
                  _resident((W_R, D_MODEL)), _resident((W_F, D_MODEL)), _resident((D_MODEL, D_MODEL)),
                  vec, vec],
        out_specs=[pl.BlockSpec((tm, D_MODEL), row), pl.BlockSpec((tm, D_MODEL), row)],
        out_shape=[jax.ShapeDtypeStruct((m, D_MODEL), F32), jax.ShapeDtypeStruct((m, D_MODEL), BF16)],
        compiler_params=_params(("parallel",)),
        name="merge_out_ln1",
    )(o_r, o_f, z, z, z, z, x, eg, eb, w_r, w_f, w_o, g1, b1)


def _ffn_up(x_ref, w1_ref, w3_ref):
    xb = x_ref[...]
    a = jnp.dot(xb, w1_ref[...].astype(BF16), preferred_element_type=F32)
    b = jnp.dot(xb, w3_ref[...].astype(BF16), preferred_element_type=F32)
    return (a * _sigmoid(a) * b).astype(BF16)


def _ffn_down(mid, w2_ref, o_ref):
    o_ref[...] += jnp.dot(mid, w2_ref[...].astype(BF16), preferred_element_type=F32)


def _ffn_tile(x_ref, w1_ref, w3_ref, w2_ref, o_ref):
    _ffn_down(_ffn_up(x_ref, w1_ref, w3_ref), w2_ref, o_ref)


def _ffn_kernel(x_ref, w1_ref, w3_ref, w2_ref, o_ref):
    @pl.when(pl.program_id(1) == 0)
    def _():
        o_ref[...] = jnp.zeros_like(o_ref)

    _ffn_tile(x_ref, w1_ref, w3_ref, w2_ref, o_ref)


def _ffn_decode_kernel(pt_ref, x_ref, w1_ref, w3_ref, w2_ref, q_ref, kn_ref, vn_ref, bias_ref, *refs,
                       pages, n_groups):
    o_ref = refs[2 * pages]
    d = _DecodeRefs(q_ref, kn_ref, vn_ref, bias_ref, refs[:pages], refs[pages:2 * pages], *refs[2 * pages + 1:])
    j = pl.program_id(1)
    g = (pl.program_id(0) * pl.num_programs(1) + j) % n_groups

    @pl.when(j == 0)
    def _():
        o_ref[...] = jnp.zeros_like(o_ref)

    pl.when(g == 0)(lambda: _decode_begin(d))
    mid = _ffn_up(x_ref, w1_ref, w3_ref)
    s_parts = _decode_scores(d)
    _ffn_down(mid, w2_ref, o_ref)
    _decode_update(d, s_parts)
    pl.when(g == n_groups - 1)(lambda: _decode_end(d))


def _ffn(x1b, w1, w3, w2, tm, tf):
    m = x1b.shape[0]
    return pl.pallas_call(
        _ffn_kernel,
        grid=(m // tm, D_FF // tf),
        in_specs=[pl.BlockSpec((tm, D_MODEL), lambda i, j: (i, 0)),
                  pl.BlockSpec((D_MODEL, tf), lambda i, j: (0, j)),
                  pl.BlockSpec((D_MODEL, tf), lambda i, j: (0, j)),
                  pl.BlockSpec((tf, D_MODEL), lambda i, j: (j, 0))],
        out_specs=pl.BlockSpec((tm, D_MODEL), lambda i, j: (i, 0)),
        out_shape=jax.ShapeDtypeStruct((m, D_MODEL), F32),
        compiler_params=_params(("parallel", "arbitrary")),
        name="swiglu_ffn_rows",
    )(x1b, w1, w3, w2)


def _ffn_with_decode(x1b, w1, w3, w2, tm, tf, page_table, q, k_new, v_new, bias, k_pool, v_pool,
                     pages=DECODE_PAGES_PER_STEP):
    m = x1b.shape[0]
    nj = D_FF // tf
    n_groups = page_table.shape[1] // pages
    n_seq = (m // tm) * nj // n_groups
    assert n_seq * n_groups == (m // tm) * nj and n_seq <= page_table.shape[0]

    def seq_group(i, j):
        s = i * nj + j
        return s // n_groups, s % n_groups

    dec_in, dec_out, dec_scratch = _decode_specs(pages, seq_group)
    return pl.pallas_call(
        functools.partial(_ffn_decode_kernel, pages=pages, n_groups=n_groups),
        grid_spec=pltpu.PrefetchScalarGridSpec(
            num_scalar_prefetch=1,
            grid=(m // tm, nj),
            in_specs=[pl.BlockSpec((tm, D_MODEL), lambda i, j, pt: (i, 0), pipeline_mode=pl.Buffered(1)),
                      pl.BlockSpec((D_MODEL, tf), lambda i, j, pt: (0, j)),
                      pl.BlockSpec((D_MODEL, tf), lambda i, j, pt: (0, j)),
                      pl.BlockSpec((tf, D_MODEL), lambda i, j, pt: (j, 0))] + dec_in,
            out_specs=[pl.BlockSpec((tm, D_MODEL), lambda i, j, pt: (i, 0)), dec_out],
            scratch_shapes=dec_scratch),
        out_shape=[jax.ShapeDtypeStruct((m, D_MODEL), F32),
                   jax.ShapeDtypeStruct((n_seq, H_F, D_F), BF16)],
        compiler_params=_params(("arbitrary", "arbitrary")),
        name="swiglu_ffn_decode",
    )(page_table, x1b, w1, w3, w2, q, k_new, v_new, bias, *([k_pool] * pages), *([v_pool] * pages))


def _ple_kernel(x1_ref, x1b_ref, ffn_ref, p_ref, wpg_ref, wpe_ref, g2_ref, b2_ref, y_ref):
    tm = x1_ref.shape[0]
    sub = min(tm, ROW_SUBTILE)
    for r in range(tm // sub):
        rows = slice(r * sub, (r + 1) * sub)
        pg = jnp.dot(x1b_ref[rows, :], wpg_ref[...], preferred_element_type=F32)
        pe = jnp.dot(p_ref[rows, :].astype(BF16), wpe_ref[...], preferred_element_type=F32)
        y_ref[rows, :] = _layer_norm(ALPHA * x1_ref[rows, :] + ffn_ref[rows, :] + _sigmoid(pg) * pe,
                                     g2_ref[...], b2_ref[...])


def _ple(x1, x1b, ffn, p, w_pg, w_pe, g2, b2, tm):
    m = x1.shape[0]
    row = lambda i: (i, 0)
    vec = pl.BlockSpec((1, D_MODEL), lambda i: (0, 0))
    return pl.pallas_call(
        _ple_kernel,
        grid=(m // tm,),
        in_specs=[pl.BlockSpec((tm, D_MODEL), row), pl.BlockSpec((tm, D_MODEL), row),
                  pl.BlockSpec((tm, D_MODEL), row), pl.BlockSpec((tm, D_PLE), row),
                  _resident((D_MODEL, D_MODEL)), _resident((D_PLE, D_MODEL)), vec, vec],
        out_specs=pl.BlockSpec((tm, D_MODEL), row),
        out_shape=jax.ShapeDtypeStruct((m, D_MODEL), F32),
        compiler_params=_params(("parallel",)),
        name="ple_ln2",
    )(x1, x1b, ffn, p, w_pg, w_pe, g2, b2)


def _rope_tables(pos):
    half = DK_R // 2
    inv = ROPE_BASE ** (-jnp.arange(half, dtype=F32) / half)
    ang = pos.astype(F32)[:, None] * inv[None, :]
    return jnp.cos(ang), jnp.sin(ang)


def kernel(x_prompt, x_sample, state_ret, cache_k, cache_v, cache_logf, page_table, p_prompt, p_sample, ln_emb_g, ln_emb_b, w_in, b_f, gn_g, w_ret_proj, w_fox_proj, w_out, ln1_g, ln1_b, w1, w3, w2, w_pg, w_pe, ln2_g, ln2_b):
    assert w_in.shape[0] == DEPTH == 1
    batch, seq, _ = x_prompt.shape
    dec_b, dec_t, _ = x_sample.shape
    assert dec_t == 1
    n_pages = page_table.shape[1]
    past_len = n_pages * PAGE_SIZE
    n_pool = cache_k.shape[1]

    w_in_t = jnp.swapaxes(w_in[0], 0, 1)
    w_ff_t = w_in_t[FF_LO:FF_HI].astype(BF16)
    w_ff_pad = jnp.pad(w_ff_t, ((0, W_FF_PAD - H_F), (0, 0)))
    bf_col = b_f[0].reshape(H_F, 1)
    vec = lambda v: v.reshape(1, D_MODEL)
    eg, eb = vec(ln_emb_g), vec(ln_emb_b)
    g1, b1, g2, b2 = vec(ln1_g[0]), vec(ln1_b[0]), vec(ln2_g[0]), vec(ln2_b[0])
    w_r, w_f, w_o = w_ret_proj[0].astype(BF16), w_fox_proj[0].astype(BF16), w_out[0].astype(BF16)
    w_pgb, w_peb = w_pg[0].astype(BF16), w_pe[0].astype(BF16)
    log_g = jnp.log1p(-jnp.exp2(-5.0 - jnp.arange(H_R, dtype=F32)))
    cos_p, sin_p = _rope_tables(jnp.arange(seq))
    cos_s, sin_s = _rope_tables(jnp.full((dec_b,), past_len))

    xp = x_prompt.reshape(batch * seq, D_MODEL)
    xs = x_sample.reshape(dec_b, D_MODEL)

    h_p = _entry_ln(xp, eg, eb, tm=512)
    z_p, fk_p, fv_p, ff_p, z_s, fk_s, fv_s = _inproj(h_p, xs, eg, eb, w_in_t, w_ff_pad,
                                                     cos_p, sin_p, cos_s, sin_s, tm=IN_TM)
    lf_p, c2_row = _logf_prompt(ff_p, bf_col, seq)
    lf_s = _logf_sample(xs, eg, eb, w_ff_t, bf_col)

    o_r_p, st_p = _ret_prompt(z_p, log_g, gn_g[0], batch, seq)
    o_f_p = _fox_prompt(z_p, c2_row, jnp.swapaxes(c2_row, 1, 2), batch, seq)

    z3 = z_s.reshape(dec_b, 1, z_s.shape[1])
    o_r_s, st_s = _ret_sample(z3, log_g, gn_g[0], state_ret[0])
    width = PAGE_SIZE * H_F
    lf_new = jnp.swapaxes(lf_s, 1, 2).reshape(dec_b, H_F)
    lnew = jnp.tile(lf_new, (1, PAGE_SIZE)).reshape(dec_b, 1, width)
    bias = _decode_bias(page_table, cache_logf[0].reshape(n_pool, width), lnew)
    heads = lambda cb: z_s[:, cb * COL_BLOCK:(cb + 1) * COL_BLOCK].reshape(dec_b, H_F, D_F)
    decode_args = (heads(CB_FQ), heads(CB_FK), heads(CB_FV), bias,
                   cache_k[0].reshape(n_pool, width, D_F), cache_v[0].reshape(n_pool, width, D_F))

    x1_p, x1b_p = _merge(o_r_p, o_f_p, z_p, xp, eg, eb, w_r, w_f, w_o, g1, b1, tm=MERGE_TM)
    ffn_p, o_f_ride = _ffn_with_decode(x1b_p, w1[0], w3[0], w2[0], FFN_TM, FFN_TF, page_table, *decode_args)
    n_ride = o_f_ride.shape[0]
    o_f_rest = _fox_decode(page_table[n_ride:], *(a[n_ride:] for a in decode_args[:4]), *decode_args[4:])
    o_f_s = jnp.concatenate([o_f_ride, o_f_rest], axis=0)
    x1_s, x1b_s = _merge(o_r_s.reshape(dec_b, W_R), o_f_s.reshape(dec_b, W_F), z_s, xs, eg, eb,
                         w_r, w_f, w_o, g1, b1, tm=dec_b)
    ffn_s = _ffn(x1b_s, w1[0], w3[0], w2[0], tm=dec_b, tf=FFN_TF)
    y_p = _ple(x1_p, x1b_p, ffn_p, p_prompt[0].reshape(batch * seq, D_PLE), w_pgb, w_peb, g2, b2, tm=PLE_TM)
    y_s = _ple(x1_s, x1b_s, ffn_s, p_sample[0].reshape(dec_b, D_PLE), w_pgb, w_peb, g2, b2, tm=dec_b)

    return (y_p.reshape(batch, seq, D_MODEL),
            y_s.reshape(dec_b, 1, D_MODEL),
            st_p[None],
            fk_p.reshape(1, batch, seq, H_F, D_F),
            fv_p.reshape(1, batch, seq, H_F, D_F),
            jnp.swapaxes(lf_p, 1, 2)[None],
            st_s[None],
            fk_s.reshape(1, dec_b, 1, H_F, D_F),
            fv_s.reshape(1, dec_b, 1, H_F, D_F),
            jnp.swapaxes(lf_s, 1, 2).reshape(1, dec_b, 1, H_F))
```

```python
import functools

import jax
import jax.numpy as jnp
import numpy as np
from jax import lax
from jax.experimental import pallas as pl
from jax.experimental.pallas import tpu as pltpu

F32 = jnp.float32
BF16 = jnp.bfloat16

D_MODEL = 2048
DEPTH = 1
PAGE_SIZE = 128
H_R = 4
DK_R = 256
DV_R = 256
W_R = H_R * DV_R
CHUNK = 256
ROPE_BASE = 10000.0
GN_EPS = 1e-6
H_F = 8
D_F = 128
W_F = H_F * D_F
D_FF = 5632
D_PLE = 256
LN_EPS = 1e-5
ALPHA = (2.0 * DEPTH) ** 0.25
FOX_SCALE = D_F ** -0.5
LOG2E = float(np.log2(np.e))

COL_BLOCK = 1024
N_MAIN = 11 * COL_BLOCK
CB_RQ, CB_RK, CB_RV, CB_RG, CB_FQ, CB_FK, CB_FV, CB_GR, CB_GF = 0, 1, 2, 3, 4, 5, 6, 7, 9
FF_LO, FF_HI = 7 * COL_BLOCK, 7 * COL_BLOCK + H_F

IN_TM, IN_TN = 1024, 1024
W_FF_PAD = 128
N_A_BLOCKS = FF_LO // IN_TN
FFN_TM, FFN_TF = 1024, 256
MERGE_TM, PLE_TM, ROW_SUBTILE = 256, 512, 256

VMEM_LIMIT = 56 * 1024 * 1024
DECODE_PAGES_PER_STEP = 8
RET_SAMPLE_SEQS = 4

NT_DIMS = (((1,), (1,)), ((), ()))
TN_DIMS = (((0,), (0,)), ((), ()))


def _params(semantics):
    return pltpu.CompilerParams(dimension_semantics=semantics, vmem_limit_bytes=VMEM_LIMIT)


def _layer_norm(x, g, b):
    mu = jnp.mean(x, axis=-1, keepdims=True)
    xc = x - mu
    var = jnp.mean(xc * xc, axis=-1, keepdims=True)
    return xc * lax.rsqrt(var + LN_EPS) * g + b


def _sigmoid(x):
    return 1.0 / (1.0 + jnp.exp(-x))


def _log_sigmoid(x):
    return jnp.minimum(x, 0.0) - jnp.log(1.0 + jnp.exp(-jnp.abs(x)))


def _resident(shape):
    return pl.BlockSpec(shape, lambda *_: (0,) * len(shape), pipeline_mode=pl.Buffered(1))


def _ln_kernel(x_ref, g_ref, b_ref, o_ref):
    o_ref[...] = _layer_norm(x_ref[...], g_ref[...], b_ref[...]).astype(o_ref.dtype)


def _entry_ln(x, g, b, tm):
    m = x.shape[0]
    return pl.pallas_call(
        _ln_kernel,
        grid=(m // tm,),
        in_specs=[pl.BlockSpec((tm, D_MODEL), lambda i: (i, 0)),
                  pl.BlockSpec((1, D_MODEL), lambda i: (0, 0)),
                  pl.BlockSpec((1, D_MODEL), lambda i: (0, 0))],
        out_specs=pl.BlockSpec((tm, D_MODEL), lambda i: (i, 0)),
        out_shape=jax.ShapeDtypeStruct((m, D_MODEL), BF16),
        compiler_params=_params(("parallel",)),
        name="entry_ln",
    )(x, g, b)


def _inproj_block(h, w_ref, z_ref, rope=None, f32_ref=None):
    half = DK_R // 2
    for hd in range(IN_TN // DK_R):
        lo = hd * DK_R
        acc = lax.dot_general(h, w_ref[lo:lo + DK_R, :].astype(BF16), NT_DIMS, preferred_element_type=F32)
        if rope is None:
            z_ref[:, lo:lo + DK_R] = acc.astype(BF16)
        else:
            cos, sin, scale = rope
            x1 = acc[:, :half]
            x2 = acc[:, half:]
            z_ref[:, lo:lo + half] = ((x1 * cos - x2 * sin) * scale).astype(BF16)
            z_ref[:, lo + half:lo + DK_R] = ((x2 * cos + x1 * sin) * scale).astype(BF16)
        if f32_ref is not None:
            f32_ref[:, lo:lo + DK_R] = acc


def _inproj_kernel(h_ref, xs_ref, eg_ref, eb_ref, w_ref, wff_ref, cos_ref, sin_ref, coss_ref, sins_ref,
                   z_ref, fk_ref, fv_ref, ff_ref, zs_ref, fks_ref, fvs_ref):
    i = pl.program_id(0)
    j = pl.program_id(1)
    per = COL_BLOCK // IN_TN

    @pl.when(j == 0)
    def _():
        ff_ref[...] = lax.dot_general(h_ref[...], wff_ref[...], NT_DIMS, preferred_element_type=F32)

    def section(lo, hi, rope_scale=None, f32_refs=(None, None)):
        @pl.when((j >= lo) & (j < hi))
        def _():
            rope = None if rope_scale is None else (cos_ref[...], sin_ref[...], rope_scale)
            _inproj_block(h_ref[...], w_ref, z_ref, rope, f32_refs[0])

            @pl.when(i == 0)
            def _():
                hs = _layer_norm(xs_ref[...], eg_ref[...], eb_ref[...]).astype(BF16)
                rope_s = None if rope_scale is None else (coss_ref[...], sins_ref[...], rope_scale)
                _inproj_block(hs, w_ref, zs_ref, rope_s, f32_refs[1])

    @pl.when(i > 0)
    def _():
        zs_ref[...] = jnp.zeros_like(zs_ref)

    section(CB_RQ * per, CB_RK * per, rope_scale=1.0)
    section(CB_RK * per, CB_RV * per, rope_scale=DK_R ** -0.5)
    section(CB_RV * per, CB_FK * per)
    section(CB_FK * per, CB_FV * per, f32_refs=(fk_ref, fks_ref))
    section(CB_FV * per, N_A_BLOCKS, f32_refs=(fv_ref, fvs_ref))
    section(N_A_BLOCKS, N_MAIN // IN_TN)


def _inproj(h, xs, eg, eb, w_in_t, w_ff_pad, cos, sin, cos_s, sin_s, tm):
    m = h.shape[0]
    ms = xs.shape[0]
    n_pos_blocks = cos.shape[0] // tm
    nj = N_MAIN // IN_TN
    per = COL_BLOCK // IN_TN
    half = DK_R // 2
    clamp = lambda j, cb: jnp.clip(j - cb * per, 0, per - 1)
    row = lambda i, j: (i, 0)
    fixed = lambda i, j: (0, 0)
    return pl.pallas_call(
        _inproj_kernel,
        grid=(m // tm, nj),
        in_specs=[pl.BlockSpec((tm, D_MODEL), row),
                  pl.BlockSpec((ms, D_MODEL), fixed),
                  pl.BlockSpec((1, D_MODEL), fixed),
                  pl.BlockSpec((1, D_MODEL), fixed),
                  pl.BlockSpec((pl.Element(IN_TN), pl.Element(D_MODEL)),
                               lambda i, j: (pl.multiple_of(IN_TN * j + jnp.where(j < N_A_BLOCKS, 0, H_F), 8), 0)),
                  pl.BlockSpec((W_FF_PAD, D_MODEL), fixed),
                  pl.BlockSpec((tm, half), lambda i, j: (i % n_pos_blocks, 0)),
                  pl.BlockSpec((tm, half), lambda i, j: (i % n_pos_blocks, 0)),
                  pl.BlockSpec((ms, half), fixed),
                  pl.BlockSpec((ms, half), fixed)],
        out_specs=[pl.BlockSpec((tm, IN_TN), lambda i, j: (i, j)),
                   pl.BlockSpec((tm, IN_TN), lambda i, j: (i, clamp(j, CB_FK))),
                   pl.BlockSpec((tm, IN_TN), lambda i, j: (i, clamp(j, CB_FV))),
                   pl.BlockSpec((tm, W_FF_PAD), row),
                   pl.BlockSpec((ms, IN_TN), lambda i, j: (0, jnp.where(i == 0, j, nj))),
                   pl.BlockSpec((ms, IN_TN), lambda i, j: (0, jnp.where(i == 0, clamp(j, CB_FK), per - 1))),
                   pl.BlockSpec((ms, IN_TN), lambda i, j: (0, jnp.where(i == 0, clamp(j, CB_FV), per - 1)))],
        out_shape=[jax.ShapeDtypeStruct((m, N_MAIN), BF16),
                   jax.ShapeDtypeStruct((m, W_F), F32),
                   jax.ShapeDtypeStruct((m, W_F), F32),
                   jax.ShapeDtypeStruct((m, W_FF_PAD), F32),
                   jax.ShapeDtypeStruct((ms, N_MAIN + IN_TN), BF16),
                   jax.ShapeDtypeStruct((ms, W_F), F32),
                   jax.ShapeDtypeStruct((ms, W_F), F32)],
        compiler_params=_params(("arbitrary", "arbitrary")),
        name="in_proj",
    )(h, xs, eg, eb, w_in_t, w_ff_pad, cos, sin, cos_s, sin_s)


def _cumsum_lanes(x):
    n = x.shape[-1]
    lane = lax.broadcasted_iota(jnp.int32, x.shape, x.ndim - 1)
    s = 1
    while s < n:
        x = x + jnp.where(lane >= s, pltpu.roll(x, s, axis=x.ndim - 1), 0.0)
        s *= 2
    return x


def _logf_prompt_kernel(ff_ref, bf_ref, lf_ref, c_ref):
    ff = ff_ref[...].T[:H_F, :]
    lf = _log_sigmoid(ff + bf_ref[...])
    lf_ref[0] = lf
    c_ref[0] = _cumsum_lanes(lf) * LOG2E


def _logf_sample_kernel(x_ref, eg_ref, eb_ref, w_ref, bf_ref, lf_ref):
    h = _layer_norm(x_ref[...], eg_ref[...], eb_ref[...]).astype(BF16)
    ff = lax.dot_general(w_ref[...], h, NT_DIMS, preferred_element_type=F32)
    lf_ref[0] = _log_sigmoid(ff + bf_ref[...])


def _logf_prompt(ff, bf_col, seq):
    nb = ff.shape[0] // seq
    shape = jax.ShapeDtypeStruct((nb, H_F, seq), F32)
    spec = pl.BlockSpec((1, H_F, seq), lambda b: (b, 0, 0))
    return pl.pallas_call(
        _logf_prompt_kernel,
        grid=(nb,),
        in_specs=[pl.BlockSpec((seq, W_FF_PAD), lambda b: (b, 0)),
                  pl.BlockSpec((H_F, 1), lambda b: (0, 0))],
        out_specs=[spec, spec],
        out_shape=[shape, shape],
        compiler_params=_params(("parallel",)),
        name="log_forget_cumsum",
    )(ff, bf_col)


def _logf_sample(xs, eg, eb, w_ff_t, bf_col):
    ms = xs.shape[0]
    fixed = lambda b: (0, 0)
    return pl.pallas_call(
        _logf_sample_kernel,
        grid=(1,),
        in_specs=[pl.BlockSpec((ms, D_MODEL), fixed), pl.BlockSpec((1, D_MODEL), fixed),
                  pl.BlockSpec((1, D_MODEL), fixed), pl.BlockSpec((H_F, D_MODEL), fixed),
                  pl.BlockSpec((H_F, 1), fixed)],
        out_specs=pl.BlockSpec((1, H_F, ms), lambda b: (0, 0, 0)),
        out_shape=jax.ShapeDtypeStruct((1, H_F, ms), F32),
        compiler_params=_params(("arbitrary",)),
        name="log_forget",
    )(xs, eg, eb, w_ff_t, bf_col)


def _group_norm_gate(o, gate_bf16, gn_row):
    mu = jnp.mean(o, axis=-1, keepdims=True)
    oc = o - mu
    var = jnp.mean(oc * oc, axis=-1, keepdims=True)
    gate = gate_bf16.astype(F32)
    return oc * lax.rsqrt(var + GN_EPS) * gn_row * (gate * _sigmoid(gate))


def _ret_prompt_kernel(lg_ref, q_ref, k_ref, v_ref, g_ref, gn_ref, o_ref, st_ref,
                       dec_ref, cross_ref, kdec_ref, carry_ref):
    b = pl.program_id(0)
    c = pl.program_id(1)
    n = q_ref.shape[0]

    @pl.when((b == 0) & (c == 0))
    def _():
        ri = lax.broadcasted_iota(jnp.int32, (n, n), 0)
        ci = lax.broadcasted_iota(jnp.int32, (n, n), 1)
        causal = ri >= ci
        diff = jnp.where(causal, (ri - ci).astype(F32), 0.0)
        pos = lax.broadcasted_iota(jnp.int32, (n, DK_R), 0).astype(F32)
        for hd in range(H_R):
            lg = lg_ref[hd]
            dec_ref[hd] = jnp.where(causal, jnp.exp(lg * diff), 0.0)
            cross_ref[hd] = jnp.exp(lg * (pos + 1.0))
            kdec_ref[hd] = jnp.exp(lg * (n - 1.0 - pos))
            carry_ref[hd] = jnp.exp(jnp.full((1, DV_R), lg * n, F32))

    @pl.when(c == 0)
    def _():
        st_ref[...] = jnp.zeros_like(st_ref)

    heads = lambda ref: jnp.stack([ref[:, hd * DK_R:(hd + 1) * DK_R] for hd in range(H_R)], axis=0)
    bmm = lambda a, b_, ca, cb: lax.dot_general(a, b_, (((ca,), (cb,)), ((0,), (0,))), preferred_element_type=F32)
    q3, k3, v3 = heads(q_ref), heads(k_ref), heads(v_ref)
    sc = bmm(q3, k3, 2, 2) * dec_ref[...]
    o = bmm(sc.astype(BF16), v3, 2, 1)
    state = st_ref[0]
    o = o + bmm(q3, state.astype(BF16), 2, 1) * cross_ref[...]
    kd = (k3.astype(F32) * kdec_ref[...]).astype(BF16)
    st_ref[0] = carry_ref[...] * state + bmm(kd, v3, 1, 1)
    out = _group_norm_gate(o, heads(g_ref), gn_ref[...][:, None, :]).astype(BF16)
    for hd in range(H_R):
        o_ref[:, hd * DV_R:(hd + 1) * DV_R] = out[hd]


def _ret_prompt(z, log_g, gn_g, batch, seq):
    nc = seq // CHUNK
    row = lambda b, c: b * nc + c
    zspec = lambda cb: pl.BlockSpec((CHUNK, COL_BLOCK), lambda b, c: (row(b, c), cb))
    return pl.pallas_call(
        _ret_prompt_kernel,
        grid=(batch, nc),
        in_specs=[pl.BlockSpec(memory_space=pltpu.SMEM),
                  zspec(CB_RQ), zspec(CB_RK), zspec(CB_RV), zspec(CB_RG),
                  pl.BlockSpec((H_R, DV_R), lambda b, c: (0, 0))],
        out_specs=[pl.BlockSpec((CHUNK, W_R), lambda b, c: (row(b, c), 0)),
                   pl.BlockSpec((1, H_R, DK_R, DV_R), lambda b, c: (b, 0, 0, 0))],
        out_shape=[jax.ShapeDtypeStruct((batch * seq, W_R), BF16),
                   jax.ShapeDtypeStruct((batch, H_R, DK_R, DV_R), F32)],
        scratch_shapes=[pltpu.VMEM((H_R, CHUNK, CHUNK), F32),
                        pltpu.VMEM((H_R, CHUNK, DV_R), F32),
                        pltpu.VMEM((H_R, CHUNK, DK_R), F32),
                        pltpu.VMEM((H_R, 1, DV_R), F32)],
        compiler_params=_params(("arbitrary", "arbitrary")),
        name="retention_prompt",
    )(log_g, z, z, z, z, gn_g)


def _ret_sample_kernel(lg_ref, q_ref, k_ref, v_ref, g_ref, gn_ref, st_ref, o_ref, ns_ref):
    row0 = lax.broadcasted_iota(jnp.int32, (8, DK_R), 0) == 0
    for sq in range(q_ref.shape[0]):
        for hd in range(H_R):
            lg = lg_ref[hd]
            sl = slice(hd * DK_R, (hd + 1) * DK_R)
            q = q_ref[sq, :, sl]
            k = k_ref[sq, :, sl]
            v = v_ref[sq, :, sl]
            qf = q.astype(F32)
            kf = k.astype(F32)
            vf = v.astype(F32)
            state = st_ref[sq, hd]
            decay = jnp.exp(jnp.full((1, DV_R), lg, F32))
            score = jnp.sum(qf * kf, axis=-1, keepdims=True)
            q8 = jnp.broadcast_to(q, (8, DK_R))
            qs = jnp.dot(q8, state.astype(BF16), preferred_element_type=F32)[0:1, :]
            o = score * vf + qs * decay
            k8 = jnp.where(row0, jnp.broadcast_to(kf, (8, DK_R)), 0.0).astype(BF16)
            v8 = jnp.broadcast_to(v, (8, DV_R))
            ns_ref[sq, hd] = decay * state + lax.dot_general(k8, v8, TN_DIMS, preferred_element_type=F32)
            o_ref[sq, :, sl] = _group_norm_gate(o, g_ref[sq, :, sl], gn_ref[hd:hd + 1, :]).astype(BF16)


def _ret_sample(z3, log_g, gn_g, state):
    nb = z3.shape[0]
    per = RET_SAMPLE_SEQS
    zspec = lambda cb: pl.BlockSpec((per, 1, COL_BLOCK), lambda b: (b, 0, cb))
    st_spec = pl.BlockSpec((per, H_R, DK_R, DV_R), lambda b: (b, 0, 0, 0))
    return pl.pallas_call(
        _ret_sample_kernel,
        grid=(nb // per,),
        in_specs=[pl.BlockSpec(memory_space=pltpu.SMEM),
                  zspec(CB_RQ), zspec(CB_RK), zspec(CB_RV), zspec(CB_RG),
                  pl.BlockSpec((H_R, DV_R), lambda b: (0, 0)),
                  st_spec],
        out_specs=[pl.BlockSpec((per, 1, W_R), lambda b: (b, 0, 0)), st_spec],
        out_shape=[jax.ShapeDtypeStruct((nb, 1, W_R), BF16),
                   jax.ShapeDtypeStruct(state.shape, F32)],
        compiler_params=_params(("parallel",)),
        name="retention_sample",
    )(log_g, z3, z3, z3, z3, gn_g, state)


def _split3(x):
    hi = x.astype(BF16).astype(F32)
    r1 = x - hi
    mid = r1.astype(BF16).astype(F32)
    return hi, mid, r1 - mid


def _fox_prompt_kernel(q_ref, k_ref, v_ref, crow_ref, o_ref, vt_ref, kx_ref, m_ref, l_ref, acc_ref,
                       *, tq):
    qi = pl.program_id(1)
    nk = vt_ref.shape[1]
    score_scale = FOX_SCALE * LOG2E

    @pl.when(qi == 0)
    def _():
        sel_r = lax.broadcasted_iota(jnp.int32, (8, D_F), 0)
        sel_l = lax.broadcasted_iota(jnp.int32, (8, D_F), 1)
        sel = jnp.where((sel_r == sel_l) & (sel_r < 3), 1.0, 0.0).astype(BF16)
        for hd in range(H_F):
            for kj in range(nk):
                v_tile = v_ref[kj * tq:(kj + 1) * tq, hd * D_F:(hd + 1) * D_F]
                vt_ref[hd, kj] = v_tile.astype(F32).T.astype(BF16)
                parts = _split3(crow_ref[0, hd, kj:kj + 1, :] * (-1.0 / score_scale))
                rows = jnp.concatenate(list(parts) + [jnp.zeros((5, tq), F32)], axis=0).astype(BF16)
                kx_ref[hd, kj] = lax.dot_general(rows, sel, TN_DIMS, preferred_element_type=F32).astype(BF16)

    m_ref[...] = jnp.full(m_ref.shape, -jnp.inf, F32)
    l_ref[...] = jnp.zeros_like(l_ref)
    acc_ref[...] = jnp.zeros_like(acc_ref)
    key_i = lax.broadcasted_iota(jnp.int32, (tq, tq), 0)
    qry_i = lax.broadcasted_iota(jnp.int32, (tq, tq), 1)
    visible = key_i <= qry_i

    heads = lambda ref, rows: jnp.stack([ref[rows, hd * D_F:(hd + 1) * D_F] for hd in range(H_F)], axis=0)
    ones3 = jnp.where(lax.broadcasted_iota(jnp.int32, (H_F, tq, D_F), 2) < 3, 1.0, 0.0).astype(BF16)
    q3 = jnp.concatenate([heads(q_ref, slice(None)), ones3], axis=2)
    c_q = crow_ref[0, :, pl.ds(qi, 1), :]

    def tile(kj, on_diagonal):
        ks = pl.multiple_of(kj * tq, tq)
        k3 = jnp.concatenate([heads(k_ref, pl.ds(ks, tq)), kx_ref[:, kj]], axis=2)
        s = lax.dot_general(k3, q3, (((2,), (2,)), ((0,), (0,))), preferred_element_type=F32)
        t = s * score_scale + c_q
        if on_diagonal:
            t = jnp.where(visible[None], t, -jnp.inf)
        m_old = m_ref[...]
        m_new = jnp.maximum(m_old, jnp.max(t, axis=1, keepdims=True))
        alpha = jnp.exp2(m_old - m_new)
        p = jnp.exp2(t - m_new)
        l_ref[...] = alpha * l_ref[...] + jnp.sum(p, axis=1, keepdims=True)
        m_ref[...] = m_new
        pv = lax.dot_general(vt_ref[:, kj], p.astype(BF16), (((2,), (1,)), ((0,), (0,))),
                             preferred_element_type=F32)
        acc_ref[...] = alpha * acc_ref[...] + pv

    def body(kj, carry):
        tile(kj, False)
        return carry

    lax.fori_loop(0, qi, body, 0)
    tile(qi, True)
    for hd in range(H_F):
        o_ref[:, hd * D_F:(hd + 1) * D_F] = (acc_ref[hd] / l_ref[hd]).T.astype(BF16)


def _fox_prompt(z, c2_row, batch, seq, tq=256):
    nq = seq // tq
    return pl.pallas_call(
        functools.partial(_fox_prompt_kernel, tq=tq),
        grid=(batch, nq),
        in_specs=[pl.BlockSpec((tq, COL_BLOCK), lambda b, i: (b * nq + i, CB_FQ)),
                  pl.BlockSpec((seq, COL_BLOCK), lambda b, i: (b, CB_FK)),
                  pl.BlockSpec((seq, COL_BLOCK), lambda b, i: (b, CB_FV)),
                  pl.BlockSpec((1, H_F, nq, tq), lambda b, i: (b, 0, 0, 0))],
        out_specs=pl.BlockSpec((tq, W_F), lambda b, i: (b * nq + i, 0)),
        out_shape=jax.ShapeDtypeStruct((batch * seq, W_F), BF16),
        scratch_shapes=[pltpu.VMEM((H_F, nq, D_F, tq), BF16),
                        pltpu.VMEM((H_F, nq, tq, D_F), BF16),
                        pltpu.VMEM((H_F, 1, tq), F32),
                        pltpu.VMEM((H_F, 1, tq), F32),
                        pltpu.VMEM((H_F, D_F, tq), F32)],
        compiler_params=_params(("parallel", "arbitrary")),
        name="fox_prompt",
    )(z, z, z, c2_row.reshape(batch, H_F, nq, tq))


def _decode_bias_kernel(pt_ref, pool_ref, lnew_ref, bias_ref, rows_ref, *, n_pages):
    b = pl.program_id(0)
    width = PAGE_SIZE * H_F
    for p in range(n_pages):
        rows_ref[p:p + 1, :] = pool_ref[pl.ds(pt_ref[b, p], 1), :]
    x = rows_ref[...]
    lane = lax.broadcasted_iota(jnp.int32, x.shape, 1)
    within = x
    total = x
    s = H_F
    while s < width:
        within = within + jnp.where(lane >= s, pltpu.roll(within, s, axis=1), 0.0)
        total = total + pltpu.roll(total, s, axis=1)
        s *= 2
    rest = total - within
    run = lnew_ref[0]
    for p in reversed(range(n_pages)):
        bias_ref[0, p:p + 1, :] = run + rest[p:p + 1, :]
        run = run + total[p:p + 1, :]


def _decode_bias(page_table, logf_pool, lnew):
    nb, n_pages = page_table.shape
    width = PAGE_SIZE * H_F
    return pl.pallas_call(
        functools.partial(_decode_bias_kernel, n_pages=n_pages),
        grid_spec=pltpu.PrefetchScalarGridSpec(
            num_scalar_prefetch=1,
            grid=(nb,),
            in_specs=[pl.BlockSpec(logf_pool.shape, lambda b, pt: (0, 0), pipeline_mode=pl.Buffered(1)),
                      pl.BlockSpec((1, 1, width), lambda b, pt: (b, 0, 0))],
            out_specs=pl.BlockSpec((1, n_pages, width), lambda b, pt: (b, 0, 0)),
            scratch_shapes=[pltpu.VMEM((n_pages, width), F32)]),
        out_shape=jax.ShapeDtypeStruct((nb, n_pages, width), F32),
        compiler_params=_params(("arbitrary",)),
        name="decode_forget_bias",
    )(page_table, logf_pool, lnew)


class _DecodeRefs:
    def __init__(self, q_ref, kn_ref, vn_ref, bias_ref, k_refs, v_refs, o_ref, m_ref, l_ref, acc_ref):
        self.q, self.kn, self.vn, self.bias = q_ref, kn_ref, vn_ref, bias_ref
        self.k, self.v = k_refs, v_refs
        self.o, self.m, self.l, self.acc = o_ref, m_ref, l_ref, acc_ref


def _decode_begin(d):
    q = d.q[0].astype(F32)
    s_new = jnp.sum(q * d.kn[0].astype(F32), axis=-1, keepdims=True) * FOX_SCALE
    d.m[...] = jnp.broadcast_to(s_new, d.m.shape)
    d.l[...] = jnp.ones_like(d.l)
    d.acc[...] = d.vn[0].astype(F32)


def _decode_scores(d):
    width = PAGE_SIZE * H_F
    row = lax.broadcasted_iota(jnp.int32, (H_F, width), 0)
    lane = lax.broadcasted_iota(jnp.int32, (H_F, width), 1)
    own = (lane % H_F) == row
    q = d.q[0]
    s_parts = []
    for i, k_ref in enumerate(d.k):
        kt = k_ref[...].astype(BF16)
        s = lax.dot_general(q, kt, NT_DIMS, preferred_element_type=F32)
        s_parts.append(jnp.where(own, s * FOX_SCALE + d.bias[0, i:i + 1, :], -jnp.inf))
    return s_parts


def _decode_update(d, s_parts):
    m_old = d.m[:, 0:1]
    m_new = m_old
    for s in s_parts:
        m_new = jnp.maximum(m_new, jnp.max(s, axis=-1, keepdims=True))
    alpha = jnp.exp(m_old - m_new)
    l_new = alpha * d.l[:, 0:1]
    pv = jnp.zeros((H_F, D_F), F32)
    for s, v_ref in zip(s_parts, d.v):
        p = jnp.exp(s - m_new)
        l_new = l_new + jnp.sum(p, axis=-1, keepdims=True)
        pv = pv + jnp.dot(p.astype(BF16), v_ref[...].astype(BF16), preferred_element_type=F32)
    d.m[...] = jnp.broadcast_to(m_new, d.m.shape)
    d.l[...] = jnp.broadcast_to(l_new, d.l.shape)
    d.acc[...] = alpha * d.acc[...] + pv


def _decode_end(d):
    d.o[0] = (d.acc[...] / d.l[:, 0:1]).astype(BF16)


def _fox_decode_kernel(pt_ref, q_ref, kn_ref, vn_ref, bias_ref, *refs, pages):
    d = _DecodeRefs(q_ref, kn_ref, vn_ref, bias_ref, refs[:pages], refs[pages:2 * pages], *refs[2 * pages:])
    g = pl.program_id(1)
    pl.when(g == 0)(lambda: _decode_begin(d))
    _decode_update(d, _decode_scores(d))
    pl.when(g == pl.num_programs(1) - 1)(lambda: _decode_end(d))


def _decode_specs(pages, seq_group):
    width = PAGE_SIZE * H_F

    def head_map(*args):
        return (seq_group(*args[:-1])[0], 0, 0)

    def bias_map(*args):
        b, g = seq_group(*args[:-1])
        return (b, g, 0)

    def page_spec(i):
        def page_map(*args):
            b, g = seq_group(*args[:-1])
            return (args[-1][b, g * pages + i], 0, 0)
        return pl.BlockSpec((None, width, D_F), page_map)

    head_spec = pl.BlockSpec((1, H_F, D_F), head_map)
    in_specs = ([head_spec, head_spec, head_spec, pl.BlockSpec((1, pages, width), bias_map)]
                + [page_spec(i) for i in range(pages)] * 2)
    scratch = [pltpu.VMEM((H_F, D_F), F32), pltpu.VMEM((H_F, D_F), F32), pltpu.VMEM((H_F, D_F), F32)]
    return in_specs, head_spec, scratch


def _fox_decode(page_table, q, k_new, v_new, bias, k_pool, v_pool, pages=DECODE_PAGES_PER_STEP):
    nb, n_pages = page_table.shape
    in_specs, out_spec, scratch = _decode_specs(pages, lambda b, g: (b, g))
    return pl.pallas_call(
        functools.partial(_fox_decode_kernel, pages=pages),
        grid_spec=pltpu.PrefetchScalarGridSpec(
            num_scalar_prefetch=1,
            grid=(nb, n_pages // pages),
            in_specs=in_specs,
            out_specs=out_spec,
            scratch_shapes=scratch),
        out_shape=jax.ShapeDtypeStruct((nb, H_F, D_F), BF16),
        compiler_params=_params(("parallel", "arbitrary")),
        name="fox_decode",
    )(page_table, q, k_new, v_new, bias, *([k_pool] * pages), *([v_pool] * pages))


def _merge_kernel(or_ref, of_ref, gr0_ref, gr1_ref, gf0_ref, gf1_ref, x_ref, eg_ref, eb_ref,
                  wr_ref, wf_ref, wo_ref, g1_ref, b1_ref, x1_ref, x1b_ref):
    tm = x_ref.shape[0]
    sub = min(tm, ROW_SUBTILE)
    for r in range(tm // sub):
        rows = slice(r * sub, (r + 1) * sub)
        y_r = jnp.dot(or_ref[rows, :], wr_ref[...], preferred_element_type=F32)
        y_f = jnp.dot(of_ref[rows, :], wf_ref[...], preferred_element_type=F32)
        halves = []
        for idx, (gr_ref, gf_ref) in enumerate(((gr0_ref, gf0_ref), (gr1_ref, gf1_ref))):
            sl = slice(idx * COL_BLOCK, (idx + 1) * COL_BLOCK)
            mixed = (_sigmoid(gr_ref[rows, :].astype(F32)) * y_r[:, sl]
                     + _sigmoid(gf_ref[rows, :].astype(F32)) * y_f[:, sl])
            halves.append(mixed.astype(BF16))
        mix = jnp.dot(jnp.concatenate(halves, axis=1), wo_ref[...], preferred_element_type=F32)
        h = _layer_norm(x_ref[rows, :], eg_ref[...], eb_ref[...])
        x1 = _layer_norm(ALPHA * h + mix, g1_ref[...], b1_ref[...])
        x1_ref[rows, :] = x1
        x1b_ref[rows, :] = x1.astype(BF16)


def _merge(o_r, o_f, z, x, eg, eb, w_r, w_f, w_o, g1, b1, tm):
    m = x.shape[0]
    row = lambda i: (i, 0)
    vec = pl.BlockSpec((1, D_MODEL), lambda i: (0, 0))
    zspec = lambda cb: pl.BlockSpec((tm, COL_BLOCK), lambda i: (i, cb))
    return pl.pallas_call(
        _merge_kernel,
        grid=(m // tm,),
        in_specs=[pl.BlockSpec((tm, W_R), row), pl.BlockSpec((tm, W_F), row),
                  zspec(CB_GR), zspec(CB_GR + 1), zspec(CB_GF), zspec(CB_GF + 1),
                  pl.BlockSpec((tm, D_MODEL), row), vec, vec,
                  _resident((W_R, D_MODEL)), _resident((W_F, D_MODEL)), _resident((D_MODEL, D_MODEL)),
                  vec, vec],
        out_specs=[pl.BlockSpec((tm, D_MODEL), row), pl.BlockSpec((tm, D_MODEL), row)],
        out_shape=[jax.ShapeDtypeStruct((m, D_MODEL), F32), jax.ShapeDtypeStruct((m, D_MODEL), BF16)],
        compiler_params=_params(("parallel",)),
        name="merge_out_ln1",
    )(o_r, o_f, z, z, z, z, x, eg, eb, w_r, w_f, w_o, g1, b1)


def _ffn_up(x_ref, w1_ref, w3_ref):
    xb = x_ref[...]
    a = jnp.dot(xb, w1_ref[...].astype(BF16), preferred_element_type=F32)
    b = jnp.dot(xb, w3_ref[...].astype(BF16), preferred_element_type=F32)
    return (a * _sigmoid(a) * b).astype(BF16)


def _ffn_down(mid, w2_ref, o_ref):
    o_ref[...] += jnp.dot(mid, w2_ref[...].astype(BF16), preferred_element_type=F32)


def _ffn_tile(x_ref, w1_ref, w3_ref, w2_ref, o_ref):
    _ffn_down(_ffn_up(x_ref, w1_ref, w3_ref), w2_ref, o_ref)


def _ffn_kernel(x_ref, w1_ref, w3_ref, w2_ref, o_ref):
    @pl.when(pl.program_id(1) == 0)
    def _():
        o_ref[...] = jnp.zeros_like(o_ref)

    _ffn_tile(x_ref, w1_ref, w3_ref, w2_ref, o_ref)


def _ffn_decode_kernel(pt_ref, x_ref, w1_ref, w3_ref, w2_ref, q_ref, kn_ref, vn_ref, bias_ref, *refs,
                       pages, n_groups):
    o_ref = refs[2 * pages]
    d = _DecodeRefs(q_ref, kn_ref, vn_ref, bias_ref, refs[:pages], refs[pages:2 * pages], *refs[2 * pages + 1:])
    j = pl.program_id(1)
    g = (pl.program_id(0) * pl.num_programs(1) + j) % n_groups

    @pl.when(j == 0)
    def _():
        o_ref[...] = jnp.zeros_like(o_ref)

    pl.when(g == 0)(lambda: _decode_begin(d))
    mid = _ffn_up(x_ref, w1_ref, w3_ref)
    s_parts = _decode_scores(d)
    _ffn_down(mid, w2_ref, o_ref)
    _decode_update(d, s_parts)
    pl.when(g == n_groups - 1)(lambda: _decode_end(d))


def _ffn(x1b, w1, w3, w2, tm, tf):
    m = x1b.shape[0]
    return pl.pallas_call(
        _ffn_kernel,
        grid=(m // tm, D_FF // tf),
        in_specs=[pl.BlockSpec((tm, D_MODEL), lambda i, j: (i, 0)),
                  pl.BlockSpec((D_MODEL, tf), lambda i, j: (0, j)),
                  pl.BlockSpec((D_MODEL, tf), lambda i, j: (0, j)),
                  pl.BlockSpec((tf, D_MODEL), lambda i, j: (j, 0))],
        out_specs=pl.BlockSpec((tm, D_MODEL), lambda i, j: (i, 0)),
        out_shape=jax.ShapeDtypeStruct((m, D_MODEL), F32),
        compiler_params=_params(("parallel", "arbitrary")),
        name="swiglu_ffn_rows",
    )(x1b, w1, w3, w2)


def _ffn_with_decode(x1b, w1, w3, w2, tm, tf, page_table, q, k_new, v_new, bias, k_pool, v_pool,
                     pages=DECODE_PAGES_PER_STEP):
    m = x1b.shape[0]
    nj = D_FF // tf
    n_groups = page_table.shape[1] // pages
    n_seq = (m // tm) * nj // n_groups
    assert n_seq * n_groups == (m // tm) * nj and n_seq <= page_table.shape[0]

    def seq_group(i, j):
        s = i * nj + j
        return s // n_groups, s % n_groups

    dec_in, dec_out, dec_scratch = _decode_specs(pages, seq_group)
    return pl.pallas_call(
        functools.partial(_ffn_decode_kernel, pages=pages, n_groups=n_groups),
        grid_spec=pltpu.PrefetchScalarGridSpec(
            num_scalar_prefetch=1,
            grid=(m // tm, nj),
            in_specs=[pl.BlockSpec((tm, D_MODEL), lambda i, j, pt: (i, 0), pipeline_mode=pl.Buffered(1)),
                      pl.BlockSpec((D_MODEL, tf), lambda i, j, pt: (0, j)),
                      pl.BlockSpec((D_MODEL, tf), lambda i, j, pt: (0, j)),
                      pl.BlockSpec((tf, D_MODEL), lambda i, j, pt: (j, 0))] + dec_in,
            out_specs=[pl.BlockSpec((tm, D_MODEL), lambda i, j, pt: (i, 0)), dec_out],
            scratch_shapes=dec_scratch),
        out_shape=[jax.ShapeDtypeStruct((m, D_MODEL), F32),
                   jax.ShapeDtypeStruct((n_seq, H_F, D_F), BF16)],
        compiler_params=_params(("arbitrary", "arbitrary")),
        name="swiglu_ffn_decode",
    )(page_table, x1b, w1, w3, w2, q, k_new, v_new, bias, *([k_pool] * pages), *([v_pool] * pages))


def _ple_kernel(x1_ref, x1b_ref, ffn_ref, p_ref, wpg_ref, wpe_ref, g2_ref, b2_ref, y_ref):
    tm = x1_ref.shape[0]
    sub = min(tm, ROW_SUBTILE)
    for r in range(tm // sub):
        rows = slice(r * sub, (r + 1) * sub)
        pg = jnp.dot(x1b_ref[rows, :], wpg_ref[...], preferred_element_type=F32)
        pe = jnp.dot(p_ref[rows, :].astype(BF16), wpe_ref[...], preferred_element_type=F32)
        y_ref[rows, :] = _layer_norm(ALPHA * x1_ref[rows, :] + ffn_ref[rows, :] + _sigmoid(pg) * pe,
                                     g2_ref[...], b2_ref[...])


def _ple(x1, x1b, ffn, p, w_pg, w_pe, g2, b2, tm):
    m = x1.shape[0]
    row = lambda i: (i, 0)
    vec = pl.BlockSpec((1, D_MODEL), lambda i: (0, 0))
    return pl.pallas_call(
        _ple_kernel,
        grid=(m // tm,),
        in_specs=[pl.BlockSpec((tm, D_MODEL), row), pl.BlockSpec((tm, D_MODEL), row),
                  pl.BlockSpec((tm, D_MODEL), row), pl.BlockSpec((tm, D_PLE), row),
                  _resident((D_MODEL, D_MODEL)), _resident((D_PLE, D_MODEL)), vec, vec],
        out_specs=pl.BlockSpec((tm, D_MODEL), row),
        out_shape=jax.ShapeDtypeStruct((m, D_MODEL), F32),
        compiler_params=_params(("parallel",)),
        name="ple_ln2",
    )(x1, x1b, ffn, p, w_pg, w_pe, g2, b2)


def _rope_tables(pos):
    half = DK_R // 2
    inv = ROPE_BASE ** (-jnp.arange(half, dtype=F32) / half)
    ang = pos.astype(F32)[:, None] * inv[None, :]
    return jnp.cos(ang), jnp.sin(ang)


def kernel(x_prompt, x_sample, state_ret, cache_k, cache_v, cache_logf, page_table, p_prompt, p_sample, ln_emb_g, ln_emb_b, w_in, b_f, gn_g, w_ret_proj, w_fox_proj, w_out, ln1_g, ln1_b, w1, w3, w2, w_pg, w_pe, ln2_g, ln2_b):
    assert w_in.shape[0] == DEPTH == 1
    batch, seq, _ = x_prompt.shape
    dec_b, dec_t, _ = x_sample.shape
    assert dec_t == 1
    n_pages = page_table.shape[1]
    past_len = n_pages * PAGE_SIZE
    n_pool = cache_k.shape[1]

    w_in_t = jnp.swapaxes(w_in[0], 0, 1)
    w_ff_t = w_in_t[FF_LO:FF_HI].astype(BF16)
    w_ff_pad = jnp.pad(w_ff_t, ((0, W_FF_PAD - H_F), (0, 0)))
    bf_col = b_f[0].reshape(H_F, 1)
    vec = lambda v: v.reshape(1, D_MODEL)
    eg, eb = vec(ln_emb_g), vec(ln_emb_b)
    g1, b1, g2, b2 = vec(ln1_g[0]), vec(ln1_b[0]), vec(ln2_g[0]), vec(ln2_b[0])
    w_r, w_f, w_o = w_ret_proj[0].astype(BF16), w_fox_proj[0].astype(BF16), w_out[0].astype(BF16)
    w_pgb, w_peb = w_pg[0].astype(BF16), w_pe[0].astype(BF16)
    log_g = jnp.log1p(-jnp.exp2(-5.0 - jnp.arange(H_R, dtype=F32)))
    cos_p, sin_p = _rope_tables(jnp.arange(seq))
    cos_s, sin_s = _rope_tables(jnp.full((dec_b,), past_len))

    xp = x_prompt.reshape(batch * seq, D_MODEL)
    xs = x_sample.reshape(dec_b, D_MODEL)

    h_p = _entry_ln(xp, eg, eb, tm=512)
    z_p, fk_p, fv_p, ff_p, z_s, fk_s, fv_s = _inproj(h_p, xs, eg, eb, w_in_t, w_ff_pad,
                                                     cos_p, sin_p, cos_s, sin_s, tm=IN_TM)
    lf_p, c2_row = _logf_prompt(ff_p, bf_col, seq)
    lf_s = _logf_sample(xs, eg, eb, w_ff_t, bf_col)

    o_r_p, st_p = _ret_prompt(z_p, log_g, gn_g[0], batch, seq)
    o_f_p = _fox_prompt(z_p, c2_row, batch, seq)

    z3 = z_s.reshape(dec_b, 1, z_s.shape[1])
    o_r_s, st_s = _ret_sample(z3, log_g, gn_g[0], state_ret[0])
    width = PAGE_SIZE * H_F
    lf_new = jnp.swapaxes(lf_s, 1, 2).reshape(dec_b, H_F)
    lnew = jnp.tile(lf_new, (1, PAGE_SIZE)).reshape(dec_b, 1, width)
    bias = _decode_bias(page_table, cache_logf[0].reshape(n_pool, width), lnew)
    heads = lambda cb: z_s[:, cb * COL_BLOCK:(cb + 1) * COL_BLOCK].reshape(dec_b, H_F, D_F)
    decode_args = (heads(CB_FQ), heads(CB_FK), heads(CB_FV), bias,
                   cache_k[0].reshape(n_pool, width, D_F), cache_v[0].reshape(n_pool, width, D_F))

    x1_p, x1b_p = _merge(o_r_p, o_f_p, z_p, xp, eg, eb, w_r, w_f, w_o, g1, b1, tm=MERGE_TM)
    ffn_p, o_f_ride = _ffn_with_decode(x1b_p, w1[0], w3[0], w2[0], FFN_TM, FFN_TF, page_table, *decode_args)
    n_ride = o_f_ride.shape[0]
    o_f_rest = _fox_decode(page_table[n_ride:], *(a[n_ride:] for a in decode_args[:4]), *decode_args[4:])
    o_f_s = jnp.concatenate([o_f_ride, o_f_rest], axis=0)
    x1_s, x1b_s = _merge(o_r_s.reshape(dec_b, W_R), o_f_s.reshape(dec_b, W_F), z_s, xs, eg, eb,
                         w_r, w_f, w_o, g1, b1, tm=dec_b)
    ffn_s = _ffn(x1b_s, w1[0], w3[0], w2[0], tm=dec_b, tf=FFN_TF)
    y_p = _ple(x1_p, x1b_p, ffn_p, p_prompt[0].reshape(batch * seq, D_PLE), w_pgb, w_peb, g2, b2, tm=PLE_TM)
    y_s = _ple(x1_s, x1b_s, ffn_s, p_sample[0].reshape(dec_b, D_PLE), w_pgb, w_peb, g2, b2, tm=dec_b)

    return (y_p.reshape(batch, seq, D_MODEL),
            y_s.reshape(dec_b, 1, D_MODEL),
            st_p[None],
            fk_p.reshape(1, batch, seq, H_F, D_F),
            fv_p.reshape(1, batch, seq, H_F, D_F),
            jnp.swapaxes(lf_p, 1, 2)[None],
            st_s[None],
            fk_s.reshape(1, dec_b, 1, H_F, D_F),
            fv_s.reshape(1, dec_b, 1, H_F, D_F),
            jnp.swapaxes(lf_s, 1, 2).reshape(1, dec_b, 1, H_F))
```

```python
import functools

import jax
import jax.numpy as jnp
import numpy as np
from jax import lax
from jax.experimental import pallas as pl
from jax.experimental.pallas import tpu as pltpu
from jax.experimental.pallas import tpu_sc as plsc

F32 = jnp.float32
BF16 = jnp.bfloat16

D_MODEL = 2048
DEPTH = 1
PAGE_SIZE = 128
H_R = 4
DK_R = 256
DV_R = 256
W_R = H_R * DV_R
CHUNK = 256
ROPE_BASE = 10000.0
GN_EPS = 1e-6
H_F = 8
D_F = 128
W_F = H_F * D_F
D_FF = 5632
D_PLE = 256
LN_EPS = 1e-5
ALPHA = (2.0 * DEPTH) ** 0.25
FOX_SCALE = D_F ** -0.5
LOG2E = float(np.log2(np.e))

COL_BLOCK = 1024
N_MAIN = 11 * COL_BLOCK
CB_RQ, CB_RK, CB_RV, CB_RG, CB_FQ, CB_FK, CB_FV, CB_GR, CB_GF = 0, 1, 2, 3, 4, 5, 6, 7, 9
FF_LO, FF_HI = 7 * COL_BLOCK, 7 * COL_BLOCK + H_F

IN_TM, IN_TN = 1024, 1024
W_FF_PAD = 128
N_A_BLOCKS = FF_LO // IN_TN
FFN_TM, FFN_TF = 1024, 256
MERGE_TM, PLE_TM, ROW_SUBTILE = 256, 512, 256

VMEM_LIMIT = 56 * 1024 * 1024
DECODE_PAGES_PER_STEP = 8
RET_SAMPLE_SEQS = 4

NT_DIMS = (((1,), (1,)), ((), ()))
TN_DIMS = (((0,), (0,)), ((), ()))


def _params(semantics):
    return pltpu.CompilerParams(dimension_semantics=semantics, vmem_limit_bytes=VMEM_LIMIT)


def _layer_norm(x, g, b):
    mu = jnp.mean(x, axis=-1, keepdims=True)
    xc = x - mu
    var = jnp.mean(xc * xc, axis=-1, keepdims=True)
    return xc * lax.rsqrt(var + LN_EPS) * g + b


def _sigmoid(x):
    return 1.0 / (1.0 + jnp.exp(-x))


def _log_sigmoid(x):
    return jnp.minimum(x, 0.0) - jnp.log(1.0 + jnp.exp(-jnp.abs(x)))


def _resident(shape):
    return pl.BlockSpec(shape, lambda *_: (0,) * len(shape), pipeline_mode=pl.Buffered(1))


def _ln_kernel(x_ref, g_ref, b_ref, o_ref):
    o_ref[...] = _layer_norm(x_ref[...], g_ref[...], b_ref[...]).astype(o_ref.dtype)


def _entry_ln(x, g, b, tm):
    m = x.shape[0]
    return pl.pallas_call(
        _ln_kernel,
        grid=(m // tm,),
        in_specs=[pl.BlockSpec((tm, D_MODEL), lambda i: (i, 0)),
                  pl.BlockSpec((1, D_MODEL), lambda i: (0, 0)),
                  pl.BlockSpec((1, D_MODEL), lambda i: (0, 0))],
        out_specs=pl.BlockSpec((tm, D_MODEL), lambda i: (i, 0)),
        out_shape=jax.ShapeDtypeStruct((m, D_MODEL), BF16),
        compiler_params=_params(("parallel",)),
        name="entry_ln",
    )(x, g, b)


def _inproj_block(h, w_ref, z_ref, rope=None, f32_ref=None):
    half = DK_R // 2
    for hd in range(IN_TN // DK_R):
        lo = hd * DK_R
        acc = lax.dot_general(h, w_ref[lo:lo + DK_R, :].astype(BF16), NT_DIMS, preferred_element_type=F32)
        if rope is None:
            z_ref[:, lo:lo + DK_R] = acc.astype(BF16)
        else:
            cos, sin, scale = rope
            x1 = acc[:, :half]
            x2 = acc[:, half:]
            z_ref[:, lo:lo + half] = ((x1 * cos - x2 * sin) * scale).astype(BF16)
            z_ref[:, lo + half:lo + DK_R] = ((x2 * cos + x1 * sin) * scale).astype(BF16)
        if f32_ref is not None:
            f32_ref[:, lo:lo + DK_R] = acc


def _inproj_kernel(h_ref, xs_ref, eg_ref, eb_ref, w_ref, wff_ref, cos_ref, sin_ref, coss_ref, sins_ref,
                   z_ref, fk_ref, fv_ref, ff_ref, zs_ref, fks_ref, fvs_ref):
    i = pl.program_id(0)
    j = pl.program_id(1)
    per = COL_BLOCK // IN_TN

    @pl.when(j == 0)
    def _():
        ff_ref[...] = lax.dot_general(h_ref[...], wff_ref[...], NT_DIMS, preferred_element_type=F32)

    def section(lo, hi, rope_scale=None, f32_refs=(None, None)):
        @pl.when((j >= lo) & (j < hi))
        def _():
            rope = None if rope_scale is None else (cos_ref[...], sin_ref[...], rope_scale)
            _inproj_block(h_ref[...], w_ref, z_ref, rope, f32_refs[0])

            @pl.when(i == 0)
            def _():
                hs = _layer_norm(xs_ref[...], eg_ref[...], eb_ref[...]).astype(BF16)
                rope_s = None if rope_scale is None else (coss_ref[...], sins_ref[...], rope_scale)
                _inproj_block(hs, w_ref, zs_ref, rope_s, f32_refs[1])

    @pl.when(i > 0)
    def _():
        zs_ref[...] = jnp.zeros_like(zs_ref)

    section(CB_RQ * per, CB_RK * per, rope_scale=1.0)
    section(CB_RK * per, CB_RV * per, rope_scale=DK_R ** -0.5)
    section(CB_RV * per, CB_FK * per)
    section(CB_FK * per, CB_FV * per, f32_refs=(fk_ref, fks_ref))
    section(CB_FV * per, N_A_BLOCKS, f32_refs=(fv_ref, fvs_ref))
    section(N_A_BLOCKS, N_MAIN // IN_TN)


def _inproj(h, xs, eg, eb, w_in_t, w_ff_pad, cos, sin, cos_s, sin_s, tm):
    m = h.shape[0]
    ms = xs.shape[0]
    n_pos_blocks = cos.shape[0] // tm
    nj = N_MAIN // IN_TN
    per = COL_BLOCK // IN_TN
    half = DK_R // 2
    clamp = lambda j, cb: jnp.clip(j - cb * per, 0, per - 1)
    row = lambda i, j: (i, 0)
    fixed = lambda i, j: (0, 0)
    return pl.pallas_call(
        _inproj_kernel,
        grid=(m // tm, nj),
        in_specs=[pl.BlockSpec((tm, D_MODEL), row),
                  pl.BlockSpec((ms, D_MODEL), fixed),
                  pl.BlockSpec((1, D_MODEL), fixed),
                  pl.BlockSpec((1, D_MODEL), fixed),
                  pl.BlockSpec((pl.Element(IN_TN), pl.Element(D_MODEL)),
                               lambda i, j: (pl.multiple_of(IN_TN * j + jnp.where(j < N_A_BLOCKS, 0, H_F), 8), 0)),
                  pl.BlockSpec((W_FF_PAD, D_MODEL), fixed),
                  pl.BlockSpec((tm, half), lambda i, j: (i % n_pos_blocks, 0)),
                  pl.BlockSpec((tm, half), lambda i, j: (i % n_pos_blocks, 0)),
                  pl.BlockSpec((ms, half), fixed),
                  pl.BlockSpec((ms, half), fixed)],
        out_specs=[pl.BlockSpec((tm, IN_TN), lambda i, j: (i, j)),
                   pl.BlockSpec((tm, IN_TN), lambda i, j: (i, clamp(j, CB_FK))),
                   pl.BlockSpec((tm, IN_TN), lambda i, j: (i, clamp(j, CB_FV))),
                   pl.BlockSpec((tm, W_FF_PAD), row),
                   pl.BlockSpec((ms, IN_TN), lambda i, j: (0, jnp.where(i == 0, j, nj))),
                   pl.BlockSpec((ms, IN_TN), lambda i, j: (0, jnp.where(i == 0, clamp(j, CB_FK), per - 1))),
                   pl.BlockSpec((ms, IN_TN), lambda i, j: (0, jnp.where(i == 0, clamp(j, CB_FV), per - 1)))],
        out_shape=[jax.ShapeDtypeStruct((m, N_MAIN), BF16),
                   jax.ShapeDtypeStruct((m, W_F), F32),
                   jax.ShapeDtypeStruct((m, W_F), F32),
                   jax.ShapeDtypeStruct((m, W_FF_PAD), F32),
                   jax.ShapeDtypeStruct((ms, N_MAIN + IN_TN), BF16),
                   jax.ShapeDtypeStruct((ms, W_F), F32),
                   jax.ShapeDtypeStruct((ms, W_F), F32)],
        compiler_params=_params(("arbitrary", "arbitrary")),
        name="in_proj",
    )(h, xs, eg, eb, w_in_t, w_ff_pad, cos, sin, cos_s, sin_s)


def _cumsum_lanes(x):
    n = x.shape[-1]
    lane = lax.broadcasted_iota(jnp.int32, x.shape, x.ndim - 1)
    s = 1
    while s < n:
        x = x + jnp.where(lane >= s, pltpu.roll(x, s, axis=x.ndim - 1), 0.0)
        s *= 2
    return x


def _logf_prompt_kernel(ff_ref, bf_ref, lf_ref, c_ref):
    ff = ff_ref[...].T[:H_F, :]
    lf = _log_sigmoid(ff + bf_ref[...])
    lf_ref[0] = lf
    c_ref[0] = _cumsum_lanes(lf) * LOG2E


def _logf_sample_kernel(x_ref, eg_ref, eb_ref, w_ref, bf_ref, lf_ref):
    h = _layer_norm(x_ref[...], eg_ref[...], eb_ref[...]).astype(BF16)
    ff = lax.dot_general(w_ref[...], h, NT_DIMS, preferred_element_type=F32)
    lf_ref[0] = _log_sigmoid(ff + bf_ref[...])


def _logf_prompt(ff, bf_col, seq):
    nb = ff.shape[0] // seq
    shape = jax.ShapeDtypeStruct((nb, H_F, seq), F32)
    spec = pl.BlockSpec((1, H_F, seq), lambda b: (b, 0, 0))
    return pl.pallas_call(
        _logf_prompt_kernel,
        grid=(nb,),
        in_specs=[pl.BlockSpec((seq, W_FF_PAD), lambda b: (b, 0)),
                  pl.BlockSpec((H_F, 1), lambda b: (0, 0))],
        out_specs=[spec, spec],
        out_shape=[shape, shape],
        compiler_params=_params(("parallel",)),
        name="log_forget_cumsum",
    )(ff, bf_col)


def _logf_sample(xs, eg, eb, w_ff_t, bf_col):
    ms = xs.shape[0]
    fixed = lambda b: (0, 0)
    return pl.pallas_call(
        _logf_sample_kernel,
        grid=(1,),
        in_specs=[pl.BlockSpec((ms, D_MODEL), fixed), pl.BlockSpec((1, D_MODEL), fixed),
                  pl.BlockSpec((1, D_MODEL), fixed), pl.BlockSpec((H_F, D_MODEL), fixed),
                  pl.BlockSpec((H_F, 1), fixed)],
        out_specs=pl.BlockSpec((1, H_F, ms), lambda b: (0, 0, 0)),
        out_shape=jax.ShapeDtypeStruct((1, H_F, ms), F32),
        compiler_params=_params(("arbitrary",)),
        name="log_forget",
    )(xs, eg, eb, w_ff_t, bf_col)


def _group_norm_gate(o, gate_bf16, gn_row):
    mu = jnp.mean(o, axis=-1, keepdims=True)
    oc = o - mu
    var = jnp.mean(oc * oc, axis=-1, keepdims=True)
    gate = gate_bf16.astype(F32)
    return oc * lax.rsqrt(var + GN_EPS) * gn_row * (gate * _sigmoid(gate))


def _ret_prompt_kernel(lg_ref, q_ref, k_ref, v_ref, g_ref, gn_ref, o_ref, st_ref,
                       dec_ref, cross_ref, kdec_ref, carry_ref):
    b = pl.program_id(0)
    c = pl.program_id(1)
    n = q_ref.shape[0]

    @pl.when((b == 0) & (c == 0))
    def _():
        ri = lax.broadcasted_iota(jnp.int32, (n, n), 0)
        ci = lax.broadcasted_iota(jnp.int32, (n, n), 1)
        causal = ri >= ci
        diff = jnp.where(causal, (ri - ci).astype(F32), 0.0)
        pos = lax.broadcasted_iota(jnp.int32, (n, DK_R), 0).astype(F32)
        for hd in range(H_R):
            lg = lg_ref[hd]
            dec_ref[hd] = jnp.where(causal, jnp.exp(lg * diff), 0.0)
            cross_ref[hd] = jnp.exp(lg * (pos + 1.0))
            kdec_ref[hd] = jnp.exp(lg * (n - 1.0 - pos))
            carry_ref[hd] = jnp.exp(jnp.full((1, DV_R), lg * n, F32))

    @pl.when(c == 0)
    def _():
        st_ref[...] = jnp.zeros_like(st_ref)

    heads = lambda ref: jnp.stack([ref[:, hd * DK_R:(hd + 1) * DK_R] for hd in range(H_R)], axis=0)
    bmm = lambda a, b_, ca, cb: lax.dot_general(a, b_, (((ca,), (cb,)), ((0,), (0,))), preferred_element_type=F32)
    q3, k3, v3 = heads(q_ref), heads(k_ref), heads(v_ref)
    sc = bmm(q3, k3, 2, 2) * dec_ref[...]
    o = bmm(sc.astype(BF16), v3, 2, 1)
    state = st_ref[0]
    o = o + bmm(q3, state.astype(BF16), 2, 1) * cross_ref[...]
    kd = (k3.astype(F32) * kdec_ref[...]).astype(BF16)
    st_ref[0] = carry_ref[...] * state + bmm(kd, v3, 1, 1)
    out = _group_norm_gate(o, heads(g_ref), gn_ref[...][:, None, :]).astype(BF16)
    for hd in range(H_R):
        o_ref[:, hd * DV_R:(hd + 1) * DV_R] = out[hd]


def _ret_prompt(z, log_g, gn_g, batch, seq):
    nc = seq // CHUNK
    row = lambda b, c: b * nc + c
    zspec = lambda cb: pl.BlockSpec((CHUNK, COL_BLOCK), lambda b, c: (row(b, c), cb))
    return pl.pallas_call(
        _ret_prompt_kernel,
        grid=(batch, nc),
        in_specs=[pl.BlockSpec(memory_space=pltpu.SMEM),
                  zspec(CB_RQ), zspec(CB_RK), zspec(CB_RV), zspec(CB_RG),
                  pl.BlockSpec((H_R, DV_R), lambda b, c: (0, 0))],
        out_specs=[pl.BlockSpec((CHUNK, W_R), lambda b, c: (row(b, c), 0)),
                   pl.BlockSpec((1, H_R, DK_R, DV_R), lambda b, c: (b, 0, 0, 0))],
        out_shape=[jax.ShapeDtypeStruct((batch * seq, W_R), BF16),
                   jax.ShapeDtypeStruct((batch, H_R, DK_R, DV_R), F32)],
        scratch_shapes=[pltpu.VMEM((H_R, CHUNK, CHUNK), F32),
                        pltpu.VMEM((H_R, CHUNK, DV_R), F32),
                        pltpu.VMEM((H_R, CHUNK, DK_R), F32),
                        pltpu.VMEM((H_R, 1, DV_R), F32)],
        compiler_params=_params(("arbitrary", "arbitrary")),
        name="retention_prompt",
    )(log_g, z, z, z, z, gn_g)


def _ret_sample_kernel(lg_ref, q_ref, k_ref, v_ref, g_ref, gn_ref, st_ref, o_ref, ns_ref):
    row0 = lax.broadcasted_iota(jnp.int32, (8, DK_R), 0) == 0
    for sq in range(q_ref.shape[0]):
        for hd in range(H_R):
            lg = lg_ref[hd]
            sl = slice(hd * DK_R, (hd + 1) * DK_R)
            q = q_ref[sq, :, sl]
            k = k_ref[sq, :, sl]
            v = v_ref[sq, :, sl]
            qf = q.astype(F32)
            kf = k.astype(F32)
            vf = v.astype(F32)
            state = st_ref[sq, hd]
            decay = jnp.exp(jnp.full((1, DV_R), lg, F32))
            score = jnp.sum(qf * kf, axis=-1, keepdims=True)
            q8 = jnp.broadcast_to(q, (8, DK_R))
            qs = jnp.dot(q8, state.astype(BF16), preferred_element_type=F32)[0:1, :]
            o = score * vf + qs * decay
            k8 = jnp.where(row0, jnp.broadcast_to(kf, (8, DK_R)), 0.0).astype(BF16)
            v8 = jnp.broadcast_to(v, (8, DV_R))
            ns_ref[sq, hd] = decay * state + lax.dot_general(k8, v8, TN_DIMS, preferred_element_type=F32)
            o_ref[sq, :, sl] = _group_norm_gate(o, g_ref[sq, :, sl], gn_ref[hd:hd + 1, :]).astype(BF16)


def _ret_sample(z3, log_g, gn_g, state):
    nb = z3.shape[0]
    per = RET_SAMPLE_SEQS
    zspec = lambda cb: pl.BlockSpec((per, 1, COL_BLOCK), lambda b: (b, 0, cb))
    st_spec = pl.BlockSpec((per, H_R, DK_R, DV_R), lambda b: (b, 0, 0, 0))
    return pl.pallas_call(
        _ret_sample_kernel,
        grid=(nb // per,),
        in_specs=[pl.BlockSpec(memory_space=pltpu.SMEM),
                  zspec(CB_RQ), zspec(CB_RK), zspec(CB_RV), zspec(CB_RG),
                  pl.BlockSpec((H_R, DV_R), lambda b: (0, 0)),
                  st_spec],
        out_specs=[pl.BlockSpec((per, 1, W_R), lambda b: (b, 0, 0)), st_spec],
        out_shape=[jax.ShapeDtypeStruct((nb, 1, W_R), BF16),
                   jax.ShapeDtypeStruct(state.shape, F32)],
        compiler_params=_params(("parallel",)),
        name="retention_sample",
    )(log_g, z3, z3, z3, z3, gn_g, state)


def _split3(x):
    hi = x.astype(BF16).astype(F32)
    r1 = x - hi
    mid = r1.astype(BF16).astype(F32)
    return hi, mid, r1 - mid


def _fox_prompt_kernel(q_ref, k_ref, v_ref, crow_ref, o_ref, vt_ref, kx_ref, m_ref, l_ref, acc_ref,
                       *, tq):
    qi = pl.program_id(1)
    nk = vt_ref.shape[1]
    score_scale = FOX_SCALE * LOG2E

    @pl.when(qi == 0)
    def _():
        sel_r = lax.broadcasted_iota(jnp.int32, (8, D_F), 0)
        sel_l = lax.broadcasted_iota(jnp.int32, (8, D_F), 1)
        sel = jnp.where((sel_r == sel_l) & (sel_r < 3), 1.0, 0.0).astype(BF16)
        for hd in range(H_F):
            for kj in range(nk):
                v_tile = v_ref[kj * tq:(kj + 1) * tq, hd * D_F:(hd + 1) * D_F]
                vt_ref[hd, kj] = v_tile.astype(F32).T.astype(BF16)
                parts = _split3(crow_ref[0, hd, kj:kj + 1, :] * (-1.0 / score_scale))
                rows = jnp.concatenate(list(parts) + [jnp.zeros((5, tq), F32)], axis=0).astype(BF16)
                kx_ref[hd, kj] = lax.dot_general(rows, sel, TN_DIMS, preferred_element_type=F32).astype(BF16)

    m_ref[...] = jnp.full(m_ref.shape, -jnp.inf, F32)
    l_ref[...] = jnp.zeros_like(l_ref)
    acc_ref[...] = jnp.zeros_like(acc_ref)
    key_i = lax.broadcasted_iota(jnp.int32, (tq, tq), 0)
    qry_i = lax.broadcasted_iota(jnp.int32, (tq, tq), 1)
    visible = key_i <= qry_i

    heads = lambda ref, rows: jnp.stack([ref[rows, hd * D_F:(hd + 1) * D_F] for hd in range(H_F)], axis=0)
    ones3 = jnp.where(lax.broadcasted_iota(jnp.int32, (H_F, tq, D_F), 2) < 3, 1.0, 0.0).astype(BF16)
    q3 = jnp.concatenate([heads(q_ref, slice(None)), ones3], axis=2)
    c_q = crow_ref[0, :, pl.ds(qi, 1), :]

    def tile(kj, on_diagonal):
        ks = pl.multiple_of(kj * tq, tq)
        k3 = jnp.concatenate([heads(k_ref, pl.ds(ks, tq)), kx_ref[:, kj]], axis=2)
        s = lax.dot_general(k3, q3, (((2,), (2,)), ((0,), (0,))), preferred_element_type=F32)
        t = s * score_scale + c_q
        if on_diagonal:
            t = jnp.where(visible[None], t, -jnp.inf)
        m_old = m_ref[...]
        m_new = jnp.maximum(m_old, jnp.max(t, axis=1, keepdims=True))
        alpha = jnp.exp2(m_old - m_new)
        p = jnp.exp2(t - m_new)
        l_ref[...] = alpha * l_ref[...] + jnp.sum(p, axis=1, keepdims=True)
        m_ref[...] = m_new
        pv = lax.dot_general(vt_ref[:, kj], p.astype(BF16), (((2,), (1,)), ((0,), (0,))),
                             preferred_element_type=F32)
        acc_ref[...] = alpha * acc_ref[...] + pv

    def body(kj, carry):
        tile(kj, False)
        return carry

    lax.fori_loop(0, qi, body, 0)
    tile(qi, True)
    for hd in range(H_F):
        o_ref[:, hd * D_F:(hd + 1) * D_F] = (acc_ref[hd] / l_ref[hd]).T.astype(BF16)


def _fox_prompt(z, c2_row, batch, seq, tq=256):
    nq = seq // tq
    return pl.pallas_call(
        functools.partial(_fox_prompt_kernel, tq=tq),
        grid=(batch, nq),
        in_specs=[pl.BlockSpec((tq, COL_BLOCK), lambda b, i: (b * nq + i, CB_FQ)),
                  pl.BlockSpec((seq, COL_BLOCK), lambda b, i: (b, CB_FK)),
                  pl.BlockSpec((seq, COL_BLOCK), lambda b, i: (b, CB_FV)),
                  pl.BlockSpec((1, H_F, nq, tq), lambda b, i: (b, 0, 0, 0))],
        out_specs=pl.BlockSpec((tq, W_F), lambda b, i: (b * nq + i, 0)),
        out_shape=jax.ShapeDtypeStruct((batch * seq, W_F), BF16),
        scratch_shapes=[pltpu.VMEM((H_F, nq, D_F, tq), BF16),
                        pltpu.VMEM((H_F, nq, tq, D_F), BF16),
                        pltpu.VMEM((H_F, 1, tq), F32),
                        pltpu.VMEM((H_F, 1, tq), F32),
                        pltpu.VMEM((H_F, D_F, tq), F32)],
        compiler_params=_params(("parallel", "arbitrary")),
        name="fox_prompt",
    )(z, z, z, c2_row.reshape(batch, H_F, nq, tq))


def _decode_bias_kernel(pt_ref, pool_ref, lnew_ref, bias_ref, rows_ref, *, n_pages):
    b = pl.program_id(0)
    width = PAGE_SIZE * H_F
    for p in range(n_pages):
        rows_ref[p:p + 1, :] = pool_ref[pl.ds(pt_ref[b, p], 1), :]
    x = rows_ref[...]
    lane = lax.broadcasted_iota(jnp.int32, x.shape, 1)
    within = x
    total = x
    s = H_F
    while s < width:
        within = within + jnp.where(lane >= s, pltpu.roll(within, s, axis=1), 0.0)
        total = total + pltpu.roll(total, s, axis=1)
        s *= 2
    rest = total - within
    run = lnew_ref[0]
    for p in reversed(range(n_pages)):
        bias_ref[0, p:p + 1, :] = run + rest[p:p + 1, :]
        run = run + total[p:p + 1, :]


def _decode_bias(page_table, logf_pool, lnew):
    nb, n_pages = page_table.shape
    width = PAGE_SIZE * H_F
    return pl.pallas_call(
        functools.partial(_decode_bias_kernel, n_pages=n_pages),
        grid_spec=pltpu.PrefetchScalarGridSpec(
            num_scalar_prefetch=1,
            grid=(nb,),
            in_specs=[pl.BlockSpec(logf_pool.shape, lambda b, pt: (0, 0), pipeline_mode=pl.Buffered(1)),
                      pl.BlockSpec((1, 1, width), lambda b, pt: (b, 0, 0))],
            out_specs=pl.BlockSpec((1, n_pages, width), lambda b, pt: (b, 0, 0)),
            scratch_shapes=[pltpu.VMEM((n_pages, width), F32)]),
        out_shape=jax.ShapeDtypeStruct((nb, n_pages, width), F32),
        compiler_params=_params(("arbitrary",)),
        name="decode_forget_bias",
    )(page_table, logf_pool, lnew)


class _DecodeRefs:
    def __init__(self, q_ref, kn_ref, vn_ref, bias_ref, k_refs, v_refs, o_ref, m_ref, l_ref, acc_ref):
        self.q, self.kn, self.vn, self.bias = q_ref, kn_ref, vn_ref, bias_ref
        self.k, self.v = k_refs, v_refs
        self.o, self.m, self.l, self.acc = o_ref, m_ref, l_ref, acc_ref


def _decode_begin(d):
    q = d.q[0].astype(F32)
    s_new = jnp.sum(q * d.kn[0].astype(F32), axis=-1, keepdims=True) * FOX_SCALE
    d.m[...] = jnp.broadcast_to(s_new, d.m.shape)
    d.l[...] = jnp.ones_like(d.l)
    d.acc[...] = d.vn[0].astype(F32)


def _decode_scores(d):
    width = PAGE_SIZE * H_F
    row = lax.broadcasted_iota(jnp.int32, (H_F, width), 0)
    lane = lax.broadcasted_iota(jnp.int32, (H_F, width), 1)
    own = (lane % H_F) == row
    q = d.q[0]
    s_parts = []
    for i, k_ref in enumerate(d.k):
        kt = k_ref[...].astype(BF16)
        s = lax.dot_general(q, kt, NT_DIMS, preferred_element_type=F32)
        s_parts.append(jnp.where(own, s * FOX_SCALE + d.bias[0, i:i + 1, :], -jnp.inf))
    return s_parts


def _decode_update(d, s_parts):
    m_old = d.m[:, 0:1]
    m_new = m_old
    for s in s_parts:
        m_new = jnp.maximum(m_new, jnp.max(s, axis=-1, keepdims=True))
    alpha = jnp.exp(m_old - m_new)
    l_new = alpha * d.l[:, 0:1]
    pv = jnp.zeros((H_F, D_F), F32)
    for s, v_ref in zip(s_parts, d.v):
        p = jnp.exp(s - m_new)
        l_new = l_new + jnp.sum(p, axis=-1, keepdims=True)
        pv = pv + jnp.dot(p.astype(BF16), v_ref[...].astype(BF16), preferred_element_type=F32)
    d.m[...] = jnp.broadcast_to(m_new, d.m.shape)
    d.l[...] = jnp.broadcast_to(l_new, d.l.shape)
    d.acc[...] = alpha * d.acc[...] + pv


def _decode_end(d):
    d.o[0] = (d.acc[...] / d.l[:, 0:1]).astype(BF16)


def _fox_decode_kernel(pt_ref, q_ref, kn_ref, vn_ref, bias_ref, *refs, pages):
    d = _DecodeRefs(q_ref, kn_ref, vn_ref, bias_ref, refs[:pages], refs[pages:2 * pages], *refs[2 * pages:])
    g = pl.program_id(1)
    pl.when(g == 0)(lambda: _decode_begin(d))
    _decode_update(d, _decode_scores(d))
    pl.when(g == pl.num_programs(1) - 1)(lambda: _decode_end(d))


def _decode_specs(pages, seq_group):
    width = PAGE_SIZE * H_F

    def head_map(*args):
        return (seq_group(*args[:-1])[0], 0, 0)

    def bias_map(*args):
        b, g = seq_group(*args[:-1])
        return (b, g, 0)

    def page_spec(i):
        def page_map(*args):
            b, g = seq_group(*args[:-1])
            return (args[-1][b, g * pages + i], 0, 0)
        return pl.BlockSpec((None, width, D_F), page_map)

    head_spec = pl.BlockSpec((1, H_F, D_F), head_map)
    in_specs = ([head_spec, head_spec, head_spec, pl.BlockSpec((1, pages, width), bias_map)]
                + [page_spec(i) for i in range(pages)] * 2)
    scratch = [pltpu.VMEM((H_F, D_F), F32), pltpu.VMEM((H_F, D_F), F32), pltpu.VMEM((H_F, D_F), F32)]
    return in_specs, head_spec, scratch


def _fox_decode(page_table, q, k_new, v_new, bias, k_pool, v_pool, pages=DECODE_PAGES_PER_STEP):
    nb, n_pages = page_table.shape
    in_specs, out_spec, scratch = _decode_specs(pages, lambda b, g: (b, g))
    return pl.pallas_call(
        functools.partial(_fox_decode_kernel, pages=pages),
        grid_spec=pltpu.PrefetchScalarGridSpec(
            num_scalar_prefetch=1,
            grid=(nb, n_pages // pages),
            in_specs=in_specs,
            out_specs=out_spec,
            scratch_shapes=scratch),
        out_shape=jax.ShapeDtypeStruct((nb, H_F, D_F), BF16),
        compiler_params=_params(("parallel", "arbitrary")),
        name="fox_decode",
    )(page_table, q, k_new, v_new, bias, *([k_pool] * pages), *([v_pool] * pages))


SC_LANES = 16
SC_PARTS = 3
SC_CHUNK_TOKENS = 16
SC_NEG = -1e30


def _sc_decode_body(pt_hbm, q_hbm, b_hbm, k_hbm, v_hbm, acc_hbm, m_hbm, l_hbm,
                    pt_v, q_v, k_v, v_v, b_v, u_v, acc_v, m_v, l_v, *, n_workers, n_pages):
    ln = SC_LANES
    chunks_per_page = PAGE_SIZE // SC_CHUNK_TOKENS
    rows = SC_CHUNK_TOKENS * H_F
    w = lax.axis_index("s") * 2 + lax.axis_index("c")

    @pl.when(w < n_workers)
    def _():
        seq = lax.shift_right_logical(w * 11, 5)
        part = w - seq * SC_PARTS
        base_pages = n_pages // SC_PARTS
        p_lo = part * base_pages + jnp.minimum(part, n_pages % SC_PARTS)
        n_p = base_pages + jnp.where(part < n_pages % SC_PARTS, 1, 0)
        pltpu.sync_copy(pt_hbm.at[seq], pt_v)
        pltpu.sync_copy(q_hbm.at[seq], q_v)
        lane = lax.iota(jnp.int32, ln)

        @pl.loop(0, W_F, step=ln)
        def _(c0):
            acc_v[pl.ds(c0, ln)] = jnp.zeros((ln,), F32)

        @pl.loop(0, H_F * ln, step=ln)
        def _(c0):
            m_v[pl.ds(c0, ln)] = jnp.full((ln,), SC_NEG, F32)
            l_v[pl.ds(c0, ln)] = jnp.zeros((ln,), F32)

        def chunk(ci, carry):
            pg_i = p_lo + lax.shift_right_logical(ci, 3)
            sub = ci & (chunks_per_page - 1)
            base16 = lax.shift_right_logical(pg_i, 4) * ln
            page = jnp.max(jnp.where(lane == pg_i - base16, pt_v[pl.ds(base16, ln)], 0))
            row0 = pl.multiple_of(page * (PAGE_SIZE * H_F) + sub * rows, rows)
            pltpu.sync_copy(k_hbm.at[pl.ds(row0, rows)], k_v)
            pltpu.sync_copy(v_hbm.at[pl.ds(row0, rows)], v_v)
            pltpu.sync_copy(b_hbm.at[(seq * n_pages + pg_i) * chunks_per_page + sub], b_v)

            @pl.loop(0, H_F)
            def _(h):
                qc = [q_v[pl.ds(h * D_F + j * ln, ln)] for j in range(D_F // ln)]
                for t in range(SC_CHUNK_TOKENS):
                    r = t * H_F + h
                    u = qc[0] * k_v[r, pl.ds(0, ln)]
                    for j in range(1, D_F // ln):
                        u = u + qc[j] * k_v[r, pl.ds(j * ln, ln)]
                    u_v[t, :] = u
                s = plsc.load_gather(u_v, [lane, jnp.zeros((ln,), jnp.int32)])
                for j in range(1, ln):
                    s = s + plsc.load_gather(u_v, [lane, jnp.full((ln,), j, jnp.int32)])
                s = s * FOX_SCALE + plsc.load_gather(b_v, [lane * H_F + h])
                m_old = m_v[pl.ds(h * ln, ln)]
                m_new = jnp.maximum(m_old, jnp.full((ln,), jnp.max(s), F32))
                alpha = jnp.exp(m_old - m_new)
                p = jnp.exp(s - m_new)
                l_v[pl.ds(h * ln, ln)] = l_v[pl.ds(h * ln, ln)] * alpha + jnp.full((ln,), jnp.sum(p), F32)
                m_v[pl.ds(h * ln, ln)] = m_new
                acc = [acc_v[pl.ds(h * D_F + j * ln, ln)] * alpha for j in range(D_F // ln)]
                for t in range(SC_CHUNK_TOKENS):
                    pt_splat = p.at[jnp.full((ln,), t, jnp.int32)].get(mode="promise_in_bounds")
                    r = t * H_F + h
                    for j in range(D_F // ln):
                        acc[j] = acc[j] + pt_splat * v_v[r, pl.ds(j * ln, ln)]
                for j in range(D_F // ln):
                    acc_v[pl.ds(h * D_F + j * ln, ln)] = acc[j]

            return carry

        lax.fori_loop(0, n_p * chunks_per_page, chunk, 0)
        pltpu.sync_copy(acc_v, acc_hbm.at[w])
        pltpu.sync_copy(m_v, m_hbm.at[w])
        pltpu.sync_copy(l_v, l_hbm.at[w])


def _sc_decode_partials(page_table, q, bias, k_pool, v_pool):
    n_seq, n_pages = page_table.shape
    n_workers = n_seq * SC_PARTS
    assert n_workers <= 32
    rows = SC_CHUNK_TOKENS * H_F
    mesh = plsc.VectorSubcoreMesh(core_axis_name="c", subcore_axis_name="s")
    f32 = lambda *shape: jax.ShapeDtypeStruct(shape, F32)
    run = pl.kernel(
        functools.partial(_sc_decode_body, n_workers=n_workers, n_pages=n_pages),
        out_type=[f32(n_workers, W_F), f32(n_workers, H_F * SC_LANES), f32(n_workers, H_F * SC_LANES)],
        mesh=mesh,
        scratch_types=[pltpu.VMEM((n_pages,), jnp.int32), pltpu.VMEM((W_F,), F32),
                       pltpu.VMEM((rows, D_F), F32), pltpu.VMEM((rows, D_F), F32),
                       pltpu.VMEM((rows,), F32), pltpu.VMEM((SC_LANES, SC_LANES), F32),
                       pltpu.VMEM((W_F,), F32),
                       pltpu.VMEM((H_F * SC_LANES,), F32), pltpu.VMEM((H_F * SC_LANES,), F32)],
        compiler_params=pltpu.CompilerParams(needs_layout_passes=False),
        name="fox_decode_sparsecore",
    )
    return run(page_table, q, bias.reshape(-1, rows), k_pool.reshape(-1, D_F), v_pool.reshape(-1, D_F))


def _decode_combine_kernel(q_ref, kn_ref, vn_ref, acc_ref, m_ref, l_ref, o_ref):
    q = q_ref[0].astype(F32)
    s_new = jnp.sum(q * kn_ref[0].astype(F32), axis=-1, keepdims=True) * FOX_SCALE
    m_i = m_ref[0][:, :, 0:1]
    l_i = l_ref[0][:, :, 0:1]
    m_all = jnp.maximum(s_new, jnp.max(m_i, axis=0))
    w_i = jnp.exp(m_i - m_all[None])
    w_new = jnp.exp(s_new - m_all)
    num = jnp.sum(w_i * acc_ref[0], axis=0) + w_new * vn_ref[0].astype(F32)
    den = jnp.sum(w_i * l_i, axis=0) + w_new
    o_ref[0] = (num / den).astype(BF16)


def _decode_combine(q, k_new, v_new, acc, m, l):
    n_seq = q.shape[0]
    head_spec = pl.BlockSpec((1, H_F, D_F), lambda b: (b, 0, 0))
    stat_spec = pl.BlockSpec((1, SC_PARTS, H_F, SC_LANES), lambda b: (b, 0, 0, 0))
    return pl.pallas_call(
        _decode_combine_kernel,
        grid=(n_seq,),
        in_specs=[head_spec, head_spec, head_spec,
                  pl.BlockSpec((1, SC_PARTS, H_F, D_F), lambda b: (b, 0, 0, 0)), stat_spec, stat_spec],
        out_specs=head_spec,
        out_shape=jax.ShapeDtypeStruct((n_seq, H_F, D_F), BF16),
        compiler_params=_params(("parallel",)),
        name="fox_decode_combine",
    )(q, k_new, v_new, acc.reshape(n_seq, SC_PARTS, H_F, D_F),
      m.reshape(n_seq, SC_PARTS, H_F, SC_LANES), l.reshape(n_seq, SC_PARTS, H_F, SC_LANES))


def _merge_kernel(or_ref, of_ref, gr0_ref, gr1_ref, gf0_ref, gf1_ref, x_ref, eg_ref, eb_ref,
                  wr_ref, wf_ref, wo_ref, g1_ref, b1_ref, x1_ref, x1b_ref):
    tm = x_ref.shape[0]
    sub = min(tm, ROW_SUBTILE)
    for r in range(tm // sub):
        rows = slice(r * sub, (r + 1) * sub)
        y_r = jnp.dot(or_ref[rows, :], wr_ref[...], preferred_element_type=F32)
        y_f = jnp.dot(of_ref[rows, :], wf_ref[...], preferred_element_type=F32)
        halves = []
        for idx, (gr_ref, gf_ref) in enumerate(((gr0_ref, gf0_ref), (gr1_ref, gf1_ref))):
            sl = slice(idx * COL_BLOCK, (idx + 1) * COL_BLOCK)
            mixed = (_sigmoid(gr_ref[rows, :].astype(F32)) * y_r[:, sl]
                     + _sigmoid(gf_ref[rows, :].astype(F32)) * y_f[:, sl])
            halves.append(mixed.astype(BF16))
        mix = jnp.dot(jnp.concatenate(halves, axis=1), wo_ref[...], preferred_element_type=F32)
        h = _layer_norm(x_ref[rows, :], eg_ref[...], eb_ref[...])
        x1 = _layer_norm(ALPHA * h + mix, g1_ref[...], b1_ref[...])
        x1_ref[rows, :] = x1
        x1b_ref[rows, :] = x1.astype(BF16)


def _merge(o_r, o_f, z, x, eg, eb, w_r, w_f, w_o, g1, b1, tm):
    m = x.shape[0]
    row = lambda i: (i, 0)
    vec = pl.BlockSpec((1, D_MODEL), lambda i: (0, 0))
    zspec = lambda cb: pl.BlockSpec((tm, COL_BLOCK), lambda i: (i, cb))
    return pl.pallas_call(
        _merge_kernel,
        grid=(m // tm,),
        in_specs=[pl.BlockSpec((tm, W_R), row), pl.BlockSpec((tm, W_F), row),
                  zspec(CB_GR), zspec(CB_GR + 1), zspec(CB_GF), zspec(CB_GF + 1),
                  pl.BlockSpec((tm, D_MODEL), row), vec, vec,
                  _resident((W_R, D_MODEL)), _resident((W_F, D_MODEL)), _resident((D_MODEL, D_MODEL)),
                  vec, vec],
        out_specs=[pl.BlockSpec((tm, D_MODEL), row), pl.BlockSpec((tm, D_MODEL), row)],
        out_shape=[jax.ShapeDtypeStruct((m, D_MODEL), F32), jax.ShapeDtypeStruct((m, D_MODEL), BF16)],
        compiler_params=_params(("parallel",)),
        name="merge_out_ln1",
    )(o_r, o_f, z, z, z, z, x, eg, eb, w_r, w_f, w_o, g1, b1)


def _ffn_up(x_ref, w1_ref, w3_ref):
    xb = x_ref[...]
    a = jnp.dot(xb, w1_ref[...].astype(BF16), preferred_element_type=F32)
    b = jnp.dot(xb, w3_ref[...].astype(BF16), preferred_element_type=F32)
    return (a * _sigmoid(a) * b).astype(BF16)


def _ffn_down(mid, w2_ref, o_ref):
    o_ref[...] += jnp.dot(mid, w2_ref[...].astype(BF16), preferred_element_type=F32)


def _ffn_tile(x_ref, w1_ref, w3_ref, w2_ref, o_ref):
    _ffn_down(_ffn_up(x_ref, w1_ref, w3_ref), w2_ref, o_ref)


def _ffn_kernel(x_ref, w1_ref, w3_ref, w2_ref, o_ref):
    @pl.when(pl.program_id(1) == 0)
    def _():
        o_ref[...] = jnp.zeros_like(o_ref)

    _ffn_tile(x_ref, w1_ref, w3_ref, w2_ref, o_ref)


def _ffn_decode_kernel(pt_ref, x_ref, w1_ref, w3_ref, w2_ref, q_ref, kn_ref, vn_ref, bias_ref, *refs,
                       pages, n_groups):
    o_ref = refs[2 * pages]
    d = _DecodeRefs(q_ref, kn_ref, vn_ref, bias_ref, refs[:pages], refs[pages:2 * pages], *refs[2 * pages + 1:])
    j = pl.program_id(1)
    g = (pl.program_id(0) * pl.num_programs(1) + j) % n_groups

    @pl.when(j == 0)
    def _():
        o_ref[...] = jnp.zeros_like(o_ref)

    pl.when(g == 0)(lambda: _decode_begin(d))
    mid = _ffn_up(x_ref, w1_ref, w3_ref)
    s_parts = _decode_scores(d)
    _ffn_down(mid, w2_ref, o_ref)
    _decode_update(d, s_parts)
    pl.when(g == n_groups - 1)(lambda: _decode_end(d))


def _ffn(x1b, w1, w3, w2, tm, tf):
    m = x1b.shape[0]
    return pl.pallas_call(
        _ffn_kernel,
        grid=(m // tm, D_FF // tf),
        in_specs=[pl.BlockSpec((tm, D_MODEL), lambda i, j: (i, 0)),
                  pl.BlockSpec((D_MODEL, tf), lambda i, j: (0, j)),
                  pl.BlockSpec((D_MODEL, tf), lambda i, j: (0, j)),
                  pl.BlockSpec((tf, D_MODEL), lambda i, j: (j, 0))],
        out_specs=pl.BlockSpec((tm, D_MODEL), lambda i, j: (i, 0)),
        out_shape=jax.ShapeDtypeStruct((m, D_MODEL), F32),
        compiler_params=_params(("parallel", "arbitrary")),
        name="swiglu_ffn_rows",
    )(x1b, w1, w3, w2)


def _ffn_with_decode(x1b, w1, w3, w2, tm, tf, page_table, q, k_new, v_new, bias, k_pool, v_pool,
                     pages=DECODE_PAGES_PER_STEP):
    m = x1b.shape[0]
    nj = D_FF // tf
    n_groups = page_table.shape[1] // pages
    n_seq = (m // tm) * nj // n_groups
    assert n_seq * n_groups == (m // tm) * nj and n_seq <= page_table.shape[0]

    def seq_group(i, j):
        s = i * nj + j
        return s // n_groups, s % n_groups

    dec_in, dec_out, dec_scratch = _decode_specs(pages, seq_group)
    return pl.pallas_call(
        functools.partial(_ffn_decode_kernel, pages=pages, n_groups=n_groups),
        grid_spec=pltpu.PrefetchScalarGridSpec(
            num_scalar_prefetch=1,
            grid=(m // tm, nj),
            in_specs=[pl.BlockSpec((tm, D_MODEL), lambda i, j, pt: (i, 0), pipeline_mode=pl.Buffered(1)),
                      pl.BlockSpec((D_MODEL, tf), lambda i, j, pt: (0, j)),
                      pl.BlockSpec((D_MODEL, tf), lambda i, j, pt: (0, j)),
                      pl.BlockSpec((tf, D_MODEL), lambda i, j, pt: (j, 0))] + dec_in,
            out_specs=[pl.BlockSpec((tm, D_MODEL), lambda i, j, pt: (i, 0)), dec_out],
            scratch_shapes=dec_scratch),
        out_shape=[jax.ShapeDtypeStruct((m, D_MODEL), F32),
                   jax.ShapeDtypeStruct((n_seq, H_F, D_F), BF16)],
        compiler_params=_params(("arbitrary", "arbitrary")),
        name="swiglu_ffn_decode",
    )(page_table, x1b, w1, w3, w2, q, k_new, v_new, bias, *([k_pool] * pages), *([v_pool] * pages))


def _ple_kernel(x1_ref, x1b_ref, ffn_ref, p_ref, wpg_ref, wpe_ref, g2_ref, b2_ref, y_ref):
    tm = x1_ref.shape[0]
    sub = min(tm, ROW_SUBTILE)
    for r in range(tm // sub):
        rows = slice(r * sub, (r + 1) * sub)
        pg = jnp.dot(x1b_ref[rows, :], wpg_ref[...], preferred_element_type=F32)
        pe = jnp.dot(p_ref[rows, :].astype(BF16), wpe_ref[...], preferred_element_type=F32)
        y_ref[rows, :] = _layer_norm(ALPHA * x1_ref[rows, :] + ffn_ref[rows, :] + _sigmoid(pg) * pe,
                                     g2_ref[...], b2_ref[...])


def _ple(x1, x1b, ffn, p, w_pg, w_pe, g2, b2, tm):
    m = x1.shape[0]
    row = lambda i: (i, 0)
    vec = pl.BlockSpec((1, D_MODEL), lambda i: (0, 0))
    return pl.pallas_call(
        _ple_kernel,
        grid=(m // tm,),
        in_specs=[pl.BlockSpec((tm, D_MODEL), row), pl.BlockSpec((tm, D_MODEL), row),
                  pl.BlockSpec((tm, D_MODEL), row), pl.BlockSpec((tm, D_PLE), row),
                  _resident((D_MODEL, D_MODEL)), _resident((D_PLE, D_MODEL)), vec, vec],
        out_specs=pl.BlockSpec((tm, D_MODEL), row),
        out_shape=jax.ShapeDtypeStruct((m, D_MODEL), F32),
        compiler_params=_params(("parallel",)),
        name="ple_ln2",
    )(x1, x1b, ffn, p, w_pg, w_pe, g2, b2)


def _rope_tables(pos):
    half = DK_R // 2
    inv = ROPE_BASE ** (-jnp.arange(half, dtype=F32) / half)
    ang = pos.astype(F32)[:, None] * inv[None, :]
    return jnp.cos(ang), jnp.sin(ang)


def kernel(x_prompt, x_sample, state_ret, cache_k, cache_v, cache_logf, page_table, p_prompt, p_sample, ln_emb_g, ln_emb_b, w_in, b_f, gn_g, w_ret_proj, w_fox_proj, w_out, ln1_g, ln1_b, w1, w3, w2, w_pg, w_pe, ln2_g, ln2_b):
    assert w_in.shape[0] == DEPTH == 1
    batch, seq, _ = x_prompt.shape
    dec_b, dec_t, _ = x_sample.shape
    assert dec_t == 1
    n_pages = page_table.shape[1]
    past_len = n_pages * PAGE_SIZE
    n_pool = cache_k.shape[1]

    w_in_t = jnp.swapaxes(w_in[0], 0, 1)
    w_ff_t = w_in_t[FF_LO:FF_HI].astype(BF16)
    w_ff_pad = jnp.pad(w_ff_t, ((0, W_FF_PAD - H_F), (0, 0)))
    bf_col = b_f[0].reshape(H_F, 1)
    vec = lambda v: v.reshape(1, D_MODEL)
    eg, eb = vec(ln_emb_g), vec(ln_emb_b)
    g1, b1, g2, b2 = vec(ln1_g[0]), vec(ln1_b[0]), vec(ln2_g[0]), vec(ln2_b[0])
    w_r, w_f, w_o = w_ret_proj[0].astype(BF16), w_fox_proj[0].astype(BF16), w_out[0].astype(BF16)
    w_pgb, w_peb = w_pg[0].astype(BF16), w_pe[0].astype(BF16)
    log_g = jnp.log1p(-jnp.exp2(-5.0 - jnp.arange(H_R, dtype=F32)))
    cos_p, sin_p = _rope_tables(jnp.arange(seq))
    cos_s, sin_s = _rope_tables(jnp.full((dec_b,), past_len))

    xp = x_prompt.reshape(batch * seq, D_MODEL)
    xs = x_sample.reshape(dec_b, D_MODEL)

    h_p = _entry_ln(xp, eg, eb, tm=512)
    z_p, fk_p, fv_p, ff_p, z_s, fk_s, fv_s = _inproj(h_p, xs, eg, eb, w_in_t, w_ff_pad,
                                                     cos_p, sin_p, cos_s, sin_s, tm=IN_TM)
    lf_p, c2_row = _logf_prompt(ff_p, bf_col, seq)
    lf_s = _logf_sample(xs, eg, eb, w_ff_t, bf_col)

    o_r_p, st_p = _ret_prompt(z_p, log_g, gn_g[0], batch, seq)
    o_f_p = _fox_prompt(z_p, c2_row, batch, seq)

    z3 = z_s.reshape(dec_b, 1, z_s.shape[1])
    o_r_s, st_s = _ret_sample(z3, log_g, gn_g[0], state_ret[0])
    width = PAGE_SIZE * H_F
    lf_new = jnp.swapaxes(lf_s, 1, 2).reshape(dec_b, H_F)
    lnew = jnp.tile(lf_new, (1, PAGE_SIZE)).reshape(dec_b, 1, width)
    bias = _decode_bias(page_table, cache_logf[0].reshape(n_pool, width), lnew)
    heads = lambda cb: z_s[:, cb * COL_BLOCK:(cb + 1) * COL_BLOCK].reshape(dec_b, H_F, D_F)
    decode_args = (heads(CB_FQ), heads(CB_FK), heads(CB_FV), bias,
                   cache_k[0].reshape(n_pool, width, D_F), cache_v[0].reshape(n_pool, width, D_F))

    x1_p, x1b_p = _merge(o_r_p, o_f_p, z_p, xp, eg, eb, w_r, w_f, w_o, g1, b1, tm=MERGE_TM)
    ffn_p, o_f_ride = _ffn_with_decode(x1b_p, w1[0], w3[0], w2[0], FFN_TM, FFN_TF, page_table, *decode_args)
    n_ride = o_f_ride.shape[0]
    q_r, kn_r, vn_r, bias_r = (a[n_ride:] for a in decode_args[:4])
    parts = _sc_decode_partials(page_table[n_ride:], q_r.reshape(-1, W_F).astype(F32), bias_r, *decode_args[4:])
    o_f_rest = _decode_combine(q_r, kn_r, vn_r, *parts)
    o_f_s = jnp.concatenate([o_f_ride, o_f_rest], axis=0)
    x1_s, x1b_s = _merge(o_r_s.reshape(dec_b, W_R), o_f_s.reshape(dec_b, W_F), z_s, xs, eg, eb,
                         w_r, w_f, w_o, g1, b1, tm=dec_b)
    ffn_s = _ffn(x1b_s, w1[0], w3[0], w2[0], tm=dec_b, tf=FFN_TF)
    y_p = _ple(x1_p, x1b_p, ffn_p, p_prompt[0].reshape(batch * seq, D_PLE), w_pgb, w_peb, g2, b2, tm=PLE_TM)
    y_s = _ple(x1_s, x1b_s, ffn_s, p_sample[0].reshape(dec_b, D_PLE), w_pgb, w_peb, g2, b2, tm=dec_b)

    return (y_p.reshape(batch, seq, D_MODEL),
            y_s.reshape(dec_b, 1, D_MODEL),
            st_p[None],
            fk_p.reshape(1, batch, seq, H_F, D_F),
            fv_p.reshape(1, batch, seq, H_F, D_F),
            jnp.swapaxes(lf_p, 1, 2)[None],
            st_s[None],
            fk_s.reshape(1, dec_b, 1, H_F, D_F),
            fv_s.reshape(1, dec_b, 1, H_F, D_F),
            jnp.swapaxes(lf_s, 1, 2).reshape(1, dec_b, 1, H_F))
```

```python
import functools

import jax
import jax.numpy as jnp
import numpy as np
from jax import lax
from jax.experimental import pallas as pl
from jax.experimental.pallas import tpu as pltpu
from jax.experimental.pallas import tpu_sc as plsc

F32 = jnp.float32
BF16 = jnp.bfloat16

D_MODEL = 2048
DEPTH = 1
PAGE_SIZE = 128
H_R = 4
DK_R = 256
DV_R = 256
W_R = H_R * DV_R
CHUNK = 256
ROPE_BASE = 10000.0
GN_EPS = 1e-6
H_F = 8
D_F = 128
W_F = H_F * D_F
D_FF = 5632
D_PLE = 256
LN_EPS = 1e-5
ALPHA = (2.0 * DEPTH) ** 0.25
FOX_SCALE = D_F ** -0.5
LOG2E = float(np.log2(np.e))

COL_BLOCK = 1024
N_MAIN = 11 * COL_BLOCK
CB_RQ, CB_RK, CB_RV, CB_RG, CB_FQ, CB_FK, CB_FV, CB_GR, CB_GF = 0, 1, 2, 3, 4, 5, 6, 7, 9
FF_LO, FF_HI = 7 * COL_BLOCK, 7 * COL_BLOCK + H_F

IN_TM, IN_TN = 1024, 1024
W_FF_PAD = 128
N_A_BLOCKS = FF_LO // IN_TN
FFN_TM, FFN_TF = 1024, 256
MERGE_TM, PLE_TM, ROW_SUBTILE = 256, 512, 256

VMEM_LIMIT = 56 * 1024 * 1024
DECODE_PAGES_PER_STEP = 8
RET_SAMPLE_SEQS = 4

NT_DIMS = (((1,), (1,)), ((), ()))
TN_DIMS = (((0,), (0,)), ((), ()))


def _params(semantics):
    return pltpu.CompilerParams(dimension_semantics=semantics, vmem_limit_bytes=VMEM_LIMIT)


def _layer_norm(x, g, b):
    mu = jnp.mean(x, axis=-1, keepdims=True)
    xc = x - mu
    var = jnp.mean(xc * xc, axis=-1, keepdims=True)
    return xc * lax.rsqrt(var + LN_EPS) * g + b


def _sigmoid(x):
    return 1.0 / (1.0 + jnp.exp(-x))


def _log_sigmoid(x):
    return jnp.minimum(x, 0.0) - jnp.log(1.0 + jnp.exp(-jnp.abs(x)))


def _resident(shape):
    return pl.BlockSpec(shape, lambda *_: (0,) * len(shape), pipeline_mode=pl.Buffered(1))


def _ln_kernel(x_ref, g_ref, b_ref, o_ref):
    o_ref[...] = _layer_norm(x_ref[...], g_ref[...], b_ref[...]).astype(o_ref.dtype)


def _entry_ln(x, g, b, tm):
    m = x.shape[0]
    return pl.pallas_call(
        _ln_kernel,
        grid=(m // tm,),
        in_specs=[pl.BlockSpec((tm, D_MODEL), lambda i: (i, 0)),
                  pl.BlockSpec((1, D_MODEL), lambda i: (0, 0)),
                  pl.BlockSpec((1, D_MODEL), lambda i: (0, 0))],
        out_specs=pl.BlockSpec((tm, D_MODEL), lambda i: (i, 0)),
        out_shape=jax.ShapeDtypeStruct((m, D_MODEL), BF16),
        compiler_params=_params(("parallel",)),
        name="entry_ln",
    )(x, g, b)


def _inproj_block(h, w_ref, z_ref, rope=None, f32_ref=None):
    half = DK_R // 2
    for hd in range(IN_TN // DK_R):
        lo = hd * DK_R
        acc = lax.dot_general(h, w_ref[lo:lo + DK_R, :].astype(BF16), NT_DIMS, preferred_element_type=F32)
        if rope is None:
            z_ref[:, lo:lo + DK_R] = acc.astype(BF16)
        else:
            cos, sin, scale = rope
            x1 = acc[:, :half]
            x2 = acc[:, half:]
            z_ref[:, lo:lo + half] = ((x1 * cos - x2 * sin) * scale).astype(BF16)
            z_ref[:, lo + half:lo + DK_R] = ((x2 * cos + x1 * sin) * scale).astype(BF16)
        if f32_ref is not None:
            f32_ref[:, lo:lo + DK_R] = acc


def _inproj_kernel(h_ref, xs_ref, eg_ref, eb_ref, w_ref, wff_ref, cos_ref, sin_ref, coss_ref, sins_ref,
                   z_ref, fk_ref, fv_ref, ff_ref, zs_ref, fks_ref, fvs_ref):
    i = pl.program_id(0)
    j = pl.program_id(1)
    per = COL_BLOCK // IN_TN

    @pl.when(j == 0)
    def _():
        ff_ref[...] = lax.dot_general(h_ref[...], wff_ref[...], NT_DIMS, preferred_element_type=F32)

    def section(lo, hi, rope_scale=None, f32_refs=(None, None)):
        @pl.when((j >= lo) & (j < hi))
        def _():
            rope = None if rope_scale is None else (cos_ref[...], sin_ref[...], rope_scale)
            _inproj_block(h_ref[...], w_ref, z_ref, rope, f32_refs[0])

            @pl.when(i == 0)
            def _():
                hs = _layer_norm(xs_ref[...], eg_ref[...], eb_ref[...]).astype(BF16)
                rope_s = None if rope_scale is None else (coss_ref[...], sins_ref[...], rope_scale)
                _inproj_block(hs, w_ref, zs_ref, rope_s, f32_refs[1])

    @pl.when(i > 0)
    def _():
        zs_ref[...] = jnp.zeros_like(zs_ref)

    section(CB_RQ * per, CB_RK * per, rope_scale=1.0)
    section(CB_RK * per, CB_RV * per, rope_scale=DK_R ** -0.5)
    section(CB_RV * per, CB_FK * per)
    section(CB_FK * per, CB_FV * per, f32_refs=(fk_ref, fks_ref))
    section(CB_FV * per, N_A_BLOCKS, f32_refs=(fv_ref, fvs_ref))
    section(N_A_BLOCKS, N_MAIN // IN_TN)


def _inproj(h, xs, eg, eb, w_in_t, w_ff_pad, cos, sin, cos_s, sin_s, tm):
    m = h.shape[0]
    ms = xs.shape[0]
    n_pos_blocks = cos.shape[0] // tm
    nj = N_MAIN // IN_TN
    per = COL_BLOCK // IN_TN
    half = DK_R // 2
    clamp = lambda j, cb: jnp.clip(j - cb * per, 0, per - 1)
    row = lambda i, j: (i, 0)
    fixed = lambda i, j: (0, 0)
    return pl.pallas_call(
        _inproj_kernel,
        grid=(m // tm, nj),
        in_specs=[pl.BlockSpec((tm, D_MODEL), row),
                  pl.BlockSpec((ms, D_MODEL), fixed),
                  pl.BlockSpec((1, D_MODEL), fixed),
                  pl.BlockSpec((1, D_MODEL), fixed),
                  pl.BlockSpec((pl.Element(IN_TN), pl.Element(D_MODEL)),
                               lambda i, j: (pl.multiple_of(IN_TN * j + jnp.where(j < N_A_BLOCKS, 0, H_F), 8), 0)),
                  pl.BlockSpec((W_FF_PAD, D_MODEL), fixed),
                  pl.BlockSpec((tm, half), lambda i, j: (i % n_pos_blocks, 0)),
                  pl.BlockSpec((tm, half), lambda i, j: (i % n_pos_blocks, 0)),
                  pl.BlockSpec((ms, half), fixed),
                  pl.BlockSpec((ms, half), fixed)],
        out_specs=[pl.BlockSpec((tm, IN_TN), lambda i, j: (i, j)),
                   pl.BlockSpec((tm, IN_TN), lambda i, j: (i, clamp(j, CB_FK))),
                   pl.BlockSpec((tm, IN_TN), lambda i, j: (i, clamp(j, CB_FV))),
                   pl.BlockSpec((tm, W_FF_PAD), row),
                   pl.BlockSpec((ms, IN_TN), lambda i, j: (0, jnp.where(i == 0, j, nj))),
                   pl.BlockSpec((ms, IN_TN), lambda i, j: (0, jnp.where(i == 0, clamp(j, CB_FK), per - 1))),
                   pl.BlockSpec((ms, IN_TN), lambda i, j: (0, jnp.where(i == 0, clamp(j, CB_FV), per - 1)))],
        out_shape=[jax.ShapeDtypeStruct((m, N_MAIN), BF16),
                   jax.ShapeDtypeStruct((m, W_F), F32),
                   jax.ShapeDtypeStruct((m, W_F), F32),
                   jax.ShapeDtypeStruct((m, W_FF_PAD), F32),
                   jax.ShapeDtypeStruct((ms, N_MAIN + IN_TN), BF16),
                   jax.ShapeDtypeStruct((ms, W_F), F32),
                   jax.ShapeDtypeStruct((ms, W_F), F32)],
        compiler_params=_params(("arbitrary", "arbitrary")),
        name="in_proj",
    )(h, xs, eg, eb, w_in_t, w_ff_pad, cos, sin, cos_s, sin_s)


def _cumsum_lanes(x):
    n = x.shape[-1]
    lane = lax.broadcasted_iota(jnp.int32, x.shape, x.ndim - 1)
    s = 1
    while s < n:
        x = x + jnp.where(lane >= s, pltpu.roll(x, s, axis=x.ndim - 1), 0.0)
        s *= 2
    return x


def _logf_prompt_kernel(ff_ref, bf_ref, lf_ref, c_ref):
    ff = ff_ref[...].T[:H_F, :]
    lf = _log_sigmoid(ff + bf_ref[...])
    lf_ref[0] = lf
    c_ref[0] = _cumsum_lanes(lf) * LOG2E


def _logf_sample_kernel(x_ref, eg_ref, eb_ref, w_ref, bf_ref, lf_ref):
    h = _layer_norm(x_ref[...], eg_ref[...], eb_ref[...]).astype(BF16)
    ff = lax.dot_general(w_ref[...], h, NT_DIMS, preferred_element_type=F32)
    lf_ref[0] = _log_sigmoid(ff + bf_ref[...])


def _logf_prompt(ff, bf_col, seq):
    nb = ff.shape[0] // seq
    shape = jax.ShapeDtypeStruct((nb, H_F, seq), F32)
    spec = pl.BlockSpec((1, H_F, seq), lambda b: (b, 0, 0))
    return pl.pallas_call(
        _logf_prompt_kernel,
        grid=(nb,),
        in_specs=[pl.BlockSpec((seq, W_FF_PAD), lambda b: (b, 0)),
                  pl.BlockSpec((H_F, 1), lambda b: (0, 0))],
        out_specs=[spec, spec],
        out_shape=[shape, shape],
        compiler_params=_params(("parallel",)),
        name="log_forget_cumsum",
    )(ff, bf_col)


def _logf_sample(xs, eg, eb, w_ff_t, bf_col):
    ms = xs.shape[0]
    fixed = lambda b: (0, 0)
    return pl.pallas_call(
        _logf_sample_kernel,
        grid=(1,),
        in_specs=[pl.BlockSpec((ms, D_MODEL), fixed), pl.BlockSpec((1, D_MODEL), fixed),
                  pl.BlockSpec((1, D_MODEL), fixed), pl.BlockSpec((H_F, D_MODEL), fixed),
                  pl.BlockSpec((H_F, 1), fixed)],
        out_specs=pl.BlockSpec((1, H_F, ms), lambda b: (0, 0, 0)),
        out_shape=jax.ShapeDtypeStruct((1, H_F, ms), F32),
        compiler_params=_params(("arbitrary",)),
        name="log_forget",
    )(xs, eg, eb, w_ff_t, bf_col)


def _group_norm_gate(o, gate_bf16, gn_row):
    mu = jnp.mean(o, axis=-1, keepdims=True)
    oc = o - mu
    var = jnp.mean(oc * oc, axis=-1, keepdims=True)
    gate = gate_bf16.astype(F32)
    return oc * lax.rsqrt(var + GN_EPS) * gn_row * (gate * _sigmoid(gate))


def _ret_prompt_kernel(lg_ref, q_ref, k_ref, v_ref, g_ref, gn_ref, o_ref, st_ref,
                       dec_ref, cross_ref, kdec_ref, carry_ref):
    b = pl.program_id(0)
    c = pl.program_id(1)
    n = q_ref.shape[0]

    @pl.when((b == 0) & (c == 0))
    def _():
        ri = lax.broadcasted_iota(jnp.int32, (n, n), 0)
        ci = lax.broadcasted_iota(jnp.int32, (n, n), 1)
        causal = ri >= ci
        diff = jnp.where(causal, (ri - ci).astype(F32), 0.0)
        pos = lax.broadcasted_iota(jnp.int32, (n, DK_R), 0).astype(F32)
        for hd in range(H_R):
            lg = lg_ref[hd]
            dec_ref[hd] = jnp.where(causal, jnp.exp(lg * diff), 0.0)
            cross_ref[hd] = jnp.exp(lg * (pos + 1.0))
            kdec_ref[hd] = jnp.exp(lg * (n - 1.0 - pos))
            carry_ref[hd] = jnp.exp(jnp.full((1, DV_R), lg * n, F32))

    @pl.when(c == 0)
    def _():
        st_ref[...] = jnp.zeros_like(st_ref)

    heads = lambda ref: jnp.stack([ref[:, hd * DK_R:(hd + 1) * DK_R] for hd in range(H_R)], axis=0)
    bmm = lambda a, b_, ca, cb: lax.dot_general(a, b_, (((ca,), (cb,)), ((0,), (0,))), preferred_element_type=F32)
    q3, k3, v3 = heads(q_ref), heads(k_ref), heads(v_ref)
    sc = bmm(q3, k3, 2, 2) * dec_ref[...]
    o = bmm(sc.astype(BF16), v3, 2, 1)
    state = st_ref[0]
    o = o + bmm(q3, state.astype(BF16), 2, 1) * cross_ref[...]
    kd = (k3.astype(F32) * kdec_ref[...]).astype(BF16)
    st_ref[0] = carry_ref[...] * state + bmm(kd, v3, 1, 1)
    out = _group_norm_gate(o, heads(g_ref), gn_ref[...][:, None, :]).astype(BF16)
    for hd in range(H_R):
        o_ref[:, hd * DV_R:(hd + 1) * DV_R] = out[hd]


def _ret_prompt(z, log_g, gn_g, batch, seq):
    nc = seq // CHUNK
    row = lambda b, c: b * nc + c
    zspec = lambda cb: pl.BlockSpec((CHUNK, COL_BLOCK), lambda b, c: (row(b, c), cb))
    return pl.pallas_call(
        _ret_prompt_kernel,
        grid=(batch, nc),
        in_specs=[pl.BlockSpec(memory_space=pltpu.SMEM),
                  zspec(CB_RQ), zspec(CB_RK), zspec(CB_RV), zspec(CB_RG),
                  pl.BlockSpec((H_R, DV_R), lambda b, c: (0, 0))],
        out_specs=[pl.BlockSpec((CHUNK, W_R), lambda b, c: (row(b, c), 0)),
                   pl.BlockSpec((1, H_R, DK_R, DV_R), lambda b, c: (b, 0, 0, 0))],
        out_shape=[jax.ShapeDtypeStruct((batch * seq, W_R), BF16),
                   jax.ShapeDtypeStruct((batch, H_R, DK_R, DV_R), F32)],
        scratch_shapes=[pltpu.VMEM((H_R, CHUNK, CHUNK), F32),
                        pltpu.VMEM((H_R, CHUNK, DV_R), F32),
                        pltpu.VMEM((H_R, CHUNK, DK_R), F32),
                        pltpu.VMEM((H_R, 1, DV_R), F32)],
        compiler_params=_params(("arbitrary", "arbitrary")),
        name="retention_prompt",
    )(log_g, z, z, z, z, gn_g)


def _ret_sample_kernel(lg_ref, q_ref, k_ref, v_ref, g_ref, gn_ref, st_ref, o_ref, ns_ref):
    row0 = lax.broadcasted_iota(jnp.int32, (8, DK_R), 0) == 0
    for sq in range(q_ref.shape[0]):
        for hd in range(H_R):
            lg = lg_ref[hd]
            sl = slice(hd * DK_R, (hd + 1) * DK_R)
            q = q_ref[sq, :, sl]
            k = k_ref[sq, :, sl]
            v = v_ref[sq, :, sl]
            qf = q.astype(F32)
            kf = k.astype(F32)
            vf = v.astype(F32)
            state = st_ref[sq, hd]
            decay = jnp.exp(jnp.full((1, DV_R), lg, F32))
            score = jnp.sum(qf * kf, axis=-1, keepdims=True)
            q8 = jnp.broadcast_to(q, (8, DK_R))
            qs = jnp.dot(q8, state.astype(BF16), preferred_element_type=F32)[0:1, :]
            o = score * vf + qs * decay
            k8 = jnp.where(row0, jnp.broadcast_to(kf, (8, DK_R)), 0.0).astype(BF16)
            v8 = jnp.broadcast_to(v, (8, DV_R))
            ns_ref[sq, hd] = decay * state + lax.dot_general(k8, v8, TN_DIMS, preferred_element_type=F32)
            o_ref[sq, :, sl] = _group_norm_gate(o, g_ref[sq, :, sl], gn_ref[hd:hd + 1, :]).astype(BF16)


def _ret_sample(z3, log_g, gn_g, state):
    nb = z3.shape[0]
    per = RET_SAMPLE_SEQS
    zspec = lambda cb: pl.BlockSpec((per, 1, COL_BLOCK), lambda b: (b, 0, cb))
    st_spec = pl.BlockSpec((per, H_R, DK_R, DV_R), lambda b: (b, 0, 0, 0))
    return pl.pallas_call(
        _ret_sample_kernel,
        grid=(nb // per,),
        in_specs=[pl.BlockSpec(memory_space=pltpu.SMEM),
                  zspec(CB_RQ), zspec(CB_RK), zspec(CB_RV), zspec(CB_RG),
                  pl.BlockSpec((H_R, DV_R), lambda b: (0, 0)),
                  st_spec],
        out_specs=[pl.BlockSpec((per, 1, W_R), lambda b: (b, 0, 0)), st_spec],
        out_shape=[jax.ShapeDtypeStruct((nb, 1, W_R), BF16),
                   jax.ShapeDtypeStruct(state.shape, F32)],
        compiler_params=_params(("parallel",)),
        name="retention_sample",
    )(log_g, z3, z3, z3, z3, gn_g, state)


def _split3(x):
    hi = x.astype(BF16).astype(F32)
    r1 = x - hi
    mid = r1.astype(BF16).astype(F32)
    return hi, mid, r1 - mid


def _fox_prompt_kernel(q_ref, k_ref, v_ref, crow_ref, o_ref, vt_ref, kx_ref, m_ref, l_ref, acc_ref,
                       *, tq):
    qi = pl.program_id(1)
    nk = vt_ref.shape[1]
    score_scale = FOX_SCALE * LOG2E

    @pl.when(qi == 0)
    def _():
        sel_r = lax.broadcasted_iota(jnp.int32, (8, D_F), 0)
        sel_l = lax.broadcasted_iota(jnp.int32, (8, D_F), 1)
        sel = jnp.where((sel_r == sel_l) & (sel_r < 3), 1.0, 0.0).astype(BF16)
        for hd in range(H_F):
            for kj in range(nk):
                v_tile = v_ref[kj * tq:(kj + 1) * tq, hd * D_F:(hd + 1) * D_F]
                vt_ref[hd, kj] = v_tile.astype(F32).T.astype(BF16)
                parts = _split3(crow_ref[0, hd, kj:kj + 1, :] * (-1.0 / score_scale))
                rows = jnp.concatenate(list(parts) + [jnp.zeros((5, tq), F32)], axis=0).astype(BF16)
                kx_ref[hd, kj] = lax.dot_general(rows, sel, TN_DIMS, preferred_element_type=F32).astype(BF16)

    m_ref[...] = jnp.full(m_ref.shape, -jnp.inf, F32)
    l_ref[...] = jnp.zeros_like(l_ref)
    acc_ref[...] = jnp.zeros_like(acc_ref)
    key_i = lax.broadcasted_iota(jnp.int32, (tq, tq), 0)
    qry_i = lax.broadcasted_iota(jnp.int32, (tq, tq), 1)
    visible = key_i <= qry_i

    heads = lambda ref, rows: jnp.stack([ref[rows, hd * D_F:(hd + 1) * D_F] for hd in range(H_F)], axis=0)
    ones3 = jnp.where(lax.broadcasted_iota(jnp.int32, (H_F, tq, D_F), 2) < 3, 1.0, 0.0).astype(BF16)
    q3 = jnp.concatenate([heads(q_ref, slice(None)), ones3], axis=2)
    c_q = crow_ref[0, :, pl.ds(qi, 1), :]

    def tile(kj, on_diagonal):
        ks = pl.multiple_of(kj * tq, tq)
        k3 = jnp.concatenate([heads(k_ref, pl.ds(ks, tq)), kx_ref[:, kj]], axis=2)
        s = lax.dot_general(k3, q3, (((2,), (2,)), ((0,), (0,))), preferred_element_type=F32)
        t = s * score_scale + c_q
        if on_diagonal:
            t = jnp.where(visible[None], t, -jnp.inf)
        m_old = m_ref[...]
        m_new = jnp.maximum(m_old, jnp.max(t, axis=1, keepdims=True))
        alpha = jnp.exp2(m_old - m_new)
        p = jnp.exp2(t - m_new)
        l_ref[...] = alpha * l_ref[...] + jnp.sum(p, axis=1, keepdims=True)
        m_ref[...] = m_new
        pv = lax.dot_general(vt_ref[:, kj], p.astype(BF16), (((2,), (1,)), ((0,), (0,))),
                             preferred_element_type=F32)
        acc_ref[...] = alpha * acc_ref[...] + pv

    def body(kj, carry):
        tile(kj, False)
        return carry

    lax.fori_loop(0, qi, body, 0)
    tile(qi, True)
    for hd in range(H_F):
        o_ref[:, hd * D_F:(hd + 1) * D_F] = (acc_ref[hd] / l_ref[hd]).T.astype(BF16)


def _fox_prompt(z, c2_row, batch, seq, tq=256):
    nq = seq // tq
    return pl.pallas_call(
        functools.partial(_fox_prompt_kernel, tq=tq),
        grid=(batch, nq),
        in_specs=[pl.BlockSpec((tq, COL_BLOCK), lambda b, i: (b * nq + i, CB_FQ)),
                  pl.BlockSpec((seq, COL_BLOCK), lambda b, i: (b, CB_FK)),
                  pl.BlockSpec((seq, COL_BLOCK), lambda b, i: (b, CB_FV)),
                  pl.BlockSpec((1, H_F, nq, tq), lambda b, i: (b, 0, 0, 0))],
        out_specs=pl.BlockSpec((tq, W_F), lambda b, i: (b * nq + i, 0)),
        out_shape=jax.ShapeDtypeStruct((batch * seq, W_F), BF16),
        scratch_shapes=[pltpu.VMEM((H_F, nq, D_F, tq), BF16),
                        pltpu.VMEM((H_F, nq, tq, D_F), BF16),
                        pltpu.VMEM((H_F, 1, tq), F32),
                        pltpu.VMEM((H_F, 1, tq), F32),
                        pltpu.VMEM((H_F, D_F, tq), F32)],
        compiler_params=_params(("parallel", "arbitrary")),
        name="fox_prompt",
    )(z, z, z, c2_row.reshape(batch, H_F, nq, tq))


def _decode_bias_kernel(pt_ref, pool_ref, lnew_ref, bias_ref, rows_ref, *, n_pages):
    b = pl.program_id(0)
    width = PAGE_SIZE * H_F
    for p in range(n_pages):
        rows_ref[p:p + 1, :] = pool_ref[pl.ds(pt_ref[b, p], 1), :]
    x = rows_ref[...]
    lane = lax.broadcasted_iota(jnp.int32, x.shape, 1)
    within = x
    total = x
    s = H_F
    while s < width:
        within = within + jnp.where(lane >= s, pltpu.roll(within, s, axis=1), 0.0)
        total = total + pltpu.roll(total, s, axis=1)
        s *= 2
    rest = total - within
    run = lnew_ref[0]
    for p in reversed(range(n_pages)):
        bias_ref[0, p:p + 1, :] = run + rest[p:p + 1, :]
        run = run + total[p:p + 1, :]


def _decode_bias(page_table, logf_pool, lnew):
    nb, n_pages = page_table.shape
    width = PAGE_SIZE * H_F
    return pl.pallas_call(
        functools.partial(_decode_bias_kernel, n_pages=n_pages),
        grid_spec=pltpu.PrefetchScalarGridSpec(
            num_scalar_prefetch=1,
            grid=(nb,),
            in_specs=[pl.BlockSpec(logf_pool.shape, lambda b, pt: (0, 0), pipeline_mode=pl.Buffered(1)),
                      pl.BlockSpec((1, 1, width), lambda b, pt: (b, 0, 0))],
            out_specs=pl.BlockSpec((1, n_pages, width), lambda b, pt: (b, 0, 0)),
            scratch_shapes=[pltpu.VMEM((n_pages, width), F32)]),
        out_shape=jax.ShapeDtypeStruct((nb, n_pages, width), F32),
        compiler_params=_params(("arbitrary",)),
        name="decode_forget_bias",
    )(page_table, logf_pool, lnew)


class _DecodeRefs:
    def __init__(self, q_ref, kn_ref, vn_ref, bias_ref, k_refs, v_refs, o_ref, m_ref, l_ref, acc_ref):
        self.q, self.kn, self.vn, self.bias = q_ref, kn_ref, vn_ref, bias_ref
        self.k, self.v = k_refs, v_refs
        self.o, self.m, self.l, self.acc = o_ref, m_ref, l_ref, acc_ref


def _decode_begin(d):
    q = d.q[0].astype(F32)
    s_new = jnp.sum(q * d.kn[0].astype(F32), axis=-1, keepdims=True) * FOX_SCALE
    d.m[...] = jnp.broadcast_to(s_new, d.m.shape)
    d.l[...] = jnp.ones_like(d.l)
    d.acc[...] = d.vn[0].astype(F32)


def _decode_scores(d):
    width = PAGE_SIZE * H_F
    row = lax.broadcasted_iota(jnp.int32, (H_F, width), 0)
    lane = lax.broadcasted_iota(jnp.int32, (H_F, width), 1)
    own = (lane % H_F) == row
    q = d.q[0]
    s_parts = []
    for i, k_ref in enumerate(d.k):
        kt = k_ref[...].astype(BF16)
        s = lax.dot_general(q, kt, NT_DIMS, preferred_element_type=F32)
        s_parts.append(jnp.where(own, s * FOX_SCALE + d.bias[0, i:i + 1, :], -jnp.inf))
    return s_parts


def _decode_update(d, s_parts):
    m_old = d.m[:, 0:1]
    m_new = m_old
    for s in s_parts:
        m_new = jnp.maximum(m_new, jnp.max(s, axis=-1, keepdims=True))
    alpha = jnp.exp(m_old - m_new)
    l_new = alpha * d.l[:, 0:1]
    pv = jnp.zeros((H_F, D_F), F32)
    for s, v_ref in zip(s_parts, d.v):
        p = jnp.exp(s - m_new)
        l_new = l_new + jnp.sum(p, axis=-1, keepdims=True)
        pv = pv + jnp.dot(p.astype(BF16), v_ref[...].astype(BF16), preferred_element_type=F32)
    d.m[...] = jnp.broadcast_to(m_new, d.m.shape)
    d.l[...] = jnp.broadcast_to(l_new, d.l.shape)
    d.acc[...] = alpha * d.acc[...] + pv


def _decode_end(d):
    d.o[0] = (d.acc[...] / d.l[:, 0:1]).astype(BF16)


def _fox_decode_kernel(pt_ref, q_ref, kn_ref, vn_ref, bias_ref, *refs, pages):
    d = _DecodeRefs(q_ref, kn_ref, vn_ref, bias_ref, refs[:pages], refs[pages:2 * pages], *refs[2 * pages:])
    g = pl.program_id(1)
    pl.when(g == 0)(lambda: _decode_begin(d))
    _decode_update(d, _decode_scores(d))
    pl.when(g == pl.num_programs(1) - 1)(lambda: _decode_end(d))


def _decode_specs(pages, seq_group):
    width = PAGE_SIZE * H_F

    def head_map(*args):
        return (seq_group(*args[:-1])[0], 0, 0)

    def bias_map(*args):
        b, g = seq_group(*args[:-1])
        return (b, g, 0)

    def page_spec(i):
        def page_map(*args):
            b, g = seq_group(*args[:-1])
            return (args[-1][b, g * pages + i], 0, 0)
        return pl.BlockSpec((None, width, D_F), page_map)

    head_spec = pl.BlockSpec((1, H_F, D_F), head_map)
    in_specs = ([head_spec, head_spec, head_spec, pl.BlockSpec((1, pages, width), bias_map)]
                + [page_spec(i) for i in range(pages)] * 2)
    scratch = [pltpu.VMEM((H_F, D_F), F32), pltpu.VMEM((H_F, D_F), F32), pltpu.VMEM((H_F, D_F), F32)]
    return in_specs, head_spec, scratch


def _fox_decode(page_table, q, k_new, v_new, bias, k_pool, v_pool, pages=DECODE_PAGES_PER_STEP):
    nb, n_pages = page_table.shape
    in_specs, out_spec, scratch = _decode_specs(pages, lambda b, g: (b, g))
    return pl.pallas_call(
        functools.partial(_fox_decode_kernel, pages=pages),
        grid_spec=pltpu.PrefetchScalarGridSpec(
            num_scalar_prefetch=1,
            grid=(nb, n_pages // pages),
            in_specs=in_specs,
            out_specs=out_spec,
            scratch_shapes=scratch),
        out_shape=jax.ShapeDtypeStruct((nb, H_F, D_F), BF16),
        compiler_params=_params(("parallel", "arbitrary")),
        name="fox_decode",
    )(page_table, q, k_new, v_new, bias, *([k_pool] * pages), *([v_pool] * pages))


SC_LANES = 16
SC_PARTS = 2
SC_CHUNK_TOKENS = 16
SC_NEG = -1e30


def _sc_decode_body(pt_hbm, q_hbm, b_hbm, k_hbm, v_hbm, acc_hbm, m_hbm, l_hbm,
                    pt_v, q_v, k_v, v_v, b_v, u_v, acc_v, m_v, l_v, sem_k, sem_v, sem_b,
                    *, n_workers, n_pages):
    ln = SC_LANES
    chunks_per_page = PAGE_SIZE // SC_CHUNK_TOKENS
    rows = SC_CHUNK_TOKENS * H_F
    w = lax.axis_index("s") * 2 + lax.axis_index("c")

    @pl.when(w < n_workers)
    def _():
        assert SC_PARTS == 2
        seq = lax.shift_right_logical(w, 1)
        part = w - seq * SC_PARTS
        base_pages = n_pages // SC_PARTS
        p_lo = part * base_pages + jnp.minimum(part, n_pages % SC_PARTS)
        n_p = base_pages + jnp.where(part < n_pages % SC_PARTS, 1, 0)
        pltpu.sync_copy(pt_hbm.at[seq], pt_v)
        pltpu.sync_copy(q_hbm.at[seq], q_v)
        lane = lax.iota(jnp.int32, ln)

        @pl.loop(0, W_F, step=ln)
        def _(c0):
            acc_v[pl.ds(c0, ln)] = jnp.zeros((ln,), F32)

        @pl.loop(0, H_F * ln, step=ln)
        def _(c0):
            m_v[pl.ds(c0, ln)] = jnp.full((ln,), SC_NEG, F32)
            l_v[pl.ds(c0, ln)] = jnp.zeros((ln,), F32)

        def chunk(ci, carry):
            pg_i = p_lo + lax.shift_right_logical(ci, 3)
            sub = ci & (chunks_per_page - 1)
            base16 = lax.shift_right_logical(pg_i, 4) * ln
            page = jnp.max(jnp.where(lane == pg_i - base16, pt_v[pl.ds(base16, ln)], 0))
            row0 = pl.multiple_of(page * (PAGE_SIZE * H_F) + sub * rows, rows)
            copy_k = pltpu.async_copy(k_hbm.at[pl.ds(row0, rows)], k_v, sem_k)
            copy_v = pltpu.async_copy(v_hbm.at[pl.ds(row0, rows)], v_v, sem_v)
            copy_b = pltpu.async_copy(b_hbm.at[(seq * n_pages + pg_i) * chunks_per_page + sub], b_v, sem_b)
            copy_k.wait()
            copy_v.wait()
            copy_b.wait()

            @pl.loop(0, H_F)
            def _(h):
                qc = [q_v[pl.ds(h * D_F + j * ln, ln)] for j in range(D_F // ln)]
                for t in range(SC_CHUNK_TOKENS):
                    r = t * H_F + h
                    u = qc[0] * k_v[r, pl.ds(0, ln)]
                    for j in range(1, D_F // ln):
                        u = u + qc[j] * k_v[r, pl.ds(j * ln, ln)]
                    u_v[t, :] = u
                s = plsc.load_gather(u_v, [lane, jnp.zeros((ln,), jnp.int32)])
                for j in range(1, ln):
                    s = s + plsc.load_gather(u_v, [lane, jnp.full((ln,), j, jnp.int32)])
                s = s * FOX_SCALE + plsc.load_gather(b_v, [lane * H_F + h])
                m_old = m_v[pl.ds(h * ln, ln)]
                m_new = jnp.maximum(m_old, jnp.full((ln,), jnp.max(s), F32))
                alpha = jnp.exp(m_old - m_new)
                p = jnp.exp(s - m_new)
                l_v[pl.ds(h * ln, ln)] = l_v[pl.ds(h * ln, ln)] * alpha + jnp.full((ln,), jnp.sum(p), F32)
                m_v[pl.ds(h * ln, ln)] = m_new
                acc = [acc_v[pl.ds(h * D_F + j * ln, ln)] * alpha for j in range(D_F // ln)]
                for t in range(SC_CHUNK_TOKENS):
                    pt_splat = p.at[jnp.full((ln,), t, jnp.int32)].get(mode="promise_in_bounds")
                    r = t * H_F + h
                    for j in range(D_F // ln):
                        acc[j] = acc[j] + pt_splat * v_v[r, pl.ds(j * ln, ln)]
                for j in range(D_F // ln):
                    acc_v[pl.ds(h * D_F + j * ln, ln)] = acc[j]

            return carry

        lax.fori_loop(0, n_p * chunks_per_page, chunk, 0)
        pltpu.sync_copy(acc_v, acc_hbm.at[w])
        pltpu.sync_copy(m_v, m_hbm.at[w])
        pltpu.sync_copy(l_v, l_hbm.at[w])


def _sc_decode_partials(page_table, q, bias, k_pool, v_pool):
    n_seq, n_pages = page_table.shape
    n_workers = n_seq * SC_PARTS
    assert n_workers <= 32
    rows = SC_CHUNK_TOKENS * H_F
    mesh = plsc.VectorSubcoreMesh(core_axis_name="c", subcore_axis_name="s")
    f32 = lambda *shape: jax.ShapeDtypeStruct(shape, F32)
    run = pl.kernel(
        functools.partial(_sc_decode_body, n_workers=n_workers, n_pages=n_pages),
        out_type=[f32(n_workers, W_F), f32(n_workers, H_F * SC_LANES), f32(n_workers, H_F * SC_LANES)],
        mesh=mesh,
        scratch_types=[pltpu.VMEM((n_pages,), jnp.int32), pltpu.VMEM((W_F,), F32),
                       pltpu.VMEM((rows, D_F), F32), pltpu.VMEM((rows, D_F), F32),
                       pltpu.VMEM((rows,), F32), pltpu.VMEM((SC_LANES, SC_LANES), F32),
                       pltpu.VMEM((W_F,), F32),
                       pltpu.VMEM((H_F * SC_LANES,), F32), pltpu.VMEM((H_F * SC_LANES,), F32),
                       pltpu.SemaphoreType.DMA, pltpu.SemaphoreType.DMA, pltpu.SemaphoreType.DMA],
        compiler_params=pltpu.CompilerParams(needs_layout_passes=False),
        name="fox_decode_sparsecore",
    )
    return run(page_table, q, bias.reshape(-1, rows), k_pool.reshape(-1, D_F), v_pool.reshape(-1, D_F))


def _decode_combine_kernel(q_ref, kn_ref, vn_ref, acc_ref, m_ref, l_ref, o_ref):
    q = q_ref[0].astype(F32)
    s_new = jnp.sum(q * kn_ref[0].astype(F32), axis=-1, keepdims=True) * FOX_SCALE
    m_i = m_ref[0][:, :, 0:1]
    l_i = l_ref[0][:, :, 0:1]
    m_all = jnp.maximum(s_new, jnp.max(m_i, axis=0))
    w_i = jnp.exp(m_i - m_all[None])
    w_new = jnp.exp(s_new - m_all)
    num = jnp.sum(w_i * acc_ref[0], axis=0) + w_new * vn_ref[0].astype(F32)
    den = jnp.sum(w_i * l_i, axis=0) + w_new
    o_ref[0] = (num / den).astype(BF16)


def _decode_combine(q, k_new, v_new, acc, m, l):
    n_seq = q.shape[0]
    head_spec = pl.BlockSpec((1, H_F, D_F), lambda b: (b, 0, 0))
    stat_spec = pl.BlockSpec((1, SC_PARTS, H_F, SC_LANES), lambda b: (b, 0, 0, 0))
    return pl.pallas_call(
        _decode_combine_kernel,
        grid=(n_seq,),
        in_specs=[head_spec, head_spec, head_spec,
                  pl.BlockSpec((1, SC_PARTS, H_F, D_F), lambda b: (b, 0, 0, 0)), stat_spec, stat_spec],
        out_specs=head_spec,
        out_shape=jax.ShapeDtypeStruct((n_seq, H_F, D_F), BF16),
        compiler_params=_params(("parallel",)),
        name="fox_decode_combine",
    )(q, k_new, v_new, acc.reshape(n_seq, SC_PARTS, H_F, D_F),
      m.reshape(n_seq, SC_PARTS, H_F, SC_LANES), l.reshape(n_seq, SC_PARTS, H_F, SC_LANES))


def _merge_kernel(or_ref, of_ref, gr0_ref, gr1_ref, gf0_ref, gf1_ref, x_ref, eg_ref, eb_ref,
                  wr_ref, wf_ref, wo_ref, g1_ref, b1_ref, x1_ref, x1b_ref):
    tm = x_ref.shape[0]
    sub = min(tm, ROW_SUBTILE)
    for r in range(tm // sub):
        rows = slice(r * sub, (r + 1) * sub)
        y_r = jnp.dot(or_ref[rows, :], wr_ref[...], preferred_element_type=F32)
        y_f = jnp.dot(of_ref[rows, :], wf_ref[...], preferred_element_type=F32)
        halves = []
        for idx, (gr_ref, gf_ref) in enumerate(((gr0_ref, gf0_ref), (gr1_ref, gf1_ref))):
            sl = slice(idx * COL_BLOCK, (idx + 1) * COL_BLOCK)
            mixed = (_sigmoid(gr_ref[rows, :].astype(F32)) * y_r[:, sl]
                     + _sigmoid(gf_ref[rows, :].astype(F32)) * y_f[:, sl])
            halves.append(mixed.astype(BF16))
        mix = jnp.dot(jnp.concatenate(halves, axis=1), wo_ref[...], preferred_element_type=F32)
        h = _layer_norm(x_ref[rows, :], eg_ref[...], eb_ref[...])
        x1 = _layer_norm(ALPHA * h + mix, g1_ref[...], b1_ref[...])
        x1_ref[rows, :] = x1
        x1b_ref[rows, :] = x1.astype(BF16)


def _merge(o_r, o_f, z, x, eg, eb, w_r, w_f, w_o, g1, b1, tm):
    m = x.shape[0]
    row = lambda i: (i, 0)
    vec = pl.BlockSpec((1, D_MODEL), lambda i: (0, 0))
    zspec = lambda cb: pl.BlockSpec((tm, COL_BLOCK), lambda i: (i, cb))
    return pl.pallas_call(
        _merge_kernel,
        grid=(m // tm,),
        in_specs=[pl.BlockSpec((tm, W_R), row), pl.BlockSpec((tm, W_F), row),
                  zspec(CB_GR), zspec(CB_GR + 1), zspec(CB_GF), zspec(CB_GF + 1),
                  pl.BlockSpec((tm, D_MODEL), row), vec, vec,
                  _resident((W_R, D_MODEL)), _resident((W_F, D_MODEL)), _resident((D_MODEL, D_MODEL)),
                  vec, vec],
        out_specs=[pl.BlockSpec((tm, D_MODEL), row), pl.BlockSpec((tm, D_MODEL), row)],
        out_shape=[jax.ShapeDtypeStruct((m, D_MODEL), F32), jax.ShapeDtypeStruct((m, D_MODEL), BF16)],
        compiler_params=_params(("parallel",)),
        name="merge_out_ln1",
    )(o_r, o_f, z, z, z, z, x, eg, eb, w_r, w_f, w_o, g1, b1)


def _ffn_up(x_ref, w1_ref, w3_ref):
    xb = x_ref[...]
    a = jnp.dot(xb, w1_ref[...].astype(BF16), preferred_element_type=F32)
    b = jnp.dot(xb, w3_ref[...].astype(BF16), preferred_element_type=F32)
    return (a * _sigmoid(a) * b).astype(BF16)


def _ffn_down(mid, w2_ref, o_ref):
    o_ref[...] += jnp.dot(mid, w2_ref[...].astype(BF16), preferred_element_type=F32)


def _ffn_tile(x_ref, w1_ref, w3_ref, w2_ref, o_ref):
    _ffn_down(_ffn_up(x_ref, w1_ref, w3_ref), w2_ref, o_ref)


def _ffn_kernel(x_ref, w1_ref, w3_ref, w2_ref, o_ref):
    @pl.when(pl.program_id(1) == 0)
    def _():
        o_ref[...] = jnp.zeros_like(o_ref)

    _ffn_tile(x_ref, w1_ref, w3_ref, w2_ref, o_ref)


def _ffn_decode_kernel(pt_ref, x_ref, w1_ref, w3_ref, w2_ref, q_ref, kn_ref, vn_ref, bias_ref, *refs,
                       pages, n_groups, n_decode_steps):
    o_ref = refs[2 * pages]
    d = _DecodeRefs(q_ref, kn_ref, vn_ref, bias_ref, refs[:pages], refs[pages:2 * pages], *refs[2 * pages + 1:])
    j = pl.program_id(1)
    step = pl.program_id(0) * pl.num_programs(1) + j
    g = step % n_groups
    active = step < n_decode_steps

    @pl.when(j == 0)
    def _():
        o_ref[...] = jnp.zeros_like(o_ref)

    pl.when(active & (g == 0))(lambda: _decode_begin(d))
    _ffn_tile(x_ref, w1_ref, w3_ref, w2_ref, o_ref)
    pl.when(active)(lambda: _decode_update(d, _decode_scores(d)))
    pl.when(active & (g == n_groups - 1))(lambda: _decode_end(d))


def _ffn(x1b, w1, w3, w2, tm, tf):
    m = x1b.shape[0]
    return pl.pallas_call(
        _ffn_kernel,
        grid=(m // tm, D_FF // tf),
        in_specs=[pl.BlockSpec((tm, D_MODEL), lambda i, j: (i, 0)),
                  pl.BlockSpec((D_MODEL, tf), lambda i, j: (0, j)),
                  pl.BlockSpec((D_MODEL, tf), lambda i, j: (0, j)),
                  pl.BlockSpec((tf, D_MODEL), lambda i, j: (j, 0))],
        out_specs=pl.BlockSpec((tm, D_MODEL), lambda i, j: (i, 0)),
        out_shape=jax.ShapeDtypeStruct((m, D_MODEL), F32),
        compiler_params=_params(("parallel", "arbitrary")),
        name="swiglu_ffn_rows",
    )(x1b, w1, w3, w2)


def _ffn_with_decode(x1b, w1, w3, w2, tm, tf, page_table, q, k_new, v_new, bias, k_pool, v_pool,
                     pages=DECODE_PAGES_PER_STEP):
    m = x1b.shape[0]
    nj = D_FF // tf
    n_seq, n_pages = page_table.shape
    n_groups = n_pages // pages
    n_decode_steps = n_seq * n_groups
    assert n_decode_steps <= (m // tm) * nj

    def seq_group(i, j):
        s = jnp.minimum(i * nj + j, n_decode_steps - 1)
        return s // n_groups, s % n_groups

    dec_in, dec_out, dec_scratch = _decode_specs(pages, seq_group)
    return pl.pallas_call(
        functools.partial(_ffn_decode_kernel, pages=pages, n_groups=n_groups, n_decode_steps=n_decode_steps),
        grid_spec=pltpu.PrefetchScalarGridSpec(
            num_scalar_prefetch=1,
            grid=(m // tm, nj),
            in_specs=[pl.BlockSpec((tm, D_MODEL), lambda i, j, pt: (i, 0), pipeline_mode=pl.Buffered(1)),
                      pl.BlockSpec((D_MODEL, tf), lambda i, j, pt: (0, j)),
                      pl.BlockSpec((D_MODEL, tf), lambda i, j, pt: (0, j)),
                      pl.BlockSpec((tf, D_MODEL), lambda i, j, pt: (j, 0))] + dec_in,
            out_specs=[pl.BlockSpec((tm, D_MODEL), lambda i, j, pt: (i, 0)), dec_out],
            scratch_shapes=dec_scratch),
        out_shape=[jax.ShapeDtypeStruct((m, D_MODEL), F32),
                   jax.ShapeDtypeStruct((n_seq, H_F, D_F), BF16)],
        compiler_params=_params(("arbitrary", "arbitrary")),
        name="swiglu_ffn_decode",
    )(page_table, x1b, w1, w3, w2, q, k_new, v_new, bias, *([k_pool] * pages), *([v_pool] * pages))


def _ple_kernel(x1_ref, x1b_ref, ffn_ref, p_ref, wpg_ref, wpe_ref, g2_ref, b2_ref, y_ref):
    tm = x1_ref.shape[0]
    sub = min(tm, ROW_SUBTILE)
    for r in range(tm // sub):
        rows = slice(r * sub, (r + 1) * sub)
        pg = jnp.dot(x1b_ref[rows, :], wpg_ref[...], preferred_element_type=F32)
        pe = jnp.dot(p_ref[rows, :].astype(BF16), wpe_ref[...], preferred_element_type=F32)
        y_ref[rows, :] = _layer_norm(ALPHA * x1_ref[rows, :] + ffn_ref[rows, :] + _sigmoid(pg) * pe,
                                     g2_ref[...], b2_ref[...])


def _ple(x1, x1b, ffn, p, w_pg, w_pe, g2, b2, tm):
    m = x1.shape[0]
    row = lambda i: (i, 0)
    vec = pl.BlockSpec((1, D_MODEL), lambda i: (0, 0))
    return pl.pallas_call(
        _ple_kernel,
        grid=(m // tm,),
        in_specs=[pl.BlockSpec((tm, D_MODEL), row), pl.BlockSpec((tm, D_MODEL), row),
                  pl.BlockSpec((tm, D_MODEL), row), pl.BlockSpec((tm, D_PLE), row),
                  _resident((D_MODEL, D_MODEL)), _resident((D_PLE, D_MODEL)), vec, vec],
        out_specs=pl.BlockSpec((tm, D_MODEL), row),
        out_shape=jax.ShapeDtypeStruct((m, D_MODEL), F32),
        compiler_params=_params(("parallel",)),
        name="ple_ln2",
    )(x1, x1b, ffn, p, w_pg, w_pe, g2, b2)


def _rope_tables(pos):
    half = DK_R // 2
    inv = ROPE_BASE ** (-jnp.arange(half, dtype=F32) / half)
    ang = pos.astype(F32)[:, None] * inv[None, :]
    return jnp.cos(ang), jnp.sin(ang)


def kernel(x_prompt, x_sample, state_ret, cache_k, cache_v, cache_logf, page_table, p_prompt, p_sample, ln_emb_g, ln_emb_b, w_in, b_f, gn_g, w_ret_proj, w_fox_proj, w_out, ln1_g, ln1_b, w1, w3, w2, w_pg, w_pe, ln2_g, ln2_b):
    assert w_in.shape[0] == DEPTH == 1
    batch, seq, _ = x_prompt.shape
    dec_b, dec_t, _ = x_sample.shape
    assert dec_t == 1
    n_pages = page_table.shape[1]
    past_len = n_pages * PAGE_SIZE
    n_pool = cache_k.shape[1]

    w_in_t = jnp.swapaxes(w_in[0], 0, 1)
    w_ff_t = w_in_t[FF_LO:FF_HI].astype(BF16)
    w_ff_pad = jnp.pad(w_ff_t, ((0, W_FF_PAD - H_F), (0, 0)))
    bf_col = b_f[0].reshape(H_F, 1)
    vec = lambda v: v.reshape(1, D_MODEL)
    eg, eb = vec(ln_emb_g), vec(ln_emb_b)
    g1, b1, g2, b2 = vec(ln1_g[0]), vec(ln1_b[0]), vec(ln2_g[0]), vec(ln2_b[0])
    w_r, w_f, w_o = w_ret_proj[0].astype(BF16), w_fox_proj[0].astype(BF16), w_out[0].astype(BF16)
    w_pgb, w_peb = w_pg[0].astype(BF16), w_pe[0].astype(BF16)
    log_g = jnp.log1p(-jnp.exp2(-5.0 - jnp.arange(H_R, dtype=F32)))
    cos_p, sin_p = _rope_tables(jnp.arange(seq))
    cos_s, sin_s = _rope_tables(jnp.full((dec_b,), past_len))

    xp = x_prompt.reshape(batch * seq, D_MODEL)
    xs = x_sample.reshape(dec_b, D_MODEL)

    h_p = _entry_ln(xp, eg, eb, tm=512)
    z_p, fk_p, fv_p, ff_p, z_s, fk_s, fv_s = _inproj(h_p, xs, eg, eb, w_in_t, w_ff_pad,
                                                     cos_p, sin_p, cos_s, sin_s, tm=IN_TM)
    lf_p, c2_row = _logf_prompt(ff_p, bf_col, seq)
    lf_s = _logf_sample(xs, eg, eb, w_ff_t, bf_col)

    o_r_p, st_p = _ret_prompt(z_p, log_g, gn_g[0], batch, seq)
    o_f_p = _fox_prompt(z_p, c2_row, batch, seq)

    z3 = z_s.reshape(dec_b, 1, z_s.shape[1])
    o_r_s, st_s = _ret_sample(z3, log_g, gn_g[0], state_ret[0])
    width = PAGE_SIZE * H_F
    lf_new = jnp.swapaxes(lf_s, 1, 2).reshape(dec_b, H_F)
    lnew = jnp.tile(lf_new, (1, PAGE_SIZE)).reshape(dec_b, 1, width)
    bias = _decode_bias(page_table, cache_logf[0].reshape(n_pool, width), lnew)
    heads = lambda cb: z_s[:, cb * COL_BLOCK:(cb + 1) * COL_BLOCK].reshape(dec_b, H_F, D_F)
    decode_args = (heads(CB_FQ), heads(CB_FK), heads(CB_FV), bias,
                   cache_k[0].reshape(n_pool, width, D_F), cache_v[0].reshape(n_pool, width, D_F))

    x1_p, x1b_p = _merge(o_r_p, o_f_p, z_p, xp, eg, eb, w_r, w_f, w_o, g1, b1, tm=MERGE_TM)
    n_ride = dec_b - 32 // SC_PARTS
    ffn_p, o_f_ride = _ffn_with_decode(x1b_p, w1[0], w3[0], w2[0], FFN_TM, FFN_TF, page_table[:n_ride],
                                       *(a[:n_ride] for a in decode_args[:4]), *decode_args[4:])
    q_r, kn_r, vn_r, bias_r = (a[n_ride:] for a in decode_args[:4])
    parts = _sc_decode_partials(page_table[n_ride:], q_r.reshape(-1, W_F).astype(F32), bias_r, *decode_args[4:])
    o_f_rest = _decode_combine(q_r, kn_r, vn_r, *parts)
    o_f_s = jnp.concatenate([o_f_ride, o_f_rest], axis=0)
    x1_s, x1b_s = _merge(o_r_s.reshape(dec_b, W_R), o_f_s.reshape(dec_b, W_F), z_s, xs, eg, eb,
                         w_r, w_f, w_o, g1, b1, tm=dec_b)
    ffn_s = _ffn(x1b_s, w1[0], w3[0], w2[0], tm=dec_b, tf=FFN_TF)
    y_p = _ple(x1_p, x1b_p, ffn_p, p_prompt[0].reshape(batch * seq, D_PLE), w_pgb, w_peb, g2, b2, tm=PLE_TM)
    y_s = _ple(x1_s, x1b_s, ffn_s, p_sample[0].reshape(dec_b, D_PLE), w_pgb, w_peb, g2, b2, tm=dec_b)

    return (y_p.reshape(batch, seq, D_MODEL),
            y_s.reshape(dec_b, 1, D_MODEL),
            st_p[None],
            fk_p.reshape(1, batch, seq, H_F, D_F),
            fv_p.reshape(1, batch, seq, H_F, D_F),
            jnp.swapaxes(lf_p, 1, 2)[None],
            st_s[None],
            fk_s.reshape(1, dec_b, 1, H_F, D_F),
            fv_s.reshape(1, dec_b, 1, H_F, D_F),
            jnp.swapaxes(lf_s, 1, 2).reshape(1, dec_b, 1, H_F))
```

```python
import functools

import jax
import jax.numpy as jnp
import numpy as np
from jax import lax
from jax.experimental import pallas as pl
from jax.experimental.pallas import tpu as pltpu
from jax.experimental.pallas import tpu_sc as plsc

F32 = jnp.float32
BF16 = jnp.bfloat16

D_MODEL = 2048
DEPTH = 1
PAGE_SIZE = 128
H_R = 4
DK_R = 256
DV_R = 256
W_R = H_R * DV_R
CHUNK = 256
ROPE_BASE = 10000.0
GN_EPS = 1e-6
H_F = 8
D_F = 128
W_F = H_F * D_F
D_FF = 5632
D_PLE = 256
LN_EPS = 1e-5
ALPHA = (2.0 * DEPTH) ** 0.25
FOX_SCALE = D_F ** -0.5
LOG2E = float(np.log2(np.e))

COL_BLOCK = 1024
N_MAIN = 11 * COL_BLOCK
CB_RQ, CB_RK, CB_RV, CB_RG, CB_FQ, CB_FK, CB_FV, CB_GR, CB_GF = 0, 1, 2, 3, 4, 5, 6, 7, 9
FF_LO, FF_HI = 7 * COL_BLOCK, 7 * COL_BLOCK + H_F

IN_TM, IN_TN = 1024, 1024
W_FF_PAD = 128
N_A_BLOCKS = FF_LO // IN_TN
FFN_TM, FFN_TF = 1024, 256
MERGE_TM, PLE_TM, ROW_SUBTILE = 256, 512, 256

VMEM_LIMIT = 56 * 1024 * 1024
DECODE_PAGES_PER_STEP = 8
RET_SAMPLE_SEQS = 4

NT_DIMS = (((1,), (1,)), ((), ()))
TN_DIMS = (((0,), (0,)), ((), ()))


def _params(semantics):
    return pltpu.CompilerParams(dimension_semantics=semantics, vmem_limit_bytes=VMEM_LIMIT)


def _layer_norm(x, g, b):
    mu = jnp.mean(x, axis=-1, keepdims=True)
    xc = x - mu
    var = jnp.mean(xc * xc, axis=-1, keepdims=True)
    return xc * lax.rsqrt(var + LN_EPS) * g + b


def _sigmoid(x):
    return 1.0 / (1.0 + jnp.exp(-x))


def _log_sigmoid(x):
    return jnp.minimum(x, 0.0) - jnp.log(1.0 + jnp.exp(-jnp.abs(x)))


def _resident(shape):
    return pl.BlockSpec(shape, lambda *_: (0,) * len(shape), pipeline_mode=pl.Buffered(1))


def _ln_kernel(x_ref, g_ref, b_ref, o_ref):
    o_ref[...] = _layer_norm(x_ref[...], g_ref[...], b_ref[...]).astype(o_ref.dtype)


def _entry_ln(x, g, b, tm):
    m = x.shape[0]
    return pl.pallas_call(
        _ln_kernel,
        grid=(m // tm,),
        in_specs=[pl.BlockSpec((tm, D_MODEL), lambda i: (i, 0)),
                  pl.BlockSpec((1, D_MODEL), lambda i: (0, 0)),
                  pl.BlockSpec((1, D_MODEL), lambda i: (0, 0))],
        out_specs=pl.BlockSpec((tm, D_MODEL), lambda i: (i, 0)),
        out_shape=jax.ShapeDtypeStruct((m, D_MODEL), BF16),
        compiler_params=_params(("parallel",)),
        name="entry_ln",
    )(x, g, b)


def _inproj_block(h, w_ref, z_ref, rope=None, f32_ref=None):
    half = DK_R // 2
    for hd in range(IN_TN // DK_R):
        lo = hd * DK_R
        acc = lax.dot_general(h, w_ref[lo:lo + DK_R, :].astype(BF16), NT_DIMS, preferred_element_type=F32)
        if rope is None:
            z_ref[:, lo:lo + DK_R] = acc.astype(BF16)
        else:
            cos, sin, scale = rope
            x1 = acc[:, :half]
            x2 = acc[:, half:]
            z_ref[:, lo:lo + half] = ((x1 * cos - x2 * sin) * scale).astype(BF16)
            z_ref[:, lo + half:lo + DK_R] = ((x2 * cos + x1 * sin) * scale).astype(BF16)
        if f32_ref is not None:
            f32_ref[:, lo:lo + DK_R] = acc


def _inproj_kernel(h_ref, xs_ref, eg_ref, eb_ref, w_ref, wff_ref, cos_ref, sin_ref, coss_ref, sins_ref,
                   z_ref, fk_ref, fv_ref, ff_ref, zs_ref, fks_ref, fvs_ref):
    i = pl.program_id(0)
    j = pl.program_id(1)
    per = COL_BLOCK // IN_TN

    @pl.when(j == 0)
    def _():
        ff_ref[...] = lax.dot_general(h_ref[...], wff_ref[...], NT_DIMS, preferred_element_type=F32)

    def section(lo, hi, rope_scale=None, f32_refs=(None, None)):
        @pl.when((j >= lo) & (j < hi))
        def _():
            rope = None if rope_scale is None else (cos_ref[...], sin_ref[...], rope_scale)
            _inproj_block(h_ref[...], w_ref, z_ref, rope, f32_refs[0])

            @pl.when(i == 0)
            def _():
                hs = _layer_norm(xs_ref[...], eg_ref[...], eb_ref[...]).astype(BF16)
                rope_s = None if rope_scale is None else (coss_ref[...], sins_ref[...], rope_scale)
                _inproj_block(hs, w_ref, zs_ref, rope_s, f32_refs[1])

    @pl.when(i > 0)
    def _():
        zs_ref[...] = jnp.zeros_like(zs_ref)

    section(CB_RQ * per, CB_RK * per, rope_scale=1.0)
    section(CB_RK * per, CB_RV * per, rope_scale=DK_R ** -0.5)
    section(CB_RV * per, CB_FK * per)
    section(CB_FK * per, CB_FV * per, f32_refs=(fk_ref, fks_ref))
    section(CB_FV * per, N_A_BLOCKS, f32_refs=(fv_ref, fvs_ref))
    section(N_A_BLOCKS, N_MAIN // IN_TN)


def _inproj(h, xs, eg, eb, w_in_t, w_ff_pad, cos, sin, cos_s, sin_s, tm):
    m = h.shape[0]
    ms = xs.shape[0]
    n_pos_blocks = cos.shape[0] // tm
    nj = N_MAIN // IN_TN
    per = COL_BLOCK // IN_TN
    half = DK_R // 2
    clamp = lambda j, cb: jnp.clip(j - cb * per, 0, per - 1)
    row = lambda i, j: (i, 0)
    fixed = lambda i, j: (0, 0)
    return pl.pallas_call(
        _inproj_kernel,
        grid=(m // tm, nj),
        in_specs=[pl.BlockSpec((tm, D_MODEL), row),
                  pl.BlockSpec((ms, D_MODEL), fixed),
                  pl.BlockSpec((1, D_MODEL), fixed),
                  pl.BlockSpec((1, D_MODEL), fixed),
                  pl.BlockSpec((pl.Element(IN_TN), pl.Element(D_MODEL)),
                               lambda i, j: (pl.multiple_of(IN_TN * j + jnp.where(j < N_A_BLOCKS, 0, H_F), 8), 0)),
                  pl.BlockSpec((W_FF_PAD, D_MODEL), fixed),
                  pl.BlockSpec((tm, half), lambda i, j: (i % n_pos_blocks, 0)),
                  pl.BlockSpec((tm, half), lambda i, j: (i % n_pos_blocks, 0)),
                  pl.BlockSpec((ms, half), fixed),
                  pl.BlockSpec((ms, half), fixed)],
        out_specs=[pl.BlockSpec((tm, IN_TN), lambda i, j: (i, j)),
                   pl.BlockSpec((tm, IN_TN), lambda i, j: (i, clamp(j, CB_FK))),
                   pl.BlockSpec((tm, IN_TN), lambda i, j: (i, clamp(j, CB_FV))),
                   pl.BlockSpec((tm, W_FF_PAD), row),
                   pl.BlockSpec((ms, IN_TN), lambda i, j: (0, jnp.where(i == 0, j, nj))),
                   pl.BlockSpec((ms, IN_TN), lambda i, j: (0, jnp.where(i == 0, clamp(j, CB_FK), per - 1))),
                   pl.BlockSpec((ms, IN_TN), lambda i, j: (0, jnp.where(i == 0, clamp(j, CB_FV), per - 1)))],
        out_shape=[jax.ShapeDtypeStruct((m, N_MAIN), BF16),
                   jax.ShapeDtypeStruct((m, W_F), F32),
                   jax.ShapeDtypeStruct((m, W_F), F32),
                   jax.ShapeDtypeStruct((m, W_FF_PAD), F32),
                   jax.ShapeDtypeStruct((ms, N_MAIN + IN_TN), BF16),
                   jax.ShapeDtypeStruct((ms, W_F), F32),
                   jax.ShapeDtypeStruct((ms, W_F), F32)],
        compiler_params=_params(("arbitrary", "arbitrary")),
        name="in_proj",
    )(h, xs, eg, eb, w_in_t, w_ff_pad, cos, sin, cos_s, sin_s)


def _cumsum_lanes(x):
    n = x.shape[-1]
    lane = lax.broadcasted_iota(jnp.int32, x.shape, x.ndim - 1)
    s = 1
    while s < n:
        x = x + jnp.where(lane >= s, pltpu.roll(x, s, axis=x.ndim - 1), 0.0)
        s *= 2
    return x


def _logf_prompt_kernel(ff_ref, bf_ref, lf_ref, c_ref):
    ff = ff_ref[...].T[:H_F, :]
    lf = _log_sigmoid(ff + bf_ref[...])
    lf_ref[0] = lf
    c_ref[0] = _cumsum_lanes(lf) * LOG2E


def _logf_sample_kernel(x_ref, eg_ref, eb_ref, w_ref, bf_ref, lf_ref):
    h = _layer_norm(x_ref[...], eg_ref[...], eb_ref[...]).astype(BF16)
    ff = lax.dot_general(w_ref[...], h, NT_DIMS, preferred_element_type=F32)
    lf_ref[0] = _log_sigmoid(ff + bf_ref[...])


def _logf_prompt(ff, bf_col, seq):
    nb = ff.shape[0] // seq
    shape = jax.ShapeDtypeStruct((nb, H_F, seq), F32)
    spec = pl.BlockSpec((1, H_F, seq), lambda b: (b, 0, 0))
    return pl.pallas_call(
        _logf_prompt_kernel,
        grid=(nb,),
        in_specs=[pl.BlockSpec((seq, W_FF_PAD), lambda b: (b, 0)),
                  pl.BlockSpec((H_F, 1), lambda b: (0, 0))],
        out_specs=[spec, spec],
        out_shape=[shape, shape],
        compiler_params=_params(("parallel",)),
        name="log_forget_cumsum",
    )(ff, bf_col)


def _logf_sample(xs, eg, eb, w_ff_t, bf_col):
    ms = xs.shape[0]
    fixed = lambda b: (0, 0)
    return pl.pallas_call(
        _logf_sample_kernel,
        grid=(1,),
        in_specs=[pl.BlockSpec((ms, D_MODEL), fixed), pl.BlockSpec((1, D_MODEL), fixed),
                  pl.BlockSpec((1, D_MODEL), fixed), pl.BlockSpec((H_F, D_MODEL), fixed),
                  pl.BlockSpec((H_F, 1), fixed)],
        out_specs=pl.BlockSpec((1, H_F, ms), lambda b: (0, 0, 0)),
        out_shape=jax.ShapeDtypeStruct((1, H_F, ms), F32),
        compiler_params=_params(("arbitrary",)),
        name="log_forget",
    )(xs, eg, eb, w_ff_t, bf_col)


def _group_norm_gate(o, gate_bf16, gn_row):
    mu = jnp.mean(o, axis=-1, keepdims=True)
    oc = o - mu
    var = jnp.mean(oc * oc, axis=-1, keepdims=True)
    gate = gate_bf16.astype(F32)
    return oc * lax.rsqrt(var + GN_EPS) * gn_row * (gate * _sigmoid(gate))


def _ret_prompt_kernel(lg_ref, q_ref, k_ref, v_ref, g_ref, gn_ref, o_ref, st_ref,
                       dec_ref, cross_ref, kdec_ref, carry_ref):
    b = pl.program_id(0)
    c = pl.program_id(1)
    n = q_ref.shape[0]

    @pl.when((b == 0) & (c == 0))
    def _():
        ri = lax.broadcasted_iota(jnp.int32, (n, n), 0)
        ci = lax.broadcasted_iota(jnp.int32, (n, n), 1)
        causal = ri >= ci
        diff = jnp.where(causal, (ri - ci).astype(F32), 0.0)
        pos = lax.broadcasted_iota(jnp.int32, (n, DK_R), 0).astype(F32)
        for hd in range(H_R):
            lg = lg_ref[hd]
            dec_ref[hd] = jnp.where(causal, jnp.exp(lg * diff), 0.0)
            cross_ref[hd] = jnp.exp(lg * (pos + 1.0))
            kdec_ref[hd] = jnp.exp(lg * (n - 1.0 - pos))
            carry_ref[hd] = jnp.exp(jnp.full((1, DV_R), lg * n, F32))

    @pl.when(c == 0)
    def _():
        st_ref[...] = jnp.zeros_like(st_ref)

    heads = lambda ref: jnp.stack([ref[:, hd * DK_R:(hd + 1) * DK_R] for hd in range(H_R)], axis=0)
    bmm = lambda a, b_, ca, cb: lax.dot_general(a, b_, (((ca,), (cb,)), ((0,), (0,))), preferred_element_type=F32)
    q3, k3, v3 = heads(q_ref), heads(k_ref), heads(v_ref)
    sc = bmm(q3, k3, 2, 2) * dec_ref[...]
    o = bmm(sc.astype(BF16), v3, 2, 1)
    state = st_ref[0]
    o = o + bmm(q3, state.astype(BF16), 2, 1) * cross_ref[...]
    kd = (k3.astype(F32) * kdec_ref[...]).astype(BF16)
    st_ref[0] = carry_ref[...] * state + bmm(kd, v3, 1, 1)
    out = _group_norm_gate(o, heads(g_ref), gn_ref[...][:, None, :]).astype(BF16)
    for hd in range(H_R):
        o_ref[:, hd * DV_R:(hd + 1) * DV_R] = out[hd]


def _ret_prompt(z, log_g, gn_g, batch, seq):
    nc = seq // CHUNK
    row = lambda b, c: b * nc + c
    zspec = lambda cb: pl.BlockSpec((CHUNK, COL_BLOCK), lambda b, c: (row(b, c), cb))
    return pl.pallas_call(
        _ret_prompt_kernel,
        grid=(batch, nc),
        in_specs=[pl.BlockSpec(memory_space=pltpu.SMEM),
                  zspec(CB_RQ), zspec(CB_RK), zspec(CB_RV), zspec(CB_RG),
                  pl.BlockSpec((H_R, DV_R), lambda b, c: (0, 0))],
        out_specs=[pl.BlockSpec((CHUNK, W_R), lambda b, c: (row(b, c), 0)),
                   pl.BlockSpec((1, H_R, DK_R, DV_R), lambda b, c: (b, 0, 0, 0))],
        out_shape=[jax.ShapeDtypeStruct((batch * seq, W_R), BF16),
                   jax.ShapeDtypeStruct((batch, H_R, DK_R, DV_R), F32)],
        scratch_shapes=[pltpu.VMEM((H_R, CHUNK, CHUNK), F32),
                        pltpu.VMEM((H_R, CHUNK, DV_R), F32),
                        pltpu.VMEM((H_R, CHUNK, DK_R), F32),
                        pltpu.VMEM((H_R, 1, DV_R), F32)],
        compiler_params=_params(("arbitrary", "arbitrary")),
        name="retention_prompt",
    )(log_g, z, z, z, z, gn_g)


def _ret_sample_kernel(lg_ref, q_ref, k_ref, v_ref, g_ref, gn_ref, st_ref, o_ref, ns_ref):
    row0 = lax.broadcasted_iota(jnp.int32, (8, DK_R), 0) == 0
    for sq in range(q_ref.shape[0]):
        for hd in range(H_R):
            lg = lg_ref[hd]
            sl = slice(hd * DK_R, (hd + 1) * DK_R)
            q = q_ref[sq, :, sl]
            k = k_ref[sq, :, sl]
            v = v_ref[sq, :, sl]
            qf = q.astype(F32)
            kf = k.astype(F32)
            vf = v.astype(F32)
            state = st_ref[sq, hd]
            decay = jnp.exp(jnp.full((1, DV_R), lg, F32))
            score = jnp.sum(qf * kf, axis=-1, keepdims=True)
            q8 = jnp.broadcast_to(q, (8, DK_R))
            qs = jnp.dot(q8, state.astype(BF16), preferred_element_type=F32)[0:1, :]
            o = score * vf + qs * decay
            k8 = jnp.where(row0, jnp.broadcast_to(kf, (8, DK_R)), 0.0).astype(BF16)
            v8 = jnp.broadcast_to(v, (8, DV_R))
            ns_ref[sq, hd] = decay * state + lax.dot_general(k8, v8, TN_DIMS, preferred_element_type=F32)
            o_ref[sq, :, sl] = _group_norm_gate(o, g_ref[sq, :, sl], gn_ref[hd:hd + 1, :]).astype(BF16)


def _ret_sample(z3, log_g, gn_g, state):
    nb = z3.shape[0]
    per = RET_SAMPLE_SEQS
    zspec = lambda cb: pl.BlockSpec((per, 1, COL_BLOCK), lambda b: (b, 0, cb))
    st_spec = pl.BlockSpec((per, H_R, DK_R, DV_R), lambda b: (b, 0, 0, 0))
    return pl.pallas_call(
        _ret_sample_kernel,
        grid=(nb // per,),
        in_specs=[pl.BlockSpec(memory_space=pltpu.SMEM),
                  zspec(CB_RQ), zspec(CB_RK), zspec(CB_RV), zspec(CB_RG),
                  pl.BlockSpec((H_R, DV_R), lambda b: (0, 0)),
                  st_spec],
        out_specs=[pl.BlockSpec((per, 1, W_R), lambda b: (b, 0, 0)), st_spec],
        out_shape=[jax.ShapeDtypeStruct((nb, 1, W_R), BF16),
                   jax.ShapeDtypeStruct(state.shape, F32)],
        compiler_params=_params(("parallel",)),
        name="retention_sample",
    )(log_g, z3, z3, z3, z3, gn_g, state)


def _split3(x):
    hi = x.astype(BF16).astype(F32)
    r1 = x - hi
    mid = r1.astype(BF16).astype(F32)
    return hi, mid, r1 - mid


def _fox_prompt_kernel(q_ref, k_ref, v_ref, crow_ref, o_ref, vt_ref, kx_ref, m_ref, l_ref, acc_ref,
                       *, tq):
    qi = pl.program_id(1)
    nk = vt_ref.shape[1]
    score_scale = FOX_SCALE * LOG2E

    @pl.when(qi == 0)
    def _():
        sel_r = lax.broadcasted_iota(jnp.int32, (8, D_F), 0)
        sel_l = lax.broadcasted_iota(jnp.int32, (8, D_F), 1)
        sel = jnp.where((sel_r == sel_l) & (sel_r < 3), 1.0, 0.0).astype(BF16)
        for hd in range(H_F):
            for kj in range(nk):
                v_tile = v_ref[kj * tq:(kj + 1) * tq, hd * D_F:(hd + 1) * D_F]
                vt_ref[hd, kj] = v_tile.astype(F32).T.astype(BF16)
                parts = _split3(crow_ref[0, hd, kj:kj + 1, :] * (-1.0 / score_scale))
                rows = jnp.concatenate(list(parts) + [jnp.zeros((5, tq), F32)], axis=0).astype(BF16)
                kx_ref[hd, kj] = lax.dot_general(rows, sel, TN_DIMS, preferred_element_type=F32).astype(BF16)

    m_ref[...] = jnp.full(m_ref.shape, -jnp.inf, F32)
    l_ref[...] = jnp.zeros_like(l_ref)
    acc_ref[...] = jnp.zeros_like(acc_ref)
    key_i = lax.broadcasted_iota(jnp.int32, (tq, tq), 0)
    qry_i = lax.broadcasted_iota(jnp.int32, (tq, tq), 1)
    visible = key_i <= qry_i

    heads = lambda ref, rows: jnp.stack([ref[rows, hd * D_F:(hd + 1) * D_F] for hd in range(H_F)], axis=0)
    ones3 = jnp.where(lax.broadcasted_iota(jnp.int32, (H_F, tq, D_F), 2) < 3, 1.0, 0.0).astype(BF16)
    q3 = jnp.concatenate([heads(q_ref, slice(None)), ones3], axis=2)
    c_q = crow_ref[0, :, pl.ds(qi, 1), :]

    def tile(kj, on_diagonal):
        ks = pl.multiple_of(kj * tq, tq)
        k3 = jnp.concatenate([heads(k_ref, pl.ds(ks, tq)), kx_ref[:, kj]], axis=2)
        s = lax.dot_general(k3, q3, (((2,), (2,)), ((0,), (0,))), preferred_element_type=F32)
        t = s * score_scale + c_q
        if on_diagonal:
            t = jnp.where(visible[None], t, -jnp.inf)
        m_old = m_ref[...]
        m_new = jnp.maximum(m_old, jnp.max(t, axis=1, keepdims=True))
        alpha = jnp.exp2(m_old - m_new)
        p = jnp.exp2(t - m_new)
        l_ref[...] = alpha * l_ref[...] + jnp.sum(p, axis=1, keepdims=True)
        m_ref[...] = m_new
        pv = lax.dot_general(vt_ref[:, kj], p.astype(BF16), (((2,), (1,)), ((0,), (0,))),
                             preferred_element_type=F32)
        acc_ref[...] = alpha * acc_ref[...] + pv

    def body(kj, carry):
        tile(kj, False)
        return carry

    lax.fori_loop(0, qi, body, 0)
    tile(qi, True)
    for hd in range(H_F):
        o_ref[:, hd * D_F:(hd + 1) * D_F] = (acc_ref[hd] / l_ref[hd]).T.astype(BF16)


def _fox_prompt(z, c2_row, batch, seq, tq=256):
    nq = seq // tq
    return pl.pallas_call(
        functools.partial(_fox_prompt_kernel, tq=tq),
        grid=(batch, nq),
        in_specs=[pl.BlockSpec((tq, COL_BLOCK), lambda b, i: (b * nq + i, CB_FQ)),
                  pl.BlockSpec((seq, COL_BLOCK), lambda b, i: (b, CB_FK)),
                  pl.BlockSpec((seq, COL_BLOCK), lambda b, i: (b, CB_FV)),
                  pl.BlockSpec((1, H_F, nq, tq), lambda b, i: (b, 0, 0, 0))],
        out_specs=pl.BlockSpec((tq, W_F), lambda b, i: (b * nq + i, 0)),
        out_shape=jax.ShapeDtypeStruct((batch * seq, W_F), BF16),
        scratch_shapes=[pltpu.VMEM((H_F, nq, D_F, tq), BF16),
                        pltpu.VMEM((H_F, nq, tq, D_F), BF16),
                        pltpu.VMEM((H_F, 1, tq), F32),
                        pltpu.VMEM((H_F, 1, tq), F32),
                        pltpu.VMEM((H_F, D_F, tq), F32)],
        compiler_params=_params(("parallel", "arbitrary")),
        name="fox_prompt",
    )(z, z, z, c2_row.reshape(batch, H_F, nq, tq))


def _decode_bias_kernel(pt_ref, pool_ref, lnew_ref, bias_ref, rows_ref, *, n_pages):
    b = pl.program_id(0)
    width = PAGE_SIZE * H_F
    for p in range(n_pages):
        rows_ref[p:p + 1, :] = pool_ref[pl.ds(pt_ref[b, p], 1), :]
    x = rows_ref[...]
    lane = lax.broadcasted_iota(jnp.int32, x.shape, 1)
    within = x
    total = x
    s = H_F
    while s < width:
        within = within + jnp.where(lane >= s, pltpu.roll(within, s, axis=1), 0.0)
        total = total + pltpu.roll(total, s, axis=1)
        s *= 2
    rest = total - within
    run = lnew_ref[0]
    for p in reversed(range(n_pages)):
        bias_ref[0, p:p + 1, :] = run + rest[p:p + 1, :]
        run = run + total[p:p + 1, :]


def _decode_bias(page_table, logf_pool, lnew):
    nb, n_pages = page_table.shape
    width = PAGE_SIZE * H_F
    return pl.pallas_call(
        functools.partial(_decode_bias_kernel, n_pages=n_pages),
        grid_spec=pltpu.PrefetchScalarGridSpec(
            num_scalar_prefetch=1,
            grid=(nb,),
            in_specs=[pl.BlockSpec(logf_pool.shape, lambda b, pt: (0, 0), pipeline_mode=pl.Buffered(1)),
                      pl.BlockSpec((1, 1, width), lambda b, pt: (b, 0, 0))],
            out_specs=pl.BlockSpec((1, n_pages, width), lambda b, pt: (b, 0, 0)),
            scratch_shapes=[pltpu.VMEM((n_pages, width), F32)]),
        out_shape=jax.ShapeDtypeStruct((nb, n_pages, width), F32),
        compiler_params=_params(("arbitrary",)),
        name="decode_forget_bias",
    )(page_table, logf_pool, lnew)


class _DecodeRefs:
    def __init__(self, q_ref, kn_ref, vn_ref, bias_ref, k_refs, v_refs, o_ref, m_ref, l_ref, acc_ref):
        self.q, self.kn, self.vn, self.bias = q_ref, kn_ref, vn_ref, bias_ref
        self.k, self.v = k_refs, v_refs
        self.o, self.m, self.l, self.acc = o_ref, m_ref, l_ref, acc_ref


def _decode_begin(d):
    q = d.q[0].astype(F32)
    s_new = jnp.sum(q * d.kn[0].astype(F32), axis=-1, keepdims=True) * FOX_SCALE
    d.m[...] = jnp.broadcast_to(s_new, d.m.shape)
    d.l[...] = jnp.ones_like(d.l)
    d.acc[...] = d.vn[0].astype(F32)


def _decode_scores(d):
    width = PAGE_SIZE * H_F
    row = lax.broadcasted_iota(jnp.int32, (H_F, width), 0)
    lane = lax.broadcasted_iota(jnp.int32, (H_F, width), 1)
    own = (lane % H_F) == row
    q = d.q[0]
    s_parts = []
    for i, k_ref in enumerate(d.k):
        kt = k_ref[...].astype(BF16)
        s = lax.dot_general(q, kt, NT_DIMS, preferred_element_type=F32)
        s_parts.append(jnp.where(own, s * FOX_SCALE + d.bias[0, i:i + 1, :], -jnp.inf))
    return s_parts


def _decode_update(d, s_parts):
    m_old = d.m[:, 0:1]
    m_new = m_old
    for s in s_parts:
        m_new = jnp.maximum(m_new, jnp.max(s, axis=-1, keepdims=True))
    alpha = jnp.exp(m_old - m_new)
    l_new = alpha * d.l[:, 0:1]
    pv = jnp.zeros((H_F, D_F), F32)
    for s, v_ref in zip(s_parts, d.v):
        p = jnp.exp(s - m_new)
        l_new = l_new + jnp.sum(p, axis=-1, keepdims=True)
        pv = pv + jnp.dot(p.astype(BF16), v_ref[...].astype(BF16), preferred_element_type=F32)
    d.m[...] = jnp.broadcast_to(m_new, d.m.shape)
    d.l[...] = jnp.broadcast_to(l_new, d.l.shape)
    d.acc[...] = alpha * d.acc[...] + pv


def _decode_end(d):
    d.o[0] = (d.acc[...] / d.l[:, 0:1]).astype(BF16)


def _fox_decode_kernel(pt_ref, q_ref, kn_ref, vn_ref, bias_ref, *refs, pages):
    d = _DecodeRefs(q_ref, kn_ref, vn_ref, bias_ref, refs[:pages], refs[pages:2 * pages], *refs[2 * pages:])
    g = pl.program_id(1)
    pl.when(g == 0)(lambda: _decode_begin(d))
    _decode_update(d, _decode_scores(d))
    pl.when(g == pl.num_programs(1) - 1)(lambda: _decode_end(d))


def _decode_specs(pages, seq_group):
    width = PAGE_SIZE * H_F

    def head_map(*args):
        return (seq_group(*args[:-1])[0], 0, 0)

    def bias_map(*args):
        b, g = seq_group(*args[:-1])
        return (b, g, 0)

    def page_spec(i):
        def page_map(*args):
            b, g = seq_group(*args[:-1])
            return (args[-1][b, g * pages + i], 0, 0)
        return pl.BlockSpec((None, width, D_F), page_map)

    head_spec = pl.BlockSpec((1, H_F, D_F), head_map)
    in_specs = ([head_spec, head_spec, head_spec, pl.BlockSpec((1, pages, width), bias_map)]
                + [page_spec(i) for i in range(pages)] * 2)
    scratch = [pltpu.VMEM((H_F, D_F), F32), pltpu.VMEM((H_F, D_F), F32), pltpu.VMEM((H_F, D_F), F32)]
    return in_specs, head_spec, scratch


def _fox_decode(page_table, q, k_new, v_new, bias, k_pool, v_pool, pages=DECODE_PAGES_PER_STEP):
    nb, n_pages = page_table.shape
    in_specs, out_spec, scratch = _decode_specs(pages, lambda b, g: (b, g))
    return pl.pallas_call(
        functools.partial(_fox_decode_kernel, pages=pages),
        grid_spec=pltpu.PrefetchScalarGridSpec(
            num_scalar_prefetch=1,
            grid=(nb, n_pages // pages),
            in_specs=in_specs,
            out_specs=out_spec,
            scratch_shapes=scratch),
        out_shape=jax.ShapeDtypeStruct((nb, H_F, D_F), BF16),
        compiler_params=_params(("parallel", "arbitrary")),
        name="fox_decode",
    )(page_table, q, k_new, v_new, bias, *([k_pool] * pages), *([v_pool] * pages))


SC_LANES = 16
SC_PARTS = 3
SC_CHUNK_TOKENS = 16
SC_NEG = -1e30


def _sc_decode_body(pt_hbm, q_hbm, b_hbm, k_hbm, v_hbm, acc_hbm, m_hbm, l_hbm,
                    pt_v, q_v, k_v, v_v, b_v, u_v, acc_v, m_v, l_v, sem_k, sem_v, sem_b,
                    *, n_workers, n_pages):
    ln = SC_LANES
    chunks_per_page = PAGE_SIZE // SC_CHUNK_TOKENS
    rows = SC_CHUNK_TOKENS * H_F
    w = lax.axis_index("s") * 2 + lax.axis_index("c")

    @pl.when(w < n_workers)
    def _():
        seq = lax.shift_right_logical(w * 11, 5)
        part = w - seq * SC_PARTS
        base_pages = n_pages // SC_PARTS
        p_lo = part * base_pages + jnp.minimum(part, n_pages % SC_PARTS)
        n_p = base_pages + jnp.where(part < n_pages % SC_PARTS, 1, 0)
        pltpu.sync_copy(pt_hbm.at[seq], pt_v)
        pltpu.sync_copy(q_hbm.at[seq], q_v)
        lane = lax.iota(jnp.int32, ln)

        @pl.loop(0, W_F, step=ln)
        def _(c0):
            acc_v[pl.ds(c0, ln)] = jnp.zeros((ln,), F32)

        @pl.loop(0, H_F * ln, step=ln)
        def _(c0):
            m_v[pl.ds(c0, ln)] = jnp.full((ln,), SC_NEG, F32)
            l_v[pl.ds(c0, ln)] = jnp.zeros((ln,), F32)

        def chunk(ci, carry):
            pg_i = p_lo + lax.shift_right_logical(ci, 3)
            sub = ci & (chunks_per_page - 1)
            base16 = lax.shift_right_logical(pg_i, 4) * ln
            page = jnp.max(jnp.where(lane == pg_i - base16, pt_v[pl.ds(base16, ln)], 0))
            row0 = pl.multiple_of(page * (PAGE_SIZE * H_F) + sub * rows, rows)
            copy_k = pltpu.async_copy(k_hbm.at[pl.ds(row0, rows)], k_v, sem_k)
            copy_v = pltpu.async_copy(v_hbm.at[pl.ds(row0, rows)], v_v, sem_v)
            copy_b = pltpu.async_copy(b_hbm.at[(seq * n_pages + pg_i) * chunks_per_page + sub], b_v, sem_b)
            copy_k.wait()
            copy_v.wait()
            copy_b.wait()

            @pl.loop(0, H_F)
            def _(h):
                qc = [q_v[pl.ds(h * D_F + j * ln, ln)] for j in range(D_F // ln)]
                for t in range(SC_CHUNK_TOKENS):
                    r = t * H_F + h
                    u = qc[0] * k_v[r, pl.ds(0, ln)]
                    for j in range(1, D_F // ln):
                        u = u + qc[j] * k_v[r, pl.ds(j * ln, ln)]
                    u_v[t, :] = u
                s = plsc.load_gather(u_v, [lane, jnp.zeros((ln,), jnp.int32)])
                for j in range(1, ln):
                    s = s + plsc.load_gather(u_v, [lane, jnp.full((ln,), j, jnp.int32)])
                s = s * FOX_SCALE + plsc.load_gather(b_v, [lane * H_F + h])
                m_old = m_v[pl.ds(h * ln, ln)]
                m_new = jnp.maximum(m_old, jnp.full((ln,), jnp.max(s), F32))
                alpha = jnp.exp(m_old - m_new)
                p = jnp.exp(s - m_new)
                l_v[pl.ds(h * ln, ln)] = l_v[pl.ds(h * ln, ln)] * alpha + jnp.full((ln,), jnp.sum(p), F32)
                m_v[pl.ds(h * ln, ln)] = m_new
                acc = [acc_v[pl.ds(h * D_F + j * ln, ln)] * alpha for j in range(D_F // ln)]
                for t in range(SC_CHUNK_TOKENS):
                    pt_splat = p.at[jnp.full((ln,), t, jnp.int32)].get(mode="promise_in_bounds")
                    r = t * H_F + h
                    for j in range(D_F // ln):
                        acc[j] = acc[j] + pt_splat * v_v[r, pl.ds(j * ln, ln)]
                for j in range(D_F // ln):
                    acc_v[pl.ds(h * D_F + j * ln, ln)] = acc[j]

            return carry

        lax.fori_loop(0, n_p * chunks_per_page, chunk, 0)
        pltpu.sync_copy(acc_v, acc_hbm.at[w])
        pltpu.sync_copy(m_v, m_hbm.at[w])
        pltpu.sync_copy(l_v, l_hbm.at[w])


def _sc_decode_partials(page_table, q, bias, k_pool, v_pool):
    n_seq, n_pages = page_table.shape
    n_workers = n_seq * SC_PARTS
    assert n_workers <= 32
    rows = SC_CHUNK_TOKENS * H_F
    mesh = plsc.VectorSubcoreMesh(core_axis_name="c", subcore_axis_name="s")
    f32 = lambda *shape: jax.ShapeDtypeStruct(shape, F32)
    run = pl.kernel(
        functools.partial(_sc_decode_body, n_workers=n_workers, n_pages=n_pages),
        out_type=[f32(n_workers, W_F), f32(n_workers, H_F * SC_LANES), f32(n_workers, H_F * SC_LANES)],
        mesh=mesh,
        scratch_types=[pltpu.VMEM((n_pages,), jnp.int32), pltpu.VMEM((W_F,), F32),
                       pltpu.VMEM((rows, D_F), F32), pltpu.VMEM((rows, D_F), F32),
                       pltpu.VMEM((rows,), F32), pltpu.VMEM((SC_LANES, SC_LANES), F32),
                       pltpu.VMEM((W_F,), F32),
                       pltpu.VMEM((H_F * SC_LANES,), F32), pltpu.VMEM((H_F * SC_LANES,), F32),
                       pltpu.SemaphoreType.DMA, pltpu.SemaphoreType.DMA, pltpu.SemaphoreType.DMA],
        compiler_params=pltpu.CompilerParams(needs_layout_passes=False),
        name="fox_decode_sparsecore",
    )
    return run(page_table, q, bias.reshape(-1, rows), k_pool.reshape(-1, D_F), v_pool.reshape(-1, D_F))


def _decode_combine_kernel(q_ref, kn_ref, vn_ref, acc_ref, m_ref, l_ref, o_ref):
    q = q_ref[0].astype(F32)
    s_new = jnp.sum(q * kn_ref[0].astype(F32), axis=-1, keepdims=True) * FOX_SCALE
    m_i = m_ref[0][:, :, 0:1]
    l_i = l_ref[0][:, :, 0:1]
    m_all = jnp.maximum(s_new, jnp.max(m_i, axis=0))
    w_i = jnp.exp(m_i - m_all[None])
    w_new = jnp.exp(s_new - m_all)
    num = jnp.sum(w_i * acc_ref[0], axis=0) + w_new * vn_ref[0].astype(F32)
    den = jnp.sum(w_i * l_i, axis=0) + w_new
    o_ref[0] = (num / den).astype(BF16)


def _decode_combine(q, k_new, v_new, acc, m, l):
    n_seq = q.shape[0]
    head_spec = pl.BlockSpec((1, H_F, D_F), lambda b: (b, 0, 0))
    stat_spec = pl.BlockSpec((1, SC_PARTS, H_F, SC_LANES), lambda b: (b, 0, 0, 0))
    return pl.pallas_call(
        _decode_combine_kernel,
        grid=(n_seq,),
        in_specs=[head_spec, head_spec, head_spec,
                  pl.BlockSpec((1, SC_PARTS, H_F, D_F), lambda b: (b, 0, 0, 0)), stat_spec, stat_spec],
        out_specs=head_spec,
        out_shape=jax.ShapeDtypeStruct((n_seq, H_F, D_F), BF16),
        compiler_params=_params(("parallel",)),
        name="fox_decode_combine",
    )(q, k_new, v_new, acc.reshape(n_seq, SC_PARTS, H_F, D_F),
      m.reshape(n_seq, SC_PARTS, H_F, SC_LANES), l.reshape(n_seq, SC_PARTS, H_F, SC_LANES))


def _merge_kernel(or_ref, of_ref, gr0_ref, gr1_ref, gf0_ref, gf1_ref, x_ref, eg_ref, eb_ref,
                  wr_ref, wf_ref, wo_ref, g1_ref, b1_ref, x1_ref, x1b_ref):
    tm = x_ref.shape[0]
    sub = min(tm, ROW_SUBTILE)
    for r in range(tm // sub):
        rows = slice(r * sub, (r + 1) * sub)
        y_r = jnp.dot(or_ref[rows, :], wr_ref[...], preferred_element_type=F32)
        y_f = jnp.dot(of_ref[rows, :], wf_ref[...], preferred_element_type=F32)
        halves = []
        for idx, (gr_ref, gf_ref) in enumerate(((gr0_ref, gf0_ref), (gr1_ref, gf1_ref))):
            sl = slice(idx * COL_BLOCK, (idx + 1) * COL_BLOCK)
            mixed = (_sigmoid(gr_ref[rows, :].astype(F32)) * y_r[:, sl]
                     + _sigmoid(gf_ref[rows, :].astype(F32)) * y_f[:, sl])
            halves.append(mixed.astype(BF16))
        mix = jnp.dot(jnp.concatenate(halves, axis=1), wo_ref[...], preferred_element_type=F32)
        h = _layer_norm(x_ref[rows, :], eg_ref[...], eb_ref[...])
        x1 = _layer_norm(ALPHA * h + mix, g1_ref[...], b1_ref[...])
        x1_ref[rows, :] = x1
        x1b_ref[rows, :] = x1.astype(BF16)


def _merge(o_r, o_f, z, x, eg, eb, w_r, w_f, w_o, g1, b1, tm):
    m = x.shape[0]
    row = lambda i: (i, 0)
    vec = pl.BlockSpec((1, D_MODEL), lambda i: (0, 0))
    zspec = lambda cb: pl.BlockSpec((tm, COL_BLOCK), lambda i: (i, cb))
    return pl.pallas_call(
        _merge_kernel,
        grid=(m // tm,),
        in_specs=[pl.BlockSpec((tm, W_R), row), pl.BlockSpec((tm, W_F), row),
                  zspec(CB_GR), zspec(CB_GR + 1), zspec(CB_GF), zspec(CB_GF + 1),
                  pl.BlockSpec((tm, D_MODEL), row), vec, vec,
                  _resident((W_R, D_MODEL)), _resident((W_F, D_MODEL)), _resident((D_MODEL, D_MODEL)),
                  vec, vec],
        out_specs=[pl.BlockSpec((tm, D_MODEL), row), pl.BlockSpec((tm, D_MODEL), row)],
        out_shape=[jax.ShapeDtypeStruct((m, D_MODEL), F32), jax.ShapeDtypeStruct((m, D_MODEL), BF16)],
        compiler_params=_params(("parallel",)),
        name="merge_out_ln1",
    )(o_r, o_f, z, z, z, z, x, eg, eb, w_r, w_f, w_o, g1, b1)


def _ffn_up(x_ref, w1_ref, w3_ref):
    xb = x_ref[...]
    a = jnp.dot(xb, w1_ref[...].astype(BF16), preferred_element_type=F32)
    b = jnp.dot(xb, w3_ref[...].astype(BF16), preferred_element_type=F32)
    return (a * _sigmoid(a) * b).astype(BF16)


def _ffn_down(mid, w2_ref, o_ref):
    o_ref[...] += jnp.dot(mid, w2_ref[...].astype(BF16), preferred_element_type=F32)


def _ffn_tile(x_ref, w1_ref, w3_ref, w2_ref, o_ref):
    _ffn_down(_ffn_up(x_ref, w1_ref, w3_ref), w2_ref, o_ref)


def _ffn_kernel(x_ref, w1_ref, w3_ref, w2_ref, o_ref):
    @pl.when(pl.program_id(1) == 0)
    def _():
        o_ref[...] = jnp.zeros_like(o_ref)

    _ffn_tile(x_ref, w1_ref, w3_ref, w2_ref, o_ref)


def _ffn_decode_kernel(pt_ref, x_ref, w1_ref, w3_ref, w2_ref, q_ref, kn_ref, vn_ref, bias_ref, *refs,
                       pages, n_groups):
    o_ref = refs[2 * pages]
    d = _DecodeRefs(q_ref, kn_ref, vn_ref, bias_ref, refs[:pages], refs[pages:2 * pages], *refs[2 * pages + 1:])
    j = pl.program_id(1)
    g = (pl.program_id(0) * pl.num_programs(1) + j) % n_groups

    @pl.when(j == 0)
    def _():
        o_ref[...] = jnp.zeros_like(o_ref)

    pl.when(g == 0)(lambda: _decode_begin(d))
    mid = _ffn_up(x_ref, w1_ref, w3_ref)
    s_parts = _decode_scores(d)
    _ffn_down(mid, w2_ref, o_ref)
    _decode_update(d, s_parts)
    pl.when(g == n_groups - 1)(lambda: _decode_end(d))


def _ffn(x1b, w1, w3, w2, tm, tf):
    m = x1b.shape[0]
    return pl.pallas_call(
        _ffn_kernel,
        grid=(m // tm, D_FF // tf),
        in_specs=[pl.BlockSpec((tm, D_MODEL), lambda i, j: (i, 0)),
                  pl.BlockSpec((D_MODEL, tf), lambda i, j: (0, j)),
                  pl.BlockSpec((D_MODEL, tf), lambda i, j: (0, j)),
                  pl.BlockSpec((tf, D_MODEL), lambda i, j: (j, 0))],
        out_specs=pl.BlockSpec((tm, D_MODEL), lambda i, j: (i, 0)),
        out_shape=jax.ShapeDtypeStruct((m, D_MODEL), F32),
        compiler_params=_params(("parallel", "arbitrary")),
        name="swiglu_ffn_rows",
    )(x1b, w1, w3, w2)


def _ffn_with_decode(x1b, w1, w3, w2, tm, tf, page_table, q, k_new, v_new, bias, k_pool, v_pool,
                     pages=DECODE_PAGES_PER_STEP):
    m = x1b.shape[0]
    nj = D_FF // tf
    n_groups = page_table.shape[1] // pages
    n_seq = (m // tm) * nj // n_groups
    assert n_seq * n_groups == (m // tm) * nj and n_seq <= page_table.shape[0]

    def seq_group(i, j):
        s = i * nj + j
        return s // n_groups, s % n_groups

    dec_in, dec_out, dec_scratch = _decode_specs(pages, seq_group)
    return pl.pallas_call(
        functools.partial(_ffn_decode_kernel, pages=pages, n_groups=n_groups),
        grid_spec=pltpu.PrefetchScalarGridSpec(
            num_scalar_prefetch=1,
            grid=(m // tm, nj),
            in_specs=[pl.BlockSpec((tm, D_MODEL), lambda i, j, pt: (i, 0), pipeline_mode=pl.Buffered(1)),
                      pl.BlockSpec((D_MODEL, tf), lambda i, j, pt: (0, j)),
                      pl.BlockSpec((D_MODEL, tf), lambda i, j, pt: (0, j)),
                      pl.BlockSpec((tf, D_MODEL), lambda i, j, pt: (j, 0))] + dec_in,
            out_specs=[pl.BlockSpec((tm, D_MODEL), lambda i, j, pt: (i, 0)), dec_out],
            scratch_shapes=dec_scratch),
        out_shape=[jax.ShapeDtypeStruct((m, D_MODEL), F32),
                   jax.ShapeDtypeStruct((n_seq, H_F, D_F), BF16)],
        compiler_params=_params(("arbitrary", "arbitrary")),
        name="swiglu_ffn_decode",
    )(page_table, x1b, w1, w3, w2, q, k_new, v_new, bias, *([k_pool] * pages), *([v_pool] * pages))


def _ple_kernel(x1_ref, x1b_ref, ffn_ref, p_ref, wpg_ref, wpe_ref, g2_ref, b2_ref, y_ref):
    tm = x1_ref.shape[0]
    sub = min(tm, ROW_SUBTILE)
    for r in range(tm // sub):
        rows = slice(r * sub, (r + 1) * sub)
        pg = jnp.dot(x1b_ref[rows, :], wpg_ref[...], preferred_element_type=F32)
        pe = jnp.dot(p_ref[rows, :].astype(BF16), wpe_ref[...], preferred_element_type=F32)
        y_ref[rows, :] = _layer_norm(ALPHA * x1_ref[rows, :] + ffn_ref[rows, :] + _sigmoid(pg) * pe,
                                     g2_ref[...], b2_ref[...])


def _ple(x1, x1b, ffn, p, w_pg, w_pe, g2, b2, tm):
    m = x1.shape[0]
    row = lambda i: (i, 0)
    vec = pl.BlockSpec((1, D_MODEL), lambda i: (0, 0))
    return pl.pallas_call(
        _ple_kernel,
        grid=(m // tm,),
        in_specs=[pl.BlockSpec((tm, D_MODEL), row), pl.BlockSpec((tm, D_MODEL), row),
                  pl.BlockSpec((tm, D_MODEL), row), pl.BlockSpec((tm, D_PLE), row),
                  _resident((D_MODEL, D_MODEL)), _resident((D_PLE, D_MODEL)), vec, vec],
        out_specs=pl.BlockSpec((tm, D_MODEL), row),
        out_shape=jax.ShapeDtypeStruct((m, D_MODEL), F32),
        compiler_params=_params(("parallel",)),
        name="ple_ln2",
    )(x1, x1b, ffn, p, w_pg, w_pe, g2, b2)


def _rope_tables(pos):
    half = DK_R // 2
    inv = ROPE_BASE ** (-jnp.arange(half, dtype=F32) / half)
    ang = pos.astype(F32)[:, None] * inv[None, :]
    return jnp.cos(ang), jnp.sin(ang)


def kernel(x_prompt, x_sample, state_ret, cache_k, cache_v, cache_logf, page_table, p_prompt, p_sample, ln_emb_g, ln_emb_b, w_in, b_f, gn_g, w_ret_proj, w_fox_proj, w_out, ln1_g, ln1_b, w1, w3, w2, w_pg, w_pe, ln2_g, ln2_b):
    assert w_in.shape[0] == DEPTH == 1
    batch, seq, _ = x_prompt.shape
    dec_b, dec_t, _ = x_sample.shape
    assert dec_t == 1
    n_pages = page_table.shape[1]
    past_len = n_pages * PAGE_SIZE
    n_pool = cache_k.shape[1]

    w_in_t = jnp.swapaxes(w_in[0], 0, 1)
    w_ff_t = w_in_t[FF_LO:FF_HI].astype(BF16)
    w_ff_pad = jnp.pad(w_ff_t, ((0, W_FF_PAD - H_F), (0, 0)))
    bf_col = b_f[0].reshape(H_F, 1)
    vec = lambda v: v.reshape(1, D_MODEL)
    eg, eb = vec(ln_emb_g), vec(ln_emb_b)
    g1, b1, g2, b2 = vec(ln1_g[0]), vec(ln1_b[0]), vec(ln2_g[0]), vec(ln2_b[0])
    w_r, w_f, w_o = w_ret_proj[0].astype(BF16), w_fox_proj[0].astype(BF16), w_out[0].astype(BF16)
    w_pgb, w_peb = w_pg[0].astype(BF16), w_pe[0].astype(BF16)
    log_g = jnp.log1p(-jnp.exp2(-5.0 - jnp.arange(H_R, dtype=F32)))
    cos_p, sin_p = _rope_tables(jnp.arange(seq))
    cos_s, sin_s = _rope_tables(jnp.full((dec_b,), past_len))

    xp = x_prompt.reshape(batch * seq, D_MODEL)
    xs = x_sample.reshape(dec_b, D_MODEL)

    h_p = _entry_ln(xp, eg, eb, tm=512)
    z_p, fk_p, fv_p, ff_p, z_s, fk_s, fv_s = _inproj(h_p, xs, eg, eb, w_in_t, w_ff_pad,
                                                     cos_p, sin_p, cos_s, sin_s, tm=IN_TM)
    lf_p, c2_row = _logf_prompt(ff_p, bf_col, seq)
    lf_s = _logf_sample(xs, eg, eb, w_ff_t, bf_col)

    o_r_p, st_p = _ret_prompt(z_p, log_g, gn_g[0], batch, seq)
    o_f_p = _fox_prompt(z_p, c2_row, batch, seq)

    z3 = z_s.reshape(dec_b, 1, z_s.shape[1])
    o_r_s, st_s = _ret_sample(z3, log_g, gn_g[0], state_ret[0])
    width = PAGE_SIZE * H_F
    lf_new = jnp.swapaxes(lf_s, 1, 2).reshape(dec_b, H_F)
    lnew = jnp.tile(lf_new, (1, PAGE_SIZE)).reshape(dec_b, 1, width)
    bias = _decode_bias(page_table, cache_logf[0].reshape(n_pool, width), lnew)
    heads = lambda cb: z_s[:, cb * COL_BLOCK:(cb + 1) * COL_BLOCK].reshape(dec_b, H_F, D_F)
    decode_args = (heads(CB_FQ), heads(CB_FK), heads(CB_FV), bias,
                   cache_k[0].reshape(n_pool, width, D_F), cache_v[0].reshape(n_pool, width, D_F))

    x1_p, x1b_p = _merge(o_r_p, o_f_p, z_p, xp, eg, eb, w_r, w_f, w_o, g1, b1, tm=MERGE_TM)
    ffn_p, o_f_ride = _ffn_with_decode(x1b_p, w1[0], w3[0], w2[0], FFN_TM, FFN_TF, page_table, *decode_args)
    n_ride = o_f_ride.shape[0]
    q_r, kn_r, vn_r, bias_r = (a[n_ride:] for a in decode_args[:4])
    parts = _sc_decode_partials(page_table[n_ride:], q_r.reshape(-1, W_F).astype(F32), bias_r, *decode_args[4:])
    o_f_rest = _decode_combine(q_r, kn_r, vn_r, *parts)
    o_f_s = jnp.concatenate([o_f_ride, o_f_rest], axis=0)
    x1_s, x1b_s = _merge(o_r_s.reshape(dec_b, W_R), o_f_s.reshape(dec_b, W_F), z_s, xs, eg, eb,
                         w_r, w_f, w_o, g1, b1, tm=dec_b)
    ffn_s = _ffn(x1b_s, w1[0], w3[0], w2[0], tm=dec_b, tf=FFN_TF)
    y_p = _ple(x1_p, x1b_p, ffn_p, p_prompt[0].reshape(batch * seq, D_PLE), w_pgb, w_peb, g2, b2, tm=PLE_TM)
    y_s = _ple(x1_s, x1b_s, ffn_s, p_sample[0].reshape(dec_b, D_PLE), w_pgb, w_peb, g2, b2, tm=dec_b)

    return (y_p.reshape(batch, seq, D_MODEL),
            y_s.reshape(dec_b, 1, D_MODEL),
            st_p[None],
            fk_p.reshape(1, batch, seq, H_F, D_F),
            fv_p.reshape(1, batch, seq, H_F, D_F),
            jnp.swapaxes(lf_p, 1, 2)[None],
            st_s[None],
            fk_s.reshape(1, dec_b, 1, H_F, D_F),
            fv_s.reshape(1, dec_b, 1, H_F, D_F),
            jnp.swapaxes(lf_s, 1, 2).reshape(1, dec_b, 1, H_F))
```
